```python
import math
import jax, jax.numpy as jnp
from jax import lax
import numpy as np

D_MODEL = 1024
BATCH = 4
SEQ = 8192
DEPTH = 1

CHUNK = 64
Q_BLOCK = 128
EPS = 1e-6
D_FF = 2816
A_HEADS = 8
A_HEAD_DIM = 64
A_Q_W = A_HEADS * A_HEAD_DIM
IDX_HEADS = 8
IDX_DIM = 64
TOPK_MAX = 256
B_HEADS = 8
QK_NOPE = 64
QK_ROPE = 32
V_DIM = 64
Q_LORA = 384
KV_LORA = 256
B_V_W = B_HEADS * V_DIM
ROPE_THETA = 10000.0
NUM_BUCKETS = 32
MAX_DISTANCE = 128
IN_SIZES = (A_Q_W, A_HEAD_DIM, A_HEAD_DIM, IDX_HEADS * IDX_DIM, IDX_DIM, IDX_HEADS,
            Q_LORA, KV_LORA, QK_ROPE, 2 * D_MODEL)
IN_COLS = sum(IN_SIZES)
N_ADA = 9

kernel_name = "hybrid_dsa_mla_macaron_block"


def rmsnorm(x, g):
    xf = x.astype(jnp.float32)
    y = xf * lax.rsqrt(jnp.mean(xf * xf, axis=-1, keepdims=True) + EPS)
    return (y * g.astype(jnp.float32)).astype(x.dtype)


def modulate(h, shift, scale):
    return h * (1 + scale[:, None, :]) + shift[:, None, :]


def swiglu(h, w_in, w_down):
    g, u = jnp.split(h @ w_in, 2, axis=-1)
    return (jax.nn.silu(g) * u) @ w_down


def rope_angles(positions):
    half = QK_ROPE // 2
    freqs = ROPE_THETA ** (-2.0 * jnp.arange(half, dtype=jnp.float32) / QK_ROPE)
    ang = positions.astype(jnp.float32)[..., None] * freqs
    return jnp.cos(ang), jnp.sin(ang)


def apply_rope(x, cos, sin):
    cos = cos.astype(x.dtype)
    sin = sin.astype(x.dtype)
    x1, x2 = jnp.split(x, 2, axis=-1)
    return jnp.concatenate([x1 * cos - x2 * sin, x2 * cos + x1 * sin], axis=-1)


def t5_bucket(rel):
    half = NUM_BUCKETS // 2
    max_exact = half // 2
    ret = jnp.where(rel > 0, half, 0)
    n = jnp.abs(rel)
    nf = jnp.maximum(n, 1).astype(jnp.float32)
    large = max_exact + (jnp.log(nf / max_exact) / math.log(MAX_DISTANCE / max_exact)
                         * (half - max_exact)).astype(jnp.int32)
    large = jnp.minimum(large, half - 1)
    return ret + jnp.where(n < max_exact, n, large)


def to_blocks(a, nb):
    return jnp.swapaxes(a.reshape(a.shape[0], nb, Q_BLOCK, *a.shape[2:]), 0, 1)


def from_blocks(o):
    o = jnp.swapaxes(o, 0, 1)
    return o.reshape(o.shape[0], o.shape[1] * o.shape[2], -1)


def block_limit(t0):
    tq = t0 + jnp.arange(Q_BLOCK, dtype=jnp.int32)
    return (tq // CHUNK + 1) * CHUNK


def dsa_attention(q_a, k_a, v_a, q_idx, k_idx, w_idx, positions, rel_bias, topk):
    B, S = q_a.shape[0], q_a.shape[1]
    nb = S // Q_BLOCK
    key_ids = jnp.arange(S, dtype=jnp.int32)
    gather = jax.vmap(lambda a, i: a[i])
    idx_scale = IDX_DIM ** -0.5
    head_w_scale = IDX_HEADS ** -0.5
    attn_scale = A_HEAD_DIM ** -0.5
    k_idx_f = k_idx.astype(jnp.float32)

    def body(args):
        q_b, qi_b, wi_b, qpos_b, t0 = args
        limit = block_limit(t0)
        dots = jnp.einsum('bqhd,bsd->bqhs', qi_b.astype(jnp.float32), k_idx_f) * idx_scale
        score = jnp.einsum('bqh,bqhs->bqs', wi_b.astype(jnp.float32) * head_w_scale,
                           jax.nn.relu(dots))
        admissible = key_ids[None, :] < limit[:, None]
        score = jnp.where(admissible[None], score, -jnp.inf)
        _, sel = lax.top_k(score, topk)
        valid = sel < limit[None, :, None]
        k_sel = gather(k_a, sel)
        v_sel = gather(v_a, sel)
        pos_sel = gather(positions, sel)
        bias = rel_bias[t5_bucket(pos_sel - qpos_b[..., None])]
        s = jnp.einsum('bqhd,bqkd->bqhk', q_b, k_sel).astype(jnp.float32) * attn_scale
        s = s + jnp.moveaxis(bias, -1, 2).astype(jnp.float32)
        s = jnp.where(valid[:, :, None, :], s, -jnp.inf)
        p = jax.nn.softmax(s, axis=-1).astype(v_sel.dtype)
        return jnp.einsum('bqhk,bqkd->bqhd', p, v_sel)

    t0s = jnp.arange(nb, dtype=jnp.int32) * Q_BLOCK
    o = lax.map(body, (to_blocks(q_a, nb), to_blocks(q_idx, nb), to_blocks(w_idx, nb),
                       to_blocks(positions, nb), t0s))
    return from_blocks(o)


def mla_attention(q_nope, q_pe, k_nope, k_pe, v):
    S = q_nope.shape[1]
    nb = S // Q_BLOCK
    key_ids = jnp.arange(S, dtype=jnp.int32)
    scale = (QK_NOPE + QK_ROPE) ** -0.5

    def body(args):
        qn, qp, t0 = args
        limit = block_limit(t0)
        s = (jnp.einsum('bqhd,bshd->bhqs', qn, k_nope)
             + jnp.einsum('bqhr,bsr->bhqs', qp, k_pe)).astype(jnp.float32) * scale
        mask = key_ids[None, :] < limit[:, None]
        s = jnp.where(mask[None, None], s, -jnp.inf)
        p = jax.nn.softmax(s, axis=-1).astype(v.dtype)
        return jnp.einsum('bhqs,bshd->bqhd', p, v)

    t0s = jnp.arange(nb, dtype=jnp.int32) * Q_BLOCK
    o = lax.map(body, (to_blocks(q_nope, nb), to_blocks(q_pe, nb), t0s))
    return from_blocks(o)


def setup_inputs(seed: int = 0) -> dict:
    key = jax.random.key(seed)
    ks = jax.random.split(key, 32)
    f32 = jnp.float32

    def nrm(k, shape, fan_in, mult=1.0):
        return jax.random.normal(k, shape, f32) * (mult * fan_in ** -0.5)

    def gain(k, shape):
        return 1.0 + 0.05 * jax.random.normal(k, shape, f32)

    x = jax.random.normal(ks[0], (BATCH, SEQ, D_MODEL), f32)
    c = jax.random.normal(ks[1], (BATCH, D_MODEL), f32)
    start = jax.random.randint(ks[2], (BATCH, 1), 0, 4096, dtype=jnp.int32)
    positions = start + jnp.arange(SEQ, dtype=jnp.int32)[None, :]
    L = DEPTH
    return {
        "x": x,
        "c": c,
        "positions": positions,
        "w_ada": nrm(ks[3], (L, D_MODEL, N_ADA * D_MODEL), D_MODEL, 0.5),
        "b_ada": 0.02 * jax.random.normal(ks[4], (L, N_ADA * D_MODEL), f32),
        "g_ffn1": gain(ks[5], (L, D_MODEL)),
        "w_ffn1_in": nrm(ks[6], (L, D_MODEL, 2 * D_FF), D_MODEL),
        "w_ffn1_down": nrm(ks[7], (L, D_FF, D_MODEL), D_FF),
        "g_mix": gain(ks[8], (L, D_MODEL)),
        "w_in": nrm(ks[9], (L, D_MODEL, IN_COLS), D_MODEL),
        "g_cq": gain(ks[10], (L, Q_LORA)),
        "w_uq": nrm(ks[11], (L, Q_LORA, B_HEADS * (QK_NOPE + QK_ROPE)), Q_LORA),
        "g_ckv": gain(ks[12], (L, KV_LORA)),
        "w_uk": nrm(ks[13], (L, KV_LORA, B_HEADS * QK_NOPE), KV_LORA),
        "w_uv": nrm(ks[14], (L, KV_LORA, B_HEADS * V_DIM), KV_LORA),
        "rel_bias": 0.5 * jax.random.normal(ks[15], (NUM_BUCKETS, A_HEADS), f32),
        "w_o_a": nrm(ks[16], (L, A_Q_W, D_MODEL), A_Q_W),
        "w_o_b": nrm(ks[17], (L, B_V_W, D_MODEL), B_V_W),
        "w_out": nrm(ks[18], (L, D_MODEL, D_MODEL), D_MODEL),
        "g_ffn2": gain(ks[19], (L, D_MODEL)),
        "w_ffn2_in": nrm(ks[20], (L, D_MODEL, 2 * D_FF), D_MODEL),
        "w_ffn2_down": nrm(ks[21], (L, D_FF, D_MODEL), D_FF),
        "g_final": gain(ks[22], (D_MODEL,)),
    }


def reference(x, c, positions, w_ada, b_ada, g_ffn1, w_ffn1_in, w_ffn1_down, g_mix, w_in,
              g_cq, w_uq, g_ckv, w_uk, w_uv, rel_bias, w_o_a, w_o_b, w_out,
              g_ffn2, w_ffn2_in, w_ffn2_down, g_final):
    B, S, _ = x.shape
    topk = min(TOPK_MAX, S // 4)
    offsets = []
    acc = 0
    for n in IN_SIZES[:-1]:
        acc += n
        offsets.append(acc)
    cos, sin = rope_angles(positions)

    for l in range(DEPTH):
        mod = jax.nn.silu(c) @ w_ada[l] + b_ada[l]
        sh1, sc1, gt1, sh2, sc2, gt2, sh3, sc3, gt3 = jnp.split(mod, N_ADA, axis=-1)

        h = modulate(rmsnorm(x, g_ffn1[l]), sh1, sc1)
        x = x + 0.5 * gt1[:, None, :] * swiglu(h, w_ffn1_in[l], w_ffn1_down[l])

        h = modulate(rmsnorm(x, g_mix[l]), sh2, sc2)
        proj = h @ w_in[l]
        (q_a, k_a, v_a, q_idx, k_idx, w_idx, c_q, c_kv, k_rope,
         gate_logits) = jnp.split(proj, offsets, axis=-1)

        o_a = dsa_attention(q_a.reshape(B, S, A_HEADS, A_HEAD_DIM), k_a, v_a,
                            q_idx.reshape(B, S, IDX_HEADS, IDX_DIM), k_idx, w_idx,
                            positions, rel_bias, topk)

        q_b = (rmsnorm(c_q, g_cq[l]) @ w_uq[l]).reshape(B, S, B_HEADS, QK_NOPE + QK_ROPE)
        q_nope, q_pe = jnp.split(q_b, [QK_NOPE], axis=-1)
        c_kv_n = rmsnorm(c_kv, g_ckv[l])
        k_nope = (c_kv_n @ w_uk[l]).reshape(B, S, B_HEADS, QK_NOPE)
        v_b = (c_kv_n @ w_uv[l]).reshape(B, S, B_HEADS, V_DIM)
        q_pe = apply_rope(q_pe, cos[:, :, None, :], sin[:, :, None, :])
        k_pe = apply_rope(k_rope, cos, sin)
        o_b = mla_attention(q_nope, q_pe, k_nope, k_pe, v_b)

        g_a, g_b = jnp.split(jax.nn.sigmoid(gate_logits), 2, axis=-1)
        y = g_a * (o_a @ w_o_a[l]) + g_b * (o_b @ w_o_b[l])
        x = x + gt2[:, None, :] * (y @ w_out[l])

        h = modulate(rmsnorm(x, g_ffn2[l]), sh3, sc3)
        x = x + 0.5 * gt3[:, None, :] * swiglu(h, w_ffn2_in[l], w_ffn2_down[l])

    return rmsnorm(x, g_final)
```

```python
import functools
import math

import jax
import jax.numpy as jnp
from jax import lax
from jax.experimental import pallas as pl
from jax.experimental.pallas import tpu as pltpu

F32 = jnp.float32
BF16 = jnp.bfloat16
I32 = jnp.int32

CHUNK = 64
EPS = 1e-6
A_HEADS = 8
A_HEAD_DIM = 64
IDX_HEADS = 8
IDX_DIM = 64
TOPK_MAX = 256
B_HEADS = 8
QK_NOPE = 64
QK_ROPE = 32
V_DIM = 64
Q_LORA = 384
KV_LORA = 256
ROPE_THETA = 10000.0
NUM_BUCKETS = 32
MAX_DISTANCE = 128
N_ADA = 9

LANES = 128
VMEM_LIMIT = 56 * 1024 * 1024
INT_MIN = -2 ** 31
NEG_BIG = -1e30

C_QA = 0
C_QI = 512
C_SM = 1024
C_WI = 1664
C_KR = 1792
C_KT = 1920
C_CQ = 2048
C_CKV = C_CQ + Q_LORA
C_GT = C_CKV + KV_LORA


def _dot(a, b):
    return jnp.dot(a, b, preferred_element_type=F32)


def _dot_nt(a, b):
    return lax.dot_general(a, b, (((1,), (1,)), ((), ())), preferred_element_type=F32)


def _rep(x, reps):
    return jnp.tile(x, (1, reps))


def _rms(x, g):
    return x * lax.rsqrt(jnp.mean(x * x, axis=-1, keepdims=True) + EPS) * g


def _const_spec(shape):
    nd = len(shape)
    return pl.BlockSpec(shape, lambda *_: (0,) * nd, pipeline_mode=pl.Buffered(1))


def _ada_kernel(c_ref, w_ref, b_ref, o_ref):
    c = c_ref[...]
    s = c * jax.nn.sigmoid(c)
    o_ref[...] = _dot(s.astype(BF16), w_ref[...].astype(BF16)) + b_ref[...]


def _ada(c_pad, w_ada, b_ada):
    d = c_pad.shape[1]
    n_out = w_ada.shape[1]
    return pl.pallas_call(
        _ada_kernel,
        grid=(n_out // d,),
        in_specs=[pl.BlockSpec(c_pad.shape, lambda j: (0, 0)),
                  pl.BlockSpec((d, d), lambda j: (0, j)),
                  pl.BlockSpec((1, d), lambda j: (0, j))],
        out_specs=pl.BlockSpec(c_pad.shape, lambda j: (0, j)),
        out_shape=jax.ShapeDtypeStruct((c_pad.shape[0], n_out), F32),
        compiler_params=pltpu.CompilerParams(vmem_limit_bytes=VMEM_LIMIT),
        name="ada",
    )(c_pad, w_ada, b_ada.reshape(1, n_out))


def _swiglu_tile(hb, win_ref, wdn_ref, d_ff, chunk):
    acc = None
    for j in range(d_ff // chunk):
        g = _dot(hb, win_ref[:, j * chunk:(j + 1) * chunk])
        u = _dot(hb, win_ref[:, d_ff + j * chunk:d_ff + (j + 1) * chunk])
        a = (g * jax.nn.sigmoid(g) * u).astype(BF16)
        part = _dot(a, wdn_ref[j * chunk:(j + 1) * chunk, :])
        acc = part if acc is None else acc + part
    return acc


def _ffn_chunk(d_ff):
    for c in (512, 256, 128):
        if d_ff % c == 0:
            return c
    raise ValueError("d_ff must be a multiple of 128")


def _ffn1_kernel(x_ref, mod_ref, g_ref, win_ref, wdn_ref, o_ref, *, d_ff, chunk):
    x = x_ref[...]
    sh, sc, gt = mod_ref[0, 0:1, :], mod_ref[0, 1:2, :], mod_ref[0, 2:3, :]
    h = _rms(x, g_ref[...]) * (1.0 + sc) + sh
    y = _swiglu_tile(h.astype(BF16), win_ref, wdn_ref, d_ff, chunk)
    o_ref[...] = x + (0.5 * gt) * y


def _ffn1(x2, mod3, g, w_in, w_down, seq, tm):
    n, d = x2.shape
    d_ff = w_down.shape[0]
    tiles_per_batch = seq // tm
    kern = functools.partial(_ffn1_kernel, d_ff=d_ff, chunk=_ffn_chunk(d_ff))
    return pl.pallas_call(
        kern,
        grid=(n // tm,),
        in_specs=[pl.BlockSpec((tm, d), lambda i: (i, 0)),
                  pl.BlockSpec((1, N_ADA, d), lambda i: (i // tiles_per_batch, 0, 0)),
                  _const_spec((1, d)),
                  _const_spec(w_in.shape),
                  _const_spec(w_down.shape)],
        out_specs=pl.BlockSpec((tm, d), lambda i: (i, 0)),
        out_shape=jax.ShapeDtypeStruct((n, d), F32),
        compiler_params=pltpu.CompilerParams(vmem_limit_bytes=VMEM_LIMIT),
        name="ffn1",
    )(x2, mod3, g, w_in, w_down)


def _proj_kernel(x_ref, mod_ref, g_ref, w_ref, gcq_ref, wq_ref, wqr_ref, gckv_ref,
                 wk_ref, wv_ref, cs_ref, sn_ref,
                 qa_ref, qi_ref, sm_ref, wi_ref, qb_ref, kb_ref, vb_ref, gt_ref, *, d):
    x = x_ref[...]
    sh, sc = mod_ref[0, 3:4, :], mod_ref[0, 4:5, :]
    hb = (_rms(x, g_ref[...]) * (1.0 + sc) + sh).astype(BF16)
    cs = cs_ref[...]
    sn = sn_ref[...]
    lane = lax.broadcasted_iota(I32, (1, LANES), 1)
    one_at_v = (lane == V_DIM).astype(F32)

    qa_ref[...] = _dot(hb, w_ref[:, C_QA:C_QA + 512]).astype(BF16)
    qi_ref[...] = _dot(hb, w_ref[:, C_QI:C_QI + 512]).astype(BF16)

    sm = _dot(hb, w_ref[:, C_SM:C_SM + 640])
    sm_ref[:, 0:256] = sm[:, 0:256].astype(BF16)
    sm_ref[:, 256:384] = (sm[:, 256:384] + one_at_v).astype(BF16)
    sm_ref[:, 384:640] = sm[:, 384:640].astype(BF16)

    misc = _dot(hb, w_ref[:, C_WI:C_WI + 384])
    wi_ref[...] = misc[:, 0:128]
    kpe = misc[:, 128:256] * cs + misc[:, 256:384] * sn

    lat = _dot(hb, w_ref[:, C_CQ:C_GT])
    cq = _rms(lat[:, 0:Q_LORA], gcq_ref[...]).astype(BF16)
    ckv = _rms(lat[:, Q_LORA:Q_LORA + KV_LORA], gckv_ref[...]).astype(BF16)
    qn = _dot(cq, wq_ref[...])
    qr = _dot(cq, wqr_ref[...])
    kn = _dot(ckv, wk_ref[...])
    vv = _dot(ckv, wv_ref[...])
    for h in range(B_HEADS):
        sl = slice(h * LANES, (h + 1) * LANES)
        qb_ref[:, sl] = (qn[:, sl] * cs + qr[:, sl] * sn).astype(BF16)
        kb_ref[:, sl] = (kn[:, sl] + kpe).astype(BF16)
        vb_ref[:, sl] = (vv[:, sl] + one_at_v).astype(BF16)

    gl = _dot(hb, w_ref[:, C_GT:C_GT + 2 * d])
    gt_ref[...] = jax.nn.sigmoid(gl).astype(BF16)


def _proj(x1, mod3, g, w_all, gcq, wq, wqr, gckv, wk, wv, cs, sn, seq, tm):
    n, d = x1.shape
    tiles_per_batch = seq // tm
    row = lambda w: pl.BlockSpec((tm, w), lambda i: (i, 0))
    out_widths = (512, 512, 640, 128, 1024, 1024, 1024, 2 * d)
    out_dtypes = (BF16, BF16, BF16, F32, BF16, BF16, BF16, BF16)
    return pl.pallas_call(
        functools.partial(_proj_kernel, d=d),
        grid=(n // tm,),
        in_specs=[row(d),
                  pl.BlockSpec((1, N_ADA, d), lambda i: (i // tiles_per_batch, 0, 0)),
                  _const_spec((1, d)), _const_spec(w_all.shape),
                  _const_spec(gcq.shape), _const_spec(wq.shape), _const_spec(wqr.shape),
                  _const_spec(gckv.shape), _const_spec(wk.shape), _const_spec(wv.shape),
                  row(LANES), row(LANES)],
        out_specs=[row(w) for w in out_widths],
        out_shape=[jax.ShapeDtypeStruct((n, w), dt) for w, dt in zip(out_widths, out_dtypes)],
        compiler_params=pltpu.CompilerParams(vmem_limit_bytes=VMEM_LIMIT),
        name="proj",
    )(x1, mod3, g, w_all, gcq, wq, wqr, gckv, wk, wv, cs, sn)


def _t5_bias_tile(posq, posk, tab_ref, heads):
    half = NUM_BUCKETS // 2
    max_exact = half // 2
    rel = posk - posq
    n = jnp.abs(rel)
    nf = jnp.maximum(n, 1).astype(F32)
    large = max_exact + (jnp.log(nf / max_exact) / math.log(MAX_DISTANCE / max_exact)
                         * (half - max_exact)).astype(I32)
    large = jnp.minimum(large, half - 1)
    bucket = jnp.where(rel > 0, half, 0) + jnp.where(n < max_exact, n, large)
    tq, tk = bucket.shape
    out = []
    for h in range(heads):
        tab = jnp.broadcast_to(tab_ref[h:h + 1, :], (tq, LANES))
        cols = [jnp.take_along_axis(tab, bucket[:, c * LANES:(c + 1) * LANES], axis=1,
                                    mode="promise_in_bounds")
                for c in range(tk // LANES)]
        out.append(jnp.concatenate(cols, axis=1))
    return out


def _dsa_kernel(pmin_ref, pmax_ref,
                qa_ref, qi_ref, wi_ref, sm_ref, posq_ref, posk_ref, tab_ref, far_ref,
                o_ref,
                keys_ref, wb_ref, tau_ref, m_ref, acc_ref, *, tq, topk, nq):
    b = pl.program_id(0)
    i = pl.program_id(1)
    tk = tq
    reps = tk // LANES
    n_tiles = i + 1

    wi = wi_ref[...]
    for h in range(IDX_HEADS):
        wb_ref[h] = jnp.broadcast_to(wi[:, h:h + 1], (tq, LANES))

    row_l = lax.broadcasted_iota(I32, (tq, tk), 0)
    col_l = lax.broadcasted_iota(I32, (tq, tk), 1)
    diag_ok = col_l < (row_l // CHUNK + 1) * CHUNK

    def score_tile(j, diag):
        r0 = pl.multiple_of(j * tk, tk)
        ki_lo = sm_ref[pl.ds(r0, tk), 384:512]
        ki_hi = sm_ref[pl.ds(r0, tk), 512:640]
        score = jnp.zeros((tq, tk), F32)
        for p in range(IDX_HEADS // 2):
            qp = qi_ref[:, p * LANES:(p + 1) * LANES]
            for h, kk in ((2 * p, ki_lo), (2 * p + 1, ki_hi)):
                d_ = _dot_nt(qp, kk)
                w_ = _rep(wb_ref[h], reps)
                score = score + w_ * jnp.maximum(d_, 0.0)
        bits = lax.bitcast_convert_type(score + 0.0, I32)
        key = bits ^ ((bits >> 31) & 0x7FFFFFFF)
        if diag:
            key = jnp.where(diag_ok, key, INT_MIN)
        keys_ref[:, pl.ds(r0, tk)] = key

    lax.fori_loop(0, i, lambda j, c: (score_tile(j, False), c)[1], 0)
    score_tile(i, True)

    def count_tiles(pred):
        def body(j, acc):
            r0 = pl.multiple_of(j * tk, tk)
            k_ = keys_ref[:, pl.ds(r0, tk)]
            for c in range(reps):
                acc = acc + jnp.where(pred(k_[:, c * LANES:(c + 1) * LANES], j, c), 1, 0)
            return acc
        acc = lax.fori_loop(0, n_tiles, body, jnp.zeros((tq, LANES), I32))
        return jnp.broadcast_to(jnp.sum(acc, axis=1, keepdims=True), (tq, LANES))

    def bit_step(p, prefix):
        cand = prefix ^ jnp.left_shift(jnp.int32(1), 31 - p)
        cnt = count_tiles(lambda k_, j, c: k_ >= cand)
        return jnp.where(cnt >= topk, cand, prefix)

    tau = lax.fori_loop(0, 32, bit_step, jnp.full((tq, LANES), INT_MIN, I32))
    cnt_ge = count_tiles(lambda k_, j, c: k_ >= tau)
    tau_ref[...] = jnp.maximum(tau, INT_MIN + 1)

    n_excess = jnp.max(jnp.where(tau > INT_MIN, cnt_ge, topk)) - topk

    @pl.when(n_excess > 0)
    def _ties():
        cnt_gt = count_tiles(lambda k_, j, c: k_ > tau)
        room = topk - cnt_gt
        lane = lax.broadcasted_iota(I32, (tq, LANES), 1)

        def idx_step(p, jmax):
            cand = jmax | jnp.left_shift(jnp.int32(1), 14 - p)
            cnt = count_tiles(
                lambda k_, j, c: (k_ == tau) & (lane < cand - (j * tk + c * LANES)))
            return jnp.where(cnt <= room, cand, jmax)

        jmax = lax.fori_loop(0, 15, idx_step, jnp.zeros((tq, LANES), I32))

        def demote(j, carry):
            r0 = pl.multiple_of(j * tk, tk)
            k_ = keys_ref[:, pl.ds(r0, tk)]
            cols = []
            for c in range(reps):
                kc = k_[:, c * LANES:(c + 1) * LANES]
                drop = (kc == tau) & (lane >= jmax - (j * tk + c * LANES)) & (tau > INT_MIN)
                cols.append(jnp.where(drop, kc - 1, kc))
            keys_ref[:, pl.ds(r0, tk)] = jnp.concatenate(cols, axis=1)
            return carry

        lax.fori_loop(0, n_tiles, demote, 0)

    m_ref[...] = jnp.full(m_ref.shape, NEG_BIG, F32)
    acc_ref[...] = jnp.zeros(acc_ref.shape, F32)
    posq = posq_ref[...]

    def attn_tile(j, near):
        r0 = pl.multiple_of(j * tk, tk)
        ka_lo = sm_ref[pl.ds(r0, tk), 0:128]
        ka_hi = sm_ref[pl.ds(r0, tk), 128:256]
        va = sm_ref[pl.ds(r0, tk), 256:384]
        sel = keys_ref[:, pl.ds(r0, tk)] >= _rep(tau_ref[...], reps)
        if near:
            posk = posk_ref[0, :, pl.ds(r0, tk)]
            bias = _t5_bias_tile(posq, posk, tab_ref, A_HEADS)
        for p in range(A_HEADS // 2):
            qp = qa_ref[:, p * LANES:(p + 1) * LANES]
            for h, kk in ((2 * p, ka_lo), (2 * p + 1, ka_hi)):
                s = _dot_nt(qp, kk)
                s = s + (bias[h] if near else far_ref[h])
                s = jnp.where(sel, s, NEG_BIG)
                m_prev = m_ref[h]
                m_next = jnp.maximum(m_prev, jnp.max(s, axis=1, keepdims=True))
                alpha = jnp.exp(m_prev - m_next)
                pr = jnp.exp(s - _rep(m_next, reps))
                acc_ref[h] = alpha * acc_ref[h] + _dot(pr.astype(BF16), va)
                m_ref[h] = m_next

    def attn_body(j, carry):
        is_far = pmax_ref[b * nq + j] - pmin_ref[b * nq + i] <= -MAX_DISTANCE
        lax.cond(is_far, lambda: attn_tile(j, False), lambda: attn_tile(j, True))
        return carry

    lax.fori_loop(0, n_tiles, attn_body, 0)

    for h in range(A_HEADS):
        a = acc_ref[h]
        o_ref[:, h * LANES:(h + 1) * LANES] = (a / a[:, V_DIM:V_DIM + 1]).astype(BF16)


def _dsa(qa, qi, wi, sm, posq, posk, tab, far, pmin, pmax, batch, seq, tq, topk):
    n = qa.shape[0]
    nq = seq // tq
    row = lambda w: pl.BlockSpec((tq, w), lambda b, i, *_: (b * nq + i, 0))
    grid_spec = pltpu.PrefetchScalarGridSpec(
        num_scalar_prefetch=2,
        grid=(batch, nq),
        in_specs=[row(512), row(512), row(LANES),
                  pl.BlockSpec((seq, 640), lambda b, i, *_: (b, 0)),
                  row(1),
                  pl.BlockSpec((1, 1, seq), lambda b, i, *_: (b, 0, 0)),
                  pl.BlockSpec((8, LANES), lambda b, i, *_: (0, 0)),
                  pl.BlockSpec(memory_space=pltpu.SMEM)],
        out_specs=row(A_HEADS * LANES),
        scratch_shapes=[pltpu.VMEM((tq, seq), I32),
                        pltpu.VMEM((IDX_HEADS, tq, LANES), F32),
                        pltpu.VMEM((tq, LANES), I32),
                        pltpu.VMEM((A_HEADS, tq, LANES), F32),
                        pltpu.VMEM((A_HEADS, tq, LANES), F32)],
    )
    return pl.pallas_call(
        functools.partial(_dsa_kernel, tq=tq, topk=topk, nq=nq),
        grid_spec=grid_spec,
        out_shape=jax.ShapeDtypeStruct((n, A_HEADS * LANES), BF16),
        compiler_params=pltpu.CompilerParams(vmem_limit_bytes=VMEM_LIMIT),
        name="dsa",
    )(pmin, pmax, qa, qi, wi, sm, posq, posk, tab, far)


def _mla_kernel(q_ref, k_ref, v_ref, o_ref, m_ref, acc_ref, *, tq):
    i = pl.program_id(2)
    tk = tq
    reps = tk // LANES
    q = q_ref[...]
    m_ref[...] = jnp.full(m_ref.shape, NEG_BIG, F32)
    acc_ref[...] = jnp.zeros(acc_ref.shape, F32)

    def tile(j, diag):
        r0 = pl.multiple_of(j * tk, tk)
        s = _dot_nt(q, k_ref[pl.ds(r0, tk), :])
        if diag:
            row_l = lax.broadcasted_iota(I32, (tq, tk), 0)
            col_l = lax.broadcasted_iota(I32, (tq, tk), 1)
            s = jnp.where(col_l < (row_l // CHUNK + 1) * CHUNK, s, NEG_BIG)
        m_prev = m_ref[...]
        m_next = jnp.maximum(m_prev, jnp.max(s, axis=1, keepdims=True))
        alpha = jnp.exp(m_prev - m_next)
        pr = jnp.exp(s - _rep(m_next, reps))
        acc_ref[...] = alpha * acc_ref[...] + _dot(pr.astype(BF16), v_ref[pl.ds(r0, tk), :])
        m_ref[...] = m_next

    lax.fori_loop(0, i, lambda j, c: (tile(j, False), c)[1], 0)
    tile(i, True)
    a = acc_ref[...]
    o_ref[...] = (a / a[:, V_DIM:V_DIM + 1]).astype(BF16)


def _mla(qb, kb, vb, batch, seq, tq):
    n = qb.shape[0]
    nq = seq // tq
    return pl.pallas_call(
        functools.partial(_mla_kernel, tq=tq),
        grid=(batch, B_HEADS, nq),
        in_specs=[pl.BlockSpec((tq, LANES), lambda b, h, i: (b * nq + i, h)),
                  pl.BlockSpec((seq, LANES), lambda b, h, i: (b, h)),
                  pl.BlockSpec((seq, LANES), lambda b, h, i: (b, h))],
        out_specs=pl.BlockSpec((tq, LANES), lambda b, h, i: (b * nq + i, h)),
        out_shape=jax.ShapeDtypeStruct((n, B_HEADS * LANES), BF16),
        scratch_shapes=[pltpu.VMEM((tq, LANES), F32), pltpu.VMEM((tq, LANES), F32)],
        compiler_params=pltpu.CompilerParams(vmem_limit_bytes=VMEM_LIMIT),
        name="mla",
    )(qb, kb, vb)


def _tail_kernel(x_ref, oa_ref, ob_ref, gt_ref, mod_ref, woa_ref, wob_ref, wout_ref,
                 g2_ref, win_ref, wdn_ref, gf_ref, o_ref, *, d, d_ff, chunk):
    x = x_ref[...]
    gt2 = mod_ref[0, 5:6, :]
    sh3, sc3, gt3 = mod_ref[0, 6:7, :], mod_ref[0, 7:8, :], mod_ref[0, 8:9, :]
    ya = _dot(oa_ref[...], woa_ref[...])
    yb = _dot(ob_ref[...], wob_ref[...])
    y = gt_ref[:, 0:d].astype(F32) * ya + gt_ref[:, d:2 * d].astype(F32) * yb
    x = x + gt2 * _dot(y.astype(BF16), wout_ref[...])
    h = _rms(x, g2_ref[...]) * (1.0 + sc3) + sh3
    f = _swiglu_tile(h.astype(BF16), win_ref, wdn_ref, d_ff, chunk)
    x = x + (0.5 * gt3) * f
    o_ref[...] = _rms(x, gf_ref[...])


def _tail(x1, oa, ob, gates, mod3, woa, wob, wout, g2, w_in, w_down, gf, seq, tm):
    n, d = x1.shape
    d_ff = w_down.shape[0]
    tiles_per_batch = seq // tm
    row = lambda w: pl.BlockSpec((tm, w), lambda i: (i, 0))
    kern = functools.partial(_tail_kernel, d=d, d_ff=d_ff, chunk=_ffn_chunk(d_ff))
    return pl.pallas_call(
        kern,
        grid=(n // tm,),
        in_specs=[row(d), row(oa.shape[1]), row(ob.shape[1]), row(2 * d),
                  pl.BlockSpec((1, N_ADA, d), lambda i: (i // tiles_per_batch, 0, 0)),
                  _const_spec(woa.shape), _const_spec(wob.shape), _const_spec(wout.shape),
                  _const_spec((1, d)), _const_spec(w_in.shape), _const_spec(w_down.shape),
                  _const_spec((1, d))],
        out_specs=row(d),
        out_shape=jax.ShapeDtypeStruct((n, d), F32),
        compiler_params=pltpu.CompilerParams(vmem_limit_bytes=VMEM_LIMIT),
        name="tail",
    )(x1, oa, ob, gates, mod3, woa, wob, wout, g2, w_in, w_down, gf)


def _pad_heads(w, heads, width):
    k = w.shape[0]
    w3 = w.reshape(k, heads, width)
    return jnp.pad(w3, ((0, 0), (0, 0), (0, LANES - width))).reshape(k, heads * LANES)


def _pad_head_rows(w, heads, width):
    dd = w.shape[1]
    w3 = w.reshape(heads, width, dd)
    return jnp.pad(w3, ((0, 0), (0, LANES - width), (0, 0))).reshape(heads * LANES, dd)


def _layout_w_in(w_in, d):
    z = lambda c: jnp.zeros((d, c), w_in.dtype)
    o = 0
    qa = w_in[:, o:o + 512] * (A_HEAD_DIM ** -0.5); o += 512
    ka = w_in[:, o:o + 64]; o += 64
    va = w_in[:, o:o + 64]; o += 64
    qi = w_in[:, o:o + 512]; o += 512
    ki = w_in[:, o:o + 64]; o += 64
    wi = w_in[:, o:o + IDX_HEADS]; o += IDX_HEADS
    cq = w_in[:, o:o + Q_LORA]; o += Q_LORA
    ckv = w_in[:, o:o + KV_LORA]; o += KV_LORA
    kr = w_in[:, o:o + QK_ROPE]; o += QK_ROPE
    gates = w_in[:, o:o + 2 * d]
    hr = QK_ROPE // 2
    kt = jnp.concatenate([-kr[:, hr:], kr[:, :hr]], axis=1)
    cols = [qa, qi,
            ka, z(64), z(64), ka, va, z(64), ki, z(64), z(64), ki,
            wi, z(LANES - IDX_HEADS),
            z(64), kr, z(32), z(64), kt, z(32),
            cq, ckv, gates]
    return jnp.concatenate(cols, axis=1).astype(BF16)


def _layout_w_uq(w_uq):
    scale = (QK_NOPE + QK_ROPE) ** -0.5
    k = w_uq.shape[0]
    w3 = w_uq.reshape(k, B_HEADS, QK_NOPE + QK_ROPE) * scale
    nope, pe = w3[:, :, :QK_NOPE], w3[:, :, QK_NOPE:]
    hr = QK_ROPE // 2
    rot = jnp.concatenate([-pe[:, :, hr:], pe[:, :, :hr]], axis=2)
    zpad = jnp.zeros((k, B_HEADS, LANES - QK_NOPE - QK_ROPE), w_uq.dtype)
    plain = jnp.concatenate([nope, pe, zpad], axis=2).reshape(k, B_HEADS * LANES)
    rotw = jnp.concatenate([jnp.zeros_like(nope), rot, zpad], axis=2).reshape(k, B_HEADS * LANES)
    return plain.astype(BF16), rotw.astype(BF16)


def _rope_tables(positions):
    half = QK_ROPE // 2
    freqs = ROPE_THETA ** (-2.0 * jnp.arange(half, dtype=F32) / QK_ROPE)
    ang = positions.astype(F32).reshape(-1)[:, None] * freqs
    cos, sin = jnp.cos(ang), jnp.sin(ang)
    n = ang.shape[0]
    ones = jnp.ones((n, QK_NOPE), F32)
    z32 = jnp.zeros((n, LANES - QK_NOPE - QK_ROPE), F32)
    cs = jnp.concatenate([ones, cos, cos, z32], axis=1)
    sn = jnp.concatenate([jnp.zeros((n, QK_NOPE), F32), sin, sin, z32], axis=1)
    return cs, sn


def _tiles(seq):
    tm = 512 if seq % 512 == 0 else 256
    tq_dsa = 256
    tq_mla = 512 if seq % 512 == 0 else 256
    return tm, tq_dsa, tq_mla


def kernel(x, c, positions, w_ada, b_ada, g_ffn1, w_ffn1_in, w_ffn1_down, g_mix, w_in,
           g_cq, w_uq, g_ckv, w_uk, w_uv, rel_bias, w_o_a, w_o_b, w_out,
           g_ffn2, w_ffn2_in, w_ffn2_down, g_final):
    batch, seq, d = x.shape
    depth = w_ada.shape[0]
    n = batch * seq
    topk = min(TOPK_MAX, seq // 4)
    tm, tq_dsa, tq_mla = _tiles(seq)
    assert seq % tq_dsa == 0 and seq % tm == 0 and d % LANES == 0

    xf = x.reshape(n, d)
    c_pad = jnp.pad(c, ((0, 8 - batch % 8 if batch % 8 else 0), (0, 0)))
    cs, sn = _rope_tables(positions)
    posq = positions.reshape(n, 1).astype(I32)
    posk = positions.reshape(batch, 1, seq).astype(I32)
    ptile = positions.reshape(batch * (seq // tq_dsa), tq_dsa).astype(I32)
    pmin, pmax = ptile.min(axis=1), ptile.max(axis=1)
    tab = jnp.pad(rel_bias.T.astype(F32), ((0, 0), (0, LANES - NUM_BUCKETS)))
    far = rel_bias[NUM_BUCKETS // 2 - 1].astype(F32)

    assert depth == 1, "the tail kernel fuses the final norm into the single layer"
    l = 0
    mod = _ada(c_pad, w_ada[l], b_ada[l])[:batch]
    mod3 = mod.reshape(batch, N_ADA, d)
    x1 = _ffn1(xf, mod3, g_ffn1[l].reshape(1, d), w_ffn1_in[l].astype(BF16),
               w_ffn1_down[l].astype(BF16), seq, tm)
    wq, wqr = _layout_w_uq(w_uq[l])
    qa, qi, sm, wi, qb, kb, vb, gates = _proj(
        x1, mod3, g_mix[l].reshape(1, d), _layout_w_in(w_in[l], d),
        g_cq[l].reshape(1, Q_LORA), wq, wqr, g_ckv[l].reshape(1, KV_LORA),
        _pad_heads(w_uk[l], B_HEADS, QK_NOPE).astype(BF16),
        _pad_heads(w_uv[l], B_HEADS, V_DIM).astype(BF16),
        cs, sn, seq, tm)
    oa = _dsa(qa, qi, wi, sm, posq, posk, tab, far, pmin, pmax, batch, seq, tq_dsa, topk)
    ob = _mla(qb, kb, vb, batch, seq, tq_mla)
    out = _tail(x1, oa, ob, gates, mod3,
                _pad_head_rows(w_o_a[l], A_HEADS, A_HEAD_DIM).astype(BF16),
                _pad_head_rows(w_o_b[l], B_HEADS, V_DIM).astype(BF16),
                w_out[l].astype(BF16), g_ffn2[l].reshape(1, d),
                w_ffn2_in[l].astype(BF16), w_ffn2_down[l].astype(BF16),
                g_final.reshape(1, d), seq, tm)
    return out.reshape(batch, seq, d)
```

```python
import functools
import math

import jax
import jax.numpy as jnp
from jax import lax
from jax.experimental import pallas as pl
from jax.experimental.pallas import tpu as pltpu

F32 = jnp.float32
BF16 = jnp.bfloat16
I32 = jnp.int32

CHUNK = 64
EPS = 1e-6
A_HEADS = 8
A_HEAD_DIM = 64
IDX_HEADS = 8
IDX_DIM = 64
TOPK_MAX = 256
B_HEADS = 8
QK_NOPE = 64
QK_ROPE = 32
V_DIM = 64
Q_LORA = 384
KV_LORA = 256
ROPE_THETA = 10000.0
NUM_BUCKETS = 32
MAX_DISTANCE = 128
N_ADA = 9

LANES = 128
SUBLANES = 8
VMEM_LIMIT = 56 * 1024 * 1024
INT_MIN = -2 ** 31
NEG_BIG = -1e30
LOG2E = math.log2(math.e)

C_QA = 0
C_QI = 512
C_SM = 1024
C_WI = 1664
C_CQ = 2048
C_CKV = C_CQ + Q_LORA
C_GT = C_CKV + KV_LORA


def _dot(a, b):
    return jnp.dot(a, b, preferred_element_type=F32)


def _dot_nt(a, b):
    return lax.dot_general(a, b, (((1,), (1,)), ((), ())), preferred_element_type=F32)


def _dot_tn(a, b):
    return lax.dot_general(a, b, (((0,), (0,)), ((), ())), preferred_element_type=F32)


def _rms(x, g):
    return x * lax.rsqrt(jnp.mean(x * x, axis=-1, keepdims=True) + EPS) * g


def _const_spec(shape):
    nd = len(shape)
    return pl.BlockSpec(shape, lambda *_: (0,) * nd, pipeline_mode=pl.Buffered(1))


def _ada_kernel(c_ref, w_ref, b_ref, o_ref):
    c = c_ref[...]
    s = c * jax.nn.sigmoid(c)
    o_ref[...] = _dot(s.astype(BF16), w_ref[...].astype(BF16)) + b_ref[...]


def _ada(c_pad, w_ada, b_ada):
    d = c_pad.shape[1]
    n_out = w_ada.shape[1]
    return pl.pallas_call(
        _ada_kernel,
        grid=(n_out // d,),
        in_specs=[pl.BlockSpec(c_pad.shape, lambda j: (0, 0)),
                  pl.BlockSpec((d, d), lambda j: (0, j)),
                  pl.BlockSpec((1, d), lambda j: (0, j))],
        out_specs=pl.BlockSpec(c_pad.shape, lambda j: (0, j)),
        out_shape=jax.ShapeDtypeStruct((c_pad.shape[0], n_out), F32),
        compiler_params=pltpu.CompilerParams(vmem_limit_bytes=VMEM_LIMIT),
        name="ada",
    )(c_pad, w_ada, b_ada.reshape(1, n_out))


def _swiglu_tile(hb, win_ref, wdn_ref, d_ff, chunk):
    acc = None
    for j in range(d_ff // chunk):
        g = _dot(hb, win_ref[:, j * chunk:(j + 1) * chunk])
        u = _dot(hb, win_ref[:, d_ff + j * chunk:d_ff + (j + 1) * chunk])
        a = (g * jax.nn.sigmoid(g) * u).astype(BF16)
        part = _dot(a, wdn_ref[j * chunk:(j + 1) * chunk, :])
        acc = part if acc is None else acc + part
    return acc


def _ffn_chunk(d_ff):
    for c in (512, 256, 128):
        if d_ff % c == 0:
            return c
    raise ValueError("d_ff must be a multiple of 128")


def _ffn1_kernel(x_ref, mod_ref, g_ref, win_ref, wdn_ref, o_ref, *, d_ff, chunk):
    x = x_ref[...]
    sh, sc, gt = mod_ref[0, 0:1, :], mod_ref[0, 1:2, :], mod_ref[0, 2:3, :]
    h = _rms(x, g_ref[...]) * (1.0 + sc) + sh
    y = _swiglu_tile(h.astype(BF16), win_ref, wdn_ref, d_ff, chunk)
    o_ref[...] = x + (0.5 * gt) * y


def _ffn1(x2, mod3, g, w_in, w_down, seq, tm):
    n, d = x2.shape
    d_ff = w_down.shape[0]
    tiles_per_batch = seq // tm
    kern = functools.partial(_ffn1_kernel, d_ff=d_ff, chunk=_ffn_chunk(d_ff))
    return pl.pallas_call(
        kern,
        grid=(n // tm,),
        in_specs=[pl.BlockSpec((tm, d), lambda i: (i, 0)),
                  pl.BlockSpec((1, N_ADA, d), lambda i: (i // tiles_per_batch, 0, 0)),
                  _const_spec((1, d)),
                  _const_spec(w_in.shape),
                  _const_spec(w_down.shape)],
        out_specs=pl.BlockSpec((tm, d), lambda i: (i, 0)),
        out_shape=jax.ShapeDtypeStruct((n, d), F32),
        compiler_params=pltpu.CompilerParams(vmem_limit_bytes=VMEM_LIMIT),
        name="ffn1",
    )(x2, mod3, g, w_in, w_down)


def _proj_kernel(x_ref, mod_ref, g_ref, w_ref, gcq_ref, wq_ref, wqr_ref, gckv_ref,
                 wk_ref, wv_ref, cs_ref, sn_ref,
                 qa_ref, qi_ref, sm_ref, wi_ref, qb_ref, kb_ref, vb_ref, gt_ref, *, d):
    x = x_ref[...]
    sh, sc = mod_ref[0, 3:4, :], mod_ref[0, 4:5, :]
    hb = (_rms(x, g_ref[...]) * (1.0 + sc) + sh).astype(BF16)
    cs = cs_ref[...]
    sn = sn_ref[...]
    lane = lax.broadcasted_iota(I32, (1, LANES), 1)
    one_at_v = (lane == V_DIM).astype(F32)

    qa_ref[...] = _dot(hb, w_ref[:, C_QA:C_QA + 512]).astype(BF16)
    qi_ref[...] = _dot(hb, w_ref[:, C_QI:C_QI + 512]).astype(BF16)

    sm = _dot(hb, w_ref[:, C_SM:C_SM + 640])
    sm_ref[:, 0:256] = sm[:, 0:256].astype(BF16)
    sm_ref[:, 256:384] = (sm[:, 256:384] + one_at_v).astype(BF16)
    sm_ref[:, 384:640] = sm[:, 384:640].astype(BF16)

    misc = _dot(hb, w_ref[:, C_WI:C_WI + 384])
    wi_ref[...] = misc[:, 0:128]
    kpe = misc[:, 128:256] * cs + misc[:, 256:384] * sn

    lat = _dot(hb, w_ref[:, C_CQ:C_GT])
    cq = _rms(lat[:, 0:Q_LORA], gcq_ref[...]).astype(BF16)
    ckv = _rms(lat[:, Q_LORA:Q_LORA + KV_LORA], gckv_ref[...]).astype(BF16)
    qn = _dot(cq, wq_ref[...])
    qr = _dot(cq, wqr_ref[...])
    kn = _dot(ckv, wk_ref[...])
    vv = _dot(ckv, wv_ref[...])
    for h in range(B_HEADS):
        sl = slice(h * LANES, (h + 1) * LANES)
        qb_ref[:, sl] = (qn[:, sl] * cs + qr[:, sl] * sn).astype(BF16)
        kb_ref[:, sl] = (kn[:, sl] + kpe).astype(BF16)
        vb_ref[:, sl] = (vv[:, sl] + one_at_v).astype(BF16)

    gl = _dot(hb, w_ref[:, C_GT:C_GT + 2 * d])
    gt_ref[...] = jax.nn.sigmoid(gl).astype(BF16)


def _proj(x1, mod3, g, w_all, gcq, wq, wqr, gckv, wk, wv, cs, sn, seq, tm):
    n, d = x1.shape
    tiles_per_batch = seq // tm
    row = lambda w: pl.BlockSpec((tm, w), lambda i: (i, 0))
    out_widths = (512, 512, 640, 128, 1024, 1024, 1024, 2 * d)
    out_dtypes = (BF16, BF16, BF16, F32, BF16, BF16, BF16, BF16)
    return pl.pallas_call(
        functools.partial(_proj_kernel, d=d),
        grid=(n // tm,),
        in_specs=[row(d),
                  pl.BlockSpec((1, N_ADA, d), lambda i: (i // tiles_per_batch, 0, 0)),
                  _const_spec((1, d)), _const_spec(w_all.shape),
                  _const_spec(gcq.shape), _const_spec(wq.shape), _const_spec(wqr.shape),
                  _const_spec(gckv.shape), _const_spec(wk.shape), _const_spec(wv.shape),
                  row(LANES), row(LANES)],
        out_specs=[row(w) for w in out_widths],
        out_shape=[jax.ShapeDtypeStruct((n, w), dt) for w, dt in zip(out_widths, out_dtypes)],
        compiler_params=pltpu.CompilerParams(vmem_limit_bytes=VMEM_LIMIT),
        name="proj",
    )(x1, mod3, g, w_all, gcq, wq, wqr, gckv, wk, wv, cs, sn)


def _rowsum8(x):
    v = x.reshape(x.shape[0] // SUBLANES, SUBLANES, x.shape[1])
    while v.shape[0] > 1:
        half = v.shape[0] // 2
        v = v[:half] + v[half:]
    return v[0]


def _t5_bias_tiles(posk, posq, tab_ref, heads):
    half = NUM_BUCKETS // 2
    max_exact = half // 2
    rel = posk - posq
    n = jnp.abs(rel)
    nf = jnp.maximum(n, 1).astype(F32)
    large = max_exact + (jnp.log(nf / max_exact) / math.log(MAX_DISTANCE / max_exact)
                         * (half - max_exact)).astype(I32)
    large = jnp.minimum(large, half - 1)
    bucket = jnp.where(rel > 0, half, 0) + jnp.where(n < max_exact, n, large)
    tk, tq = bucket.shape
    out = []
    for h in range(heads):
        tab = jnp.broadcast_to(tab_ref[h:h + 1, :], (tk, LANES))
        cols = [jnp.take_along_axis(tab, bucket[:, c * LANES:(c + 1) * LANES], axis=1,
                                    mode="promise_in_bounds")
                for c in range(tq // LANES)]
        out.append(jnp.concatenate(cols, axis=1))
    return out


def _dsa_kernel(pmin_ref, pmax_ref,
                qa_ref, qi_ref, wi_ref, sm_ref, vat_ref, posq_ref, posk_ref, tab_ref, far_ref,
                o_ref,
                keys_ref, m_ref, acc_ref, *, tq, topk, nq):
    b = pl.program_id(0)
    i = pl.program_id(1)
    tk = tq
    n_tiles = i + 1

    w_t = wi_ref[...].T[0:IDX_HEADS, :]
    key_l = lax.broadcasted_iota(I32, (tk, tq), 0)
    q_l = lax.broadcasted_iota(I32, (tk, tq), 1)
    diag_ok = key_l < (q_l // CHUNK + 1) * CHUNK

    def score_tile(j, diag):
        r0 = pl.multiple_of(j * tk, tk)
        ki_lo = sm_ref[pl.ds(r0, tk), 384:512]
        ki_hi = sm_ref[pl.ds(r0, tk), 512:640]
        score = None
        for p in range(IDX_HEADS // 2):
            qp = qi_ref[:, p * LANES:(p + 1) * LANES]
            for h, kk in ((2 * p, ki_lo), (2 * p + 1, ki_hi)):
                t = w_t[h:h + 1, :] * jnp.maximum(_dot_nt(kk, qp), 0.0)
                score = t if score is None else score + t
        bits = lax.bitcast_convert_type(score + 0.0, I32)
        key = bits ^ ((bits >> 31) & 0x7FFFFFFF)
        if diag:
            key = jnp.where(diag_ok, key, INT_MIN)
        keys_ref[pl.ds(r0, tk), :] = key

    lax.fori_loop(0, i, lambda j, c: (score_tile(j, False), c)[1], 0)
    score_tile(i, True)

    def count_tiles(pred):
        def body(j, acc):
            r0 = pl.multiple_of(j * tk, tk)
            return acc + _rowsum8(jnp.where(pred(keys_ref[pl.ds(r0, tk), :], j), 1, 0))
        acc = lax.fori_loop(0, n_tiles, body, jnp.zeros((SUBLANES, tq), I32))
        return jnp.sum(acc, axis=0, keepdims=True)

    q_row = lax.broadcasted_iota(I32, (1, tq), 1)
    n_adm = i * tq + (q_row // CHUNK + 1) * CHUNK

    def bit_cond(st):
        return (st[0] < 32) & (st[3] > topk)

    def bit_step(st):
        p, prefix, cnt_pref, _ = st
        cand = prefix ^ jnp.left_shift(jnp.int32(1), 31 - p)
        cnt = count_tiles(lambda k_, j: k_ >= cand)
        take = cnt >= topk
        cnt_pref = jnp.where(take, cnt, cnt_pref)
        return p + 1, jnp.where(take, cand, prefix), cnt_pref, jnp.max(cnt_pref)

    _, tau, cnt_pref, worst = lax.while_loop(
        bit_cond, bit_step,
        (jnp.int32(0), jnp.full((1, tq), INT_MIN, I32), n_adm, jnp.max(n_adm)))
    tau_sel = jnp.maximum(tau, INT_MIN + 1)

    @pl.when(worst > topk)
    def _ties():
        cnt_gt = count_tiles(lambda k_, j: k_ > tau)
        room = topk - cnt_gt

        def idx_step(p, jmax):
            cand = jmax | jnp.left_shift(jnp.int32(1), 14 - p)
            cnt = count_tiles(lambda k_, j: (k_ == tau) & (key_l + j * tk < cand))
            return jnp.where(cnt <= room, cand, jmax)

        jmax = lax.fori_loop(0, 15, idx_step, jnp.zeros((1, tq), I32))
        over = cnt_pref > topk

        def demote(j, carry):
            r0 = pl.multiple_of(j * tk, tk)
            k_ = keys_ref[pl.ds(r0, tk), :]
            drop = (k_ == tau) & (key_l + j * tk >= jmax) & over
            keys_ref[pl.ds(r0, tk), :] = jnp.where(drop, k_ - 1, k_)
            return carry

        lax.fori_loop(0, n_tiles, demote, 0)

    m_ref[...] = jnp.full(m_ref.shape, NEG_BIG, F32)
    acc_ref[...] = jnp.zeros(acc_ref.shape, F32)
    posq = posq_ref[0]

    def attn_tile(j, near):
        r0 = pl.multiple_of(j * tk, tk)
        ka_lo = sm_ref[pl.ds(r0, tk), 0:128]
        ka_hi = sm_ref[pl.ds(r0, tk), 128:256]
        va_t = vat_ref[:, pl.ds(r0, tk)]
        sel = keys_ref[pl.ds(r0, tk), :] >= tau_sel
        if near:
            bias = _t5_bias_tiles(posk_ref[pl.ds(r0, tk), :], posq, tab_ref, A_HEADS)
        for p in range(A_HEADS // 2):
            qp = qa_ref[:, p * LANES:(p + 1) * LANES]
            for h, kk in ((2 * p, ka_lo), (2 * p + 1, ka_hi)):
                s = _dot_nt(kk, qp) + (bias[h] if near else far_ref[h])
                s = jnp.where(sel, s, NEG_BIG)
                m_prev = m_ref[h:h + 1, :]
                m_next = jnp.maximum(m_prev, jnp.max(s, axis=0, keepdims=True))
                alpha = jnp.exp2(m_prev - m_next)
                pr = jnp.exp2(s - m_next).astype(BF16)
                acc_ref[h] = alpha * acc_ref[h] + _dot(va_t, pr)
                m_ref[h:h + 1, :] = m_next

    def attn_body(j, carry):
        is_far = pmax_ref[b * nq + j] - pmin_ref[b * nq + i] <= -MAX_DISTANCE
        lax.cond(is_far, lambda: attn_tile(j, False), lambda: attn_tile(j, True))
        return carry

    lax.fori_loop(0, n_tiles, attn_body, 0)

    for h in range(A_HEADS):
        a = acc_ref[h]
        o_ref[:, h * LANES:(h + 1) * LANES] = (a / a[V_DIM:V_DIM + 1, :]).T.astype(BF16)


def _dsa(qa, qi, wi, sm, vat, posq, posk, tab, far, pmin, pmax, batch, seq, tq, topk):
    n = qa.shape[0]
    nq = seq // tq
    row = lambda w: pl.BlockSpec((tq, w), lambda b, i, *_: (b * nq + i, 0))
    grid_spec = pltpu.PrefetchScalarGridSpec(
        num_scalar_prefetch=2,
        grid=(batch, nq),
        in_specs=[row(512), row(512), row(LANES),
                  pl.BlockSpec((seq, 640), lambda b, i, *_: (b, 0),
                               pipeline_mode=pl.Buffered(1)),
                  pl.BlockSpec((LANES, seq), lambda b, i, *_: (0, b)),
                  pl.BlockSpec((1, 1, tq), lambda b, i, *_: (b * nq + i, 0, 0)),
                  pl.BlockSpec((seq, 1), lambda b, i, *_: (b, 0),
                               pipeline_mode=pl.Buffered(1)),
                  pl.BlockSpec((A_HEADS, LANES), lambda b, i, *_: (0, 0)),
                  pl.BlockSpec(memory_space=pltpu.SMEM)],
        out_specs=row(A_HEADS * LANES),
        scratch_shapes=[pltpu.VMEM((seq, tq), I32),
                        pltpu.VMEM((A_HEADS, tq), F32),
                        pltpu.VMEM((A_HEADS, LANES, tq), F32)],
    )
    return pl.pallas_call(
        functools.partial(_dsa_kernel, tq=tq, topk=topk, nq=nq),
        grid_spec=grid_spec,
        out_shape=jax.ShapeDtypeStruct((n, A_HEADS * LANES), BF16),
        compiler_params=pltpu.CompilerParams(vmem_limit_bytes=VMEM_LIMIT),
        name="dsa",
    )(pmin, pmax, qa, qi, wi, sm, vat, posq, posk, tab, far)


MLA_HEAD_GROUP = 4


def _mla_kernel(q_ref, k_ref, vt_ref, o_ref, acc_ref, *, tq):
    i = pl.program_id(2)
    tk = tq
    hg = MLA_HEAD_GROUP
    acc_ref[...] = jnp.zeros(acc_ref.shape, F32)

    def tile(j, ms, diag):
        r0 = pl.multiple_of(j * tk, tk)
        if diag:
            key_l = lax.broadcasted_iota(I32, (tk, tq), 0)
            q_l = lax.broadcasted_iota(I32, (tk, tq), 1)
            ok = key_l < (q_l // CHUNK + 1) * CHUNK
        out = []
        for h in range(hg):
            sl = slice(h * LANES, (h + 1) * LANES)
            s = _dot_nt(k_ref[pl.ds(r0, tk), sl], q_ref[:, sl])
            if diag:
                s = jnp.where(ok, s, NEG_BIG)
            m_next = jnp.maximum(ms[h], jnp.max(s, axis=0, keepdims=True))
            alpha = jnp.exp2(ms[h] - m_next)
            pr = jnp.exp2(s - m_next).astype(BF16)
            acc_ref[h] = alpha * acc_ref[h] + _dot(vt_ref[sl, pl.ds(r0, tk)], pr)
            out.append(m_next)
        return tuple(out)

    m0 = tuple(jnp.full((1, tq), NEG_BIG, F32) for _ in range(hg))
    ms = lax.fori_loop(0, i, lambda j, m_: tile(j, m_, False), m0)
    tile(i, ms, True)
    for h in range(hg):
        a = acc_ref[h]
        o_ref[:, h * LANES:(h + 1) * LANES] = (a / a[V_DIM:V_DIM + 1, :]).T.astype(BF16)


def _mla(qb, kb, vbt, batch, seq, tq):
    n = qb.shape[0]
    nq = seq // tq
    hg = MLA_HEAD_GROUP
    ng = B_HEADS // hg
    w = hg * LANES
    return pl.pallas_call(
        functools.partial(_mla_kernel, tq=tq),
        grid=(batch, ng, nq),
        in_specs=[pl.BlockSpec((tq, w), lambda b, g, i: (b * nq + i, g)),
                  pl.BlockSpec((seq, w), lambda b, g, i: (b, g)),
                  pl.BlockSpec((w, seq), lambda b, g, i: (b * ng + g, 0))],
        out_specs=pl.BlockSpec((tq, w), lambda b, g, i: (b * nq + i, g)),
        out_shape=jax.ShapeDtypeStruct((n, B_HEADS * LANES), BF16),
        scratch_shapes=[pltpu.VMEM((hg, LANES, tq), F32)],
        compiler_params=pltpu.CompilerParams(vmem_limit_bytes=VMEM_LIMIT),
        name="mla",
    )(qb, kb, vbt)


def _tail_kernel(x_ref, oa_ref, ob_ref, gt_ref, mod_ref, woa_ref, wob_ref, wout_ref,
                 g2_ref, win_ref, wdn_ref, gf_ref, o_ref, *, d, d_ff, chunk):
    x = x_ref[...]
    gt2 = mod_ref[0, 5:6, :]
    sh3, sc3, gt3 = mod_ref[0, 6:7, :], mod_ref[0, 7:8, :], mod_ref[0, 8:9, :]
    ya = _dot(oa_ref[...], woa_ref[...])
    yb = _dot(ob_ref[...], wob_ref[...])
    y = gt_ref[:, 0:d].astype(F32) * ya + gt_ref[:, d:2 * d].astype(F32) * yb
    x = x + gt2 * _dot(y.astype(BF16), wout_ref[...])
    h = _rms(x, g2_ref[...]) * (1.0 + sc3) + sh3
    f = _swiglu_tile(h.astype(BF16), win_ref, wdn_ref, d_ff, chunk)
    x = x + (0.5 * gt3) * f
    o_ref[...] = _rms(x, gf_ref[...])


def _tail(x1, oa, ob, gates, mod3, woa, wob, wout, g2, w_in, w_down, gf, seq, tm):
    n, d = x1.shape
    d_ff = w_down.shape[0]
    tiles_per_batch = seq // tm
    row = lambda w: pl.BlockSpec((tm, w), lambda i: (i, 0))
    kern = functools.partial(_tail_kernel, d=d, d_ff=d_ff, chunk=_ffn_chunk(d_ff))
    return pl.pallas_call(
        kern,
        grid=(n // tm,),
        in_specs=[row(d), row(oa.shape[1]), row(ob.shape[1]), row(2 * d),
                  pl.BlockSpec((1, N_ADA, d), lambda i: (i // tiles_per_batch, 0, 0)),
                  _const_spec(woa.shape), _const_spec(wob.shape), _const_spec(wout.shape),
                  _const_spec((1, d)), _const_spec(w_in.shape), _const_spec(w_down.shape),
                  _const_spec((1, d))],
        out_specs=row(d),
        out_shape=jax.ShapeDtypeStruct((n, d), F32),
        compiler_params=pltpu.CompilerParams(vmem_limit_bytes=VMEM_LIMIT),
        name="tail",
    )(x1, oa, ob, gates, mod3, woa, wob, wout, g2, w_in, w_down, gf)


def _pad_heads(w, heads, width):
    k = w.shape[0]
    w3 = w.reshape(k, heads, width)
    return jnp.pad(w3, ((0, 0), (0, 0), (0, LANES - width))).reshape(k, heads * LANES)


def _pad_head_rows(w, heads, width):
    dd = w.shape[1]
    w3 = w.reshape(heads, width, dd)
    return jnp.pad(w3, ((0, 0), (0, LANES - width), (0, 0))).reshape(heads * LANES, dd)


def _layout_w_in(w_in, d):
    z = lambda c: jnp.zeros((d, c), w_in.dtype)
    o = 0
    qa = w_in[:, o:o + 512] * (A_HEAD_DIM ** -0.5 * LOG2E); o += 512
    ka = w_in[:, o:o + 64]; o += 64
    va = w_in[:, o:o + 64]; o += 64
    qi = w_in[:, o:o + 512]; o += 512
    ki = w_in[:, o:o + 64]; o += 64
    wi = w_in[:, o:o + IDX_HEADS]; o += IDX_HEADS
    cq = w_in[:, o:o + Q_LORA]; o += Q_LORA
    ckv = w_in[:, o:o + KV_LORA]; o += KV_LORA
    kr = w_in[:, o:o + QK_ROPE]; o += QK_ROPE
    gates = w_in[:, o:o + 2 * d]
    hr = QK_ROPE // 2
    kt = jnp.concatenate([-kr[:, hr:], kr[:, :hr]], axis=1)
    cols = [qa, qi,
            ka, z(64), z(64), ka, va, z(64), ki, z(64), z(64), ki,
            wi, z(LANES - IDX_HEADS),
            z(64), kr, z(32), z(64), kt, z(32),
            cq, ckv, gates]
    return jnp.concatenate(cols, axis=1).astype(BF16)


def _layout_w_uq(w_uq):
    scale = (QK_NOPE + QK_ROPE) ** -0.5 * LOG2E
    k = w_uq.shape[0]
    w3 = w_uq.reshape(k, B_HEADS, QK_NOPE + QK_ROPE) * scale
    nope, pe = w3[:, :, :QK_NOPE], w3[:, :, QK_NOPE:]
    hr = QK_ROPE // 2
    rot = jnp.concatenate([-pe[:, :, hr:], pe[:, :, :hr]], axis=2)
    zpad = jnp.zeros((k, B_HEADS, LANES - QK_NOPE - QK_ROPE), w_uq.dtype)
    plain = jnp.concatenate([nope, pe, zpad], axis=2).reshape(k, B_HEADS * LANES)
    rotw = jnp.concatenate([jnp.zeros_like(nope), rot, zpad], axis=2).reshape(k, B_HEADS * LANES)
    return plain.astype(BF16), rotw.astype(BF16)


def _rope_tables(positions):
    half = QK_ROPE // 2
    freqs = ROPE_THETA ** (-2.0 * jnp.arange(half, dtype=F32) / QK_ROPE)
    ang = positions.astype(F32).reshape(-1)[:, None] * freqs
    cos, sin = jnp.cos(ang), jnp.sin(ang)
    n = ang.shape[0]
    ones = jnp.ones((n, QK_NOPE), F32)
    z32 = jnp.zeros((n, LANES - QK_NOPE - QK_ROPE), F32)
    cs = jnp.concatenate([ones, cos, cos, z32], axis=1)
    sn = jnp.concatenate([jnp.zeros((n, QK_NOPE), F32), sin, sin, z32], axis=1)
    return cs, sn


def _tiles(seq):
    tm = 512 if seq % 512 == 0 else 256
    tq_dsa = 512 if seq % 512 == 0 else 256
    tq_mla = 512 if seq % 512 == 0 else 256
    return tm, tq_dsa, tq_mla


def kernel(x, c, positions, w_ada, b_ada, g_ffn1, w_ffn1_in, w_ffn1_down, g_mix, w_in,
           g_cq, w_uq, g_ckv, w_uk, w_uv, rel_bias, w_o_a, w_o_b, w_out,
           g_ffn2, w_ffn2_in, w_ffn2_down, g_final):
    batch, seq, d = x.shape
    depth = w_ada.shape[0]
    n = batch * seq
    topk = min(TOPK_MAX, seq // 4)
    tm, tq_dsa, tq_mla = _tiles(seq)
    assert seq % tq_dsa == 0 and seq % tm == 0 and d % LANES == 0
    assert depth == 1, "the tail kernel fuses the final norm into the single layer"

    xf = x.reshape(n, d)
    c_pad = jnp.pad(c, ((0, 8 - batch % 8 if batch % 8 else 0), (0, 0)))
    cs, sn = _rope_tables(positions)
    pos = positions.astype(I32)
    posq = pos.reshape(n // tq_dsa, 1, tq_dsa)
    posk = pos.reshape(n, 1)
    ptile = pos.reshape(n // tq_dsa, tq_dsa)
    pmin, pmax = ptile.min(axis=1), ptile.max(axis=1)
    tab = jnp.pad(rel_bias.T.astype(F32) * LOG2E, ((0, 0), (0, LANES - NUM_BUCKETS)))
    far = rel_bias[NUM_BUCKETS // 2 - 1].astype(F32) * LOG2E

    l = 0
    mod = _ada(c_pad, w_ada[l], b_ada[l])[:batch]
    mod3 = mod.reshape(batch, N_ADA, d)
    x1 = _ffn1(xf, mod3, g_ffn1[l].reshape(1, d), w_ffn1_in[l].astype(BF16),
               w_ffn1_down[l].astype(BF16), seq, tm)
    wq, wqr = _layout_w_uq(w_uq[l])
    qa, qi, sm, wi, qb, kb, vb, gates = _proj(
        x1, mod3, g_mix[l].reshape(1, d), _layout_w_in(w_in[l], d),
        g_cq[l].reshape(1, Q_LORA), wq, wqr, g_ckv[l].reshape(1, KV_LORA),
        _pad_heads(w_uk[l], B_HEADS, QK_NOPE).astype(BF16),
        _pad_heads(w_uv[l], B_HEADS, V_DIM).astype(BF16),
        cs, sn, seq, tm)
    vat = sm[:, 256:384].T
    vbt = vb.reshape(batch, seq, B_HEADS * LANES).transpose(0, 2, 1).reshape(-1, seq)
    oa = _dsa(qa, qi, wi, sm, vat, posq, posk, tab, far, pmin, pmax, batch, seq, tq_dsa, topk)
    ob = _mla(qb, kb, vbt, batch, seq, tq_mla)
    out = _tail(x1, oa, ob, gates, mod3,
                _pad_head_rows(w_o_a[l], A_HEADS, A_HEAD_DIM).astype(BF16),
                _pad_head_rows(w_o_b[l], B_HEADS, V_DIM).astype(BF16),
                w_out[l].astype(BF16), g_ffn2[l].reshape(1, d),
                w_ffn2_in[l].astype(BF16), w_ffn2_down[l].astype(BF16),
                g_final.reshape(1, d), seq, tm)
    return out.reshape(batch, seq, d)
```

```python
import functools
import math

import jax
import jax.numpy as jnp
from jax import lax
from jax.experimental import pallas as pl
from jax.experimental.pallas import tpu as pltpu

F32 = jnp.float32
BF16 = jnp.bfloat16
I32 = jnp.int32

CHUNK = 64
EPS = 1e-6
A_HEADS = 8
A_HEAD_DIM = 64
IDX_HEADS = 8
IDX_DIM = 64
TOPK_MAX = 256
B_HEADS = 8
QK_NOPE = 64
QK_ROPE = 32
V_DIM = 64
Q_LORA = 384
KV_LORA = 256
ROPE_THETA = 10000.0
NUM_BUCKETS = 32
MAX_DISTANCE = 128
N_ADA = 9

LANES = 128
SUBLANES = 8
VMEM_LIMIT = 56 * 1024 * 1024
INT_MIN = -2 ** 31
NEG_BIG = -1e30
LOG2E = math.log2(math.e)

C_QA = 0
C_QI = 512
C_SM = 1024
C_WI = 1664
C_CQ = 2048
C_CKV = C_CQ + Q_LORA
C_GT = C_CKV + KV_LORA


def _dot(a, b):
    return jnp.dot(a, b, preferred_element_type=F32)


def _dot_nt(a, b):
    return lax.dot_general(a, b, (((1,), (1,)), ((), ())), preferred_element_type=F32)


def _dot_tn(a, b):
    return lax.dot_general(a, b, (((0,), (0,)), ((), ())), preferred_element_type=F32)


def _rms(x, g):
    return x * lax.rsqrt(jnp.mean(x * x, axis=-1, keepdims=True) + EPS) * g


def _const_spec(shape):
    nd = len(shape)
    return pl.BlockSpec(shape, lambda *_: (0,) * nd, pipeline_mode=pl.Buffered(1))


def _ada_kernel(c_ref, w_ref, b_ref, o_ref):
    c = c_ref[...]
    s = c * jax.nn.sigmoid(c)
    o_ref[...] = _dot(s.astype(BF16), w_ref[...].astype(BF16)) + b_ref[...]


def _ada(c_pad, w_ada, b_ada):
    d = c_pad.shape[1]
    n_out = w_ada.shape[1]
    return pl.pallas_call(
        _ada_kernel,
        grid=(n_out // d,),
        in_specs=[pl.BlockSpec(c_pad.shape, lambda j: (0, 0)),
                  pl.BlockSpec((d, d), lambda j: (0, j)),
                  pl.BlockSpec((1, d), lambda j: (0, j))],
        out_specs=pl.BlockSpec(c_pad.shape, lambda j: (0, j)),
        out_shape=jax.ShapeDtypeStruct((c_pad.shape[0], n_out), F32),
        compiler_params=pltpu.CompilerParams(vmem_limit_bytes=VMEM_LIMIT),
        name="ada",
    )(c_pad, w_ada, b_ada.reshape(1, n_out))


def _swiglu_tile(hb, win_ref, wdn_ref, d_ff, chunk):
    acc = None
    for j in range(d_ff // chunk):
        g = _dot(hb, win_ref[:, j * chunk:(j + 1) * chunk])
        u = _dot(hb, win_ref[:, d_ff + j * chunk:d_ff + (j + 1) * chunk])
        a = (g * jax.nn.sigmoid(g) * u).astype(BF16)
        part = _dot(a, wdn_ref[j * chunk:(j + 1) * chunk, :])
        acc = part if acc is None else acc + part
    return acc


def _ffn_chunk(d_ff):
    for c in (512, 256, 128):
        if d_ff % c == 0:
            return c
    raise ValueError("d_ff must be a multiple of 128")


def _ffn1_kernel(x_ref, mod_ref, g_ref, win_ref, wdn_ref, o_ref, *, d_ff, chunk):
    x = x_ref[...]
    sh, sc, gt = mod_ref[0, 0:1, :], mod_ref[0, 1:2, :], mod_ref[0, 2:3, :]
    h = _rms(x, g_ref[...]) * (1.0 + sc) + sh
    y = _swiglu_tile(h.astype(BF16), win_ref, wdn_ref, d_ff, chunk)
    o_ref[...] = x + (0.5 * gt) * y


def _ffn1(x2, mod3, g, w_in, w_down, seq, tm):
    n, d = x2.shape
    d_ff = w_down.shape[0]
    tiles_per_batch = seq // tm
    kern = functools.partial(_ffn1_kernel, d_ff=d_ff, chunk=_ffn_chunk(d_ff))
    return pl.pallas_call(
        kern,
        grid=(n // tm,),
        in_specs=[pl.BlockSpec((tm, d), lambda i: (i, 0)),
                  pl.BlockSpec((1, N_ADA, d), lambda i: (i // tiles_per_batch, 0, 0)),
                  _const_spec((1, d)),
                  _const_spec(w_in.shape),
                  _const_spec(w_down.shape)],
        out_specs=pl.BlockSpec((tm, d), lambda i: (i, 0)),
        out_shape=jax.ShapeDtypeStruct((n, d), F32),
        compiler_params=pltpu.CompilerParams(vmem_limit_bytes=VMEM_LIMIT),
        name="ffn1",
    )(x2, mod3, g, w_in, w_down)


def _proj_kernel(x_ref, mod_ref, g_ref, w_ref, gcq_ref, wq_ref, wqr_ref, gckv_ref,
                 wk_ref, wv_ref, cs_ref, sn_ref,
                 qa_ref, qi_ref, sm_ref, va_ref, wi_ref, qb_ref, kb_ref, vb_ref, gt_ref, *, d):
    x = x_ref[...]
    sh, sc = mod_ref[0, 3:4, :], mod_ref[0, 4:5, :]
    hb = (_rms(x, g_ref[...]) * (1.0 + sc) + sh).astype(BF16)
    cs = cs_ref[...]
    sn = sn_ref[...]
    lane = lax.broadcasted_iota(I32, (1, LANES), 1)
    one_at_v = (lane == V_DIM).astype(F32)

    qa_ref[...] = _dot(hb, w_ref[:, C_QA:C_QA + 512]).astype(BF16)
    qi_ref[...] = _dot(hb, w_ref[:, C_QI:C_QI + 512]).astype(BF16)

    sm = _dot(hb, w_ref[:, C_SM:C_SM + 640])
    sm_ref[:, 0:256] = sm[:, 0:256].astype(BF16)
    sm_ref[:, 256:512] = sm[:, 384:640].astype(BF16)
    va_ref[...] = (sm[:, 256:384] + one_at_v).astype(BF16)

    misc = _dot(hb, w_ref[:, C_WI:C_WI + 384])
    wi_ref[...] = misc[:, 0:128]
    kpe = misc[:, 128:256] * cs + misc[:, 256:384] * sn

    lat = _dot(hb, w_ref[:, C_CQ:C_GT])
    cq = _rms(lat[:, 0:Q_LORA], gcq_ref[...]).astype(BF16)
    ckv = _rms(lat[:, Q_LORA:Q_LORA + KV_LORA], gckv_ref[...]).astype(BF16)
    qn = _dot(cq, wq_ref[...])
    qr = _dot(cq, wqr_ref[...])
    kn = _dot(ckv, wk_ref[...])
    vv = _dot(ckv, wv_ref[...])
    for h in range(B_HEADS):
        sl = slice(h * LANES, (h + 1) * LANES)
        qb_ref[:, sl] = (qn[:, sl] * cs + qr[:, sl] * sn).astype(BF16)
        kb_ref[:, sl] = (kn[:, sl] + kpe).astype(BF16)
        vb_ref[:, sl] = (vv[:, sl] + one_at_v).astype(BF16)

    gl = _dot(hb, w_ref[:, C_GT:C_GT + 2 * d])
    gt_ref[...] = jax.nn.sigmoid(gl).astype(BF16)


def _proj(x1, mod3, g, w_all, gcq, wq, wqr, gckv, wk, wv, cs, sn, seq, tm):
    n, d = x1.shape
    tiles_per_batch = seq // tm
    row = lambda w: pl.BlockSpec((tm, w), lambda i: (i, 0))
    out_widths = (512, 512, 512, 128, 128, 1024, 1024, 1024, 2 * d)
    out_dtypes = (BF16, BF16, BF16, BF16, F32, BF16, BF16, BF16, BF16)
    return pl.pallas_call(
        functools.partial(_proj_kernel, d=d),
        grid=(n // tm,),
        in_specs=[row(d),
                  pl.BlockSpec((1, N_ADA, d), lambda i: (i // tiles_per_batch, 0, 0)),
                  _const_spec((1, d)), _const_spec(w_all.shape),
                  _const_spec(gcq.shape), _const_spec(wq.shape), _const_spec(wqr.shape),
                  _const_spec(gckv.shape), _const_spec(wk.shape), _const_spec(wv.shape),
                  row(LANES), row(LANES)],
        out_specs=[row(w) for w in out_widths],
        out_shape=[jax.ShapeDtypeStruct((n, w), dt) for w, dt in zip(out_widths, out_dtypes)],
        compiler_params=pltpu.CompilerParams(vmem_limit_bytes=VMEM_LIMIT),
        name="proj",
    )(x1, mod3, g, w_all, gcq, wq, wqr, gckv, wk, wv, cs, sn)


DSA_HEAD_GROUP = 4


def _rowsum8(x):
    v = x.reshape(x.shape[0] // SUBLANES, SUBLANES, x.shape[1])
    while v.shape[0] > 1:
        half = v.shape[0] // 2
        v = v[:half] + v[half:]
    return v[0]


def _t5_bias_tiles(posk, posq, tab_ref, heads):
    half = NUM_BUCKETS // 2
    max_exact = half // 2
    rel = posk - posq
    n = jnp.abs(rel)
    nf = jnp.maximum(n, 1).astype(F32)
    large = max_exact + (jnp.log(nf / max_exact) / math.log(MAX_DISTANCE / max_exact)
                         * (half - max_exact)).astype(I32)
    large = jnp.minimum(large, half - 1)
    bucket = jnp.where(rel > 0, half, 0) + jnp.where(n < max_exact, n, large)
    tk, tq = bucket.shape
    out = []
    for h in range(heads):
        tab = jnp.broadcast_to(tab_ref[h:h + 1, :], (tk, LANES))
        cols = [jnp.take_along_axis(tab, bucket[:, c * LANES:(c + 1) * LANES], axis=1,
                                    mode="promise_in_bounds")
                for c in range(tq // LANES)]
        out.append(jnp.concatenate(cols, axis=1))
    return out


def _dsa_kernel(pmin_ref, pmax_ref,
                qa_ref, qi_ref, wi_ref, sm_ref, vat_ref, posq_ref, posk_ref, tab_ref, far_ref,
                o_ref,
                keys_ref, m_ref, acc_ref, s_ref, *, tq, topk, nq):
    b = pl.program_id(0)
    i = pl.program_id(1)
    tk = tq
    n_tiles = i + 1

    w_t = wi_ref[...].T[0:IDX_HEADS, :]
    key_l = lax.broadcasted_iota(I32, (tk, tq), 0)
    q_l = lax.broadcasted_iota(I32, (tk, tq), 1)
    diag_ok = key_l < (q_l // CHUNK + 1) * CHUNK

    def score_tile(j, diag):
        r0 = pl.multiple_of(j * tk, tk)
        ki_lo = sm_ref[pl.ds(r0, tk), 256:384]
        ki_hi = sm_ref[pl.ds(r0, tk), 384:512]
        score = None
        for p in range(IDX_HEADS // 2):
            qp = qi_ref[:, p * LANES:(p + 1) * LANES]
            for h, kk in ((2 * p, ki_lo), (2 * p + 1, ki_hi)):
                t = w_t[h:h + 1, :] * jnp.maximum(_dot_nt(kk, qp), 0.0)
                score = t if score is None else score + t
        bits = lax.bitcast_convert_type(score + 0.0, I32)
        key = bits ^ ((bits >> 31) & 0x7FFFFFFF)
        if diag:
            key = jnp.where(diag_ok, key, INT_MIN)
        keys_ref[pl.ds(r0, tk), :] = key

    lax.fori_loop(0, i, lambda j, c: (score_tile(j, False), c)[1], 0)
    score_tile(i, True)

    def count_tiles(pred):
        def body(j, acc):
            r0 = pl.multiple_of(j * tk, tk)
            return acc + _rowsum8(jnp.where(pred(keys_ref[pl.ds(r0, tk), :], j), 1, 0))
        acc = lax.fori_loop(0, n_tiles, body, jnp.zeros((SUBLANES, tq), I32))
        return jnp.sum(acc, axis=0, keepdims=True)

    q_row = lax.broadcasted_iota(I32, (1, tq), 1)
    n_adm = i * tq + (q_row // CHUNK + 1) * CHUNK

    def bit_cond(st):
        return (st[0] < 32) & (st[3] > topk)

    def bit_step(st):
        p, prefix, cnt_pref, _ = st
        cand = prefix ^ jnp.left_shift(jnp.int32(1), 31 - p)
        cnt = count_tiles(lambda k_, j: k_ >= cand)
        take = cnt >= topk
        cnt_pref = jnp.where(take, cnt, cnt_pref)
        return p + 1, jnp.where(take, cand, prefix), cnt_pref, jnp.max(cnt_pref)

    _, tau, cnt_pref, worst = lax.while_loop(
        bit_cond, bit_step,
        (jnp.int32(0), jnp.full((1, tq), INT_MIN, I32), n_adm, jnp.max(n_adm)))
    tau_sel = jnp.maximum(tau, INT_MIN + 1)

    @pl.when(worst > topk)
    def _ties():
        cnt_gt = count_tiles(lambda k_, j: k_ > tau)
        room = topk - cnt_gt

        def idx_step(p, jmax):
            cand = jmax | jnp.left_shift(jnp.int32(1), 14 - p)
            cnt = count_tiles(lambda k_, j: (k_ == tau) & (key_l + j * tk < cand))
            return jnp.where(cnt <= room, cand, jmax)

        jmax = lax.fori_loop(0, 15, idx_step, jnp.zeros((1, tq), I32))
        over = cnt_pref > topk

        def demote(j, carry):
            r0 = pl.multiple_of(j * tk, tk)
            k_ = keys_ref[pl.ds(r0, tk), :]
            drop = (k_ == tau) & (key_l + j * tk >= jmax) & over
            keys_ref[pl.ds(r0, tk), :] = jnp.where(drop, k_ - 1, k_)
            return carry

        lax.fori_loop(0, n_tiles, demote, 0)

    m_ref[...] = jnp.full(m_ref.shape, NEG_BIG, F32)
    acc_ref[...] = jnp.zeros(acc_ref.shape, F32)
    posq = posq_ref[0]

    def attn_tile(j, near):
        r0 = pl.multiple_of(j * tk, tk)
        ka_lo = sm_ref[pl.ds(r0, tk), 0:128]
        ka_hi = sm_ref[pl.ds(r0, tk), 128:256]
        va_t = vat_ref[:, pl.ds(r0, tk)]
        sel = keys_ref[pl.ds(r0, tk), :] >= tau_sel
        if near:
            posk_row = lax.bitcast_convert_type(posk_ref[0, :, pl.ds(r0, tk)], F32)
            posk_col = lax.bitcast_convert_type(
                jnp.broadcast_to(posk_row, (LANES, tk)).T, I32)
            bias = _t5_bias_tiles(jnp.tile(posk_col, (1, tq // LANES)), posq, tab_ref, A_HEADS)
        m_prev = m_ref[...]
        m_next = []
        for g in range(A_HEADS // DSA_HEAD_GROUP):
            heads = range(g * DSA_HEAD_GROUP, (g + 1) * DSA_HEAD_GROUP)
            for h in heads:
                s = _dot_nt(ka_hi if h % 2 else ka_lo,
                            qa_ref[:, (h // 2) * LANES:(h // 2 + 1) * LANES])
                s = s + (bias[h] if near else far_ref[h])
                s = jnp.where(sel, s, NEG_BIG)
                s_ref[h % DSA_HEAD_GROUP] = s
                m_next.append(jnp.maximum(m_prev[h:h + 1, :],
                                          jnp.max(s, axis=0, keepdims=True)))
            for h in heads:
                alpha = jnp.exp2(m_prev[h:h + 1, :] - m_next[h])
                pr = jnp.exp2(s_ref[h % DSA_HEAD_GROUP] - m_next[h]).astype(BF16)
                acc_ref[h] = alpha * acc_ref[h] + _dot(va_t, pr)
        m_ref[...] = jnp.concatenate(m_next, axis=0)

    def attn_body(j, carry):
        is_far = pmax_ref[b * nq + j] - pmin_ref[b * nq + i] <= -MAX_DISTANCE
        lax.cond(is_far, lambda: attn_tile(j, False), lambda: attn_tile(j, True))
        return carry

    lax.fori_loop(0, n_tiles, attn_body, 0)

    for h in range(A_HEADS):
        a = acc_ref[h]
        o_ref[:, h * LANES:(h + 1) * LANES] = (a / a[V_DIM:V_DIM + 1, :]).T.astype(BF16)


def _dsa(qa, qi, wi, sm, vat, posq, posk, tab, far, pmin, pmax, batch, seq, tq, topk):
    n = qa.shape[0]
    nq = seq // tq
    row = lambda w: pl.BlockSpec((tq, w), lambda b, i, *_: (b * nq + i, 0))
    grid_spec = pltpu.PrefetchScalarGridSpec(
        num_scalar_prefetch=2,
        grid=(batch, nq),
        in_specs=[row(512), row(512), row(LANES),
                  pl.BlockSpec((seq, 512), lambda b, i, *_: (b, 0),
                               pipeline_mode=pl.Buffered(1)),
                  pl.BlockSpec((LANES, seq), lambda b, i, *_: (0, b)),
                  pl.BlockSpec((1, 1, tq), lambda b, i, *_: (b * nq + i, 0, 0)),
                  pl.BlockSpec((1, 1, seq), lambda b, i, *_: (b, 0, 0)),
                  pl.BlockSpec((A_HEADS, LANES), lambda b, i, *_: (0, 0)),
                  pl.BlockSpec(memory_space=pltpu.SMEM)],
        out_specs=row(A_HEADS * LANES),
        scratch_shapes=[pltpu.VMEM((seq, tq), I32),
                        pltpu.VMEM((A_HEADS, tq), F32),
                        pltpu.VMEM((A_HEADS, LANES, tq), F32),
                        pltpu.VMEM((DSA_HEAD_GROUP, tq, tq), F32)],
    )
    return pl.pallas_call(
        functools.partial(_dsa_kernel, tq=tq, topk=topk, nq=nq),
        grid_spec=grid_spec,
        out_shape=jax.ShapeDtypeStruct((n, A_HEADS * LANES), BF16),
        compiler_params=pltpu.CompilerParams(vmem_limit_bytes=VMEM_LIMIT),
        name="dsa",
    )(pmin, pmax, qa, qi, wi, sm, vat, posq, posk, tab, far)


MLA_HEAD_GROUP = 4


def _mla_kernel(q_ref, k_ref, vt_ref, o_ref, acc_ref, s_ref, *, tq):
    i = pl.program_id(2)
    tk = tq
    hg = MLA_HEAD_GROUP
    acc_ref[...] = jnp.zeros(acc_ref.shape, F32)

    def tile(j, ms, diag):
        r0 = pl.multiple_of(j * tk, tk)
        if diag:
            key_l = lax.broadcasted_iota(I32, (tk, tq), 0)
            q_l = lax.broadcasted_iota(I32, (tk, tq), 1)
            ok = key_l < (q_l // CHUNK + 1) * CHUNK
        head = lambda h: slice(h * LANES, (h + 1) * LANES)
        out = []
        for h in range(hg):
            s = _dot_nt(k_ref[pl.ds(r0, tk), head(h)], q_ref[:, head(h)])
            if diag:
                s = jnp.where(ok, s, NEG_BIG)
            s_ref[h] = s
            out.append(jnp.maximum(ms[h], jnp.max(s, axis=0, keepdims=True)))
        for h in range(hg):
            alpha = jnp.exp2(ms[h] - out[h])
            pr = jnp.exp2(s_ref[h] - out[h]).astype(BF16)
            acc_ref[h] = alpha * acc_ref[h] + _dot(vt_ref[head(h), pl.ds(r0, tk)], pr)
        return tuple(out)

    m0 = tuple(jnp.full((1, tq), NEG_BIG, F32) for _ in range(hg))
    ms = lax.fori_loop(0, i, lambda j, m_: tile(j, m_, False), m0)
    tile(i, ms, True)
    for h in range(hg):
        a = acc_ref[h]
        o_ref[:, h * LANES:(h + 1) * LANES] = (a / a[V_DIM:V_DIM + 1, :]).T.astype(BF16)


def _mla(qb, kb, vbt, batch, seq, tq):
    n = qb.shape[0]
    nq = seq // tq
    hg = MLA_HEAD_GROUP
    ng = B_HEADS // hg
    w = hg * LANES
    return pl.pallas_call(
        functools.partial(_mla_kernel, tq=tq),
        grid=(batch, ng, nq),
        in_specs=[pl.BlockSpec((tq, w), lambda b, g, i: (b * nq + i, g)),
                  pl.BlockSpec((seq, w), lambda b, g, i: (b, g)),
                  pl.BlockSpec((w, seq), lambda b, g, i: (b * ng + g, 0))],
        out_specs=pl.BlockSpec((tq, w), lambda b, g, i: (b * nq + i, g)),
        out_shape=jax.ShapeDtypeStruct((n, B_HEADS * LANES), BF16),
        scratch_shapes=[pltpu.VMEM((hg, LANES, tq), F32), pltpu.VMEM((hg, tq, tq), F32)],
        compiler_params=pltpu.CompilerParams(vmem_limit_bytes=VMEM_LIMIT),
        name="mla",
    )(qb, kb, vbt)


def _tail_kernel(x_ref, oa_ref, ob_ref, gt_ref, mod_ref, woa_ref, wob_ref, wout_ref,
                 g2_ref, win_ref, wdn_ref, gf_ref, o_ref, *, d, d_ff, chunk):
    x = x_ref[...]
    gt2 = mod_ref[0, 5:6, :]
    sh3, sc3, gt3 = mod_ref[0, 6:7, :], mod_ref[0, 7:8, :], mod_ref[0, 8:9, :]
    ya = _dot(oa_ref[...], woa_ref[...])
    yb = _dot(ob_ref[...], wob_ref[...])
    y = gt_ref[:, 0:d].astype(F32) * ya + gt_ref[:, d:2 * d].astype(F32) * yb
    x = x + gt2 * _dot(y.astype(BF16), wout_ref[...])
    h = _rms(x, g2_ref[...]) * (1.0 + sc3) + sh3
    f = _swiglu_tile(h.astype(BF16), win_ref, wdn_ref, d_ff, chunk)
    x = x + (0.5 * gt3) * f
    o_ref[...] = _rms(x, gf_ref[...])


def _tail(x1, oa, ob, gates, mod3, woa, wob, wout, g2, w_in, w_down, gf, seq, tm):
    n, d = x1.shape
    d_ff = w_down.shape[0]
    tiles_per_batch = seq // tm
    row = lambda w: pl.BlockSpec((tm, w), lambda i: (i, 0))
    kern = functools.partial(_tail_kernel, d=d, d_ff=d_ff, chunk=_ffn_chunk(d_ff))
    return pl.pallas_call(
        kern,
        grid=(n // tm,),
        in_specs=[row(d), row(oa.shape[1]), row(ob.shape[1]), row(2 * d),
                  pl.BlockSpec((1, N_ADA, d), lambda i: (i // tiles_per_batch, 0, 0)),
                  _const_spec(woa.shape), _const_spec(wob.shape), _const_spec(wout.shape),
                  _const_spec((1, d)), _const_spec(w_in.shape), _const_spec(w_down.shape),
                  _const_spec((1, d))],
        out_specs=row(d),
        out_shape=jax.ShapeDtypeStruct((n, d), F32),
        compiler_params=pltpu.CompilerParams(vmem_limit_bytes=VMEM_LIMIT),
        name="tail",
    )(x1, oa, ob, gates, mod3, woa, wob, wout, g2, w_in, w_down, gf)


def _pad_heads(w, heads, width):
    k = w.shape[0]
    w3 = w.reshape(k, heads, width)
    return jnp.pad(w3, ((0, 0), (0, 0), (0, LANES - width))).reshape(k, heads * LANES)


def _pad_head_rows(w, heads, width):
    dd = w.shape[1]
    w3 = w.reshape(heads, width, dd)
    return jnp.pad(w3, ((0, 0), (0, LANES - width), (0, 0))).reshape(heads * LANES, dd)


def _layout_w_in(w_in, d):
    z = lambda c: jnp.zeros((d, c), w_in.dtype)
    o = 0
    qa = w_in[:, o:o + 512] * (A_HEAD_DIM ** -0.5 * LOG2E); o += 512
    ka = w_in[:, o:o + 64]; o += 64
    va = w_in[:, o:o + 64]; o += 64
    qi = w_in[:, o:o + 512]; o += 512
    ki = w_in[:, o:o + 64]; o += 64
    wi = w_in[:, o:o + IDX_HEADS]; o += IDX_HEADS
    cq = w_in[:, o:o + Q_LORA]; o += Q_LORA
    ckv = w_in[:, o:o + KV_LORA]; o += KV_LORA
    kr = w_in[:, o:o + QK_ROPE]; o += QK_ROPE
    gates = w_in[:, o:o + 2 * d]
    hr = QK_ROPE // 2
    kt = jnp.concatenate([-kr[:, hr:], kr[:, :hr]], axis=1)
    cols = [qa, qi,
            ka, z(64), z(64), ka, va, z(64), ki, z(64), z(64), ki,
            wi, z(LANES - IDX_HEADS),
            z(64), kr, z(32), z(64), kt, z(32),
            cq, ckv, gates]
    return jnp.concatenate(cols, axis=1).astype(BF16)


def _layout_w_uq(w_uq):
    scale = (QK_NOPE + QK_ROPE) ** -0.5 * LOG2E
    k = w_uq.shape[0]
    w3 = w_uq.reshape(k, B_HEADS, QK_NOPE + QK_ROPE) * scale
    nope, pe = w3[:, :, :QK_NOPE], w3[:, :, QK_NOPE:]
    hr = QK_ROPE // 2
    rot = jnp.concatenate([-pe[:, :, hr:], pe[:, :, :hr]], axis=2)
    zpad = jnp.zeros((k, B_HEADS, LANES - QK_NOPE - QK_ROPE), w_uq.dtype)
    plain = jnp.concatenate([nope, pe, zpad], axis=2).reshape(k, B_HEADS * LANES)
    rotw = jnp.concatenate([jnp.zeros_like(nope), rot, zpad], axis=2).reshape(k, B_HEADS * LANES)
    return plain.astype(BF16), rotw.astype(BF16)


def _rope_tables(positions):
    half = QK_ROPE // 2
    freqs = ROPE_THETA ** (-2.0 * jnp.arange(half, dtype=F32) / QK_ROPE)
    ang = positions.astype(F32).reshape(-1)[:, None] * freqs
    cos, sin = jnp.cos(ang), jnp.sin(ang)
    n = ang.shape[0]
    ones = jnp.ones((n, QK_NOPE), F32)
    z32 = jnp.zeros((n, LANES - QK_NOPE - QK_ROPE), F32)
    cs = jnp.concatenate([ones, cos, cos, z32], axis=1)
    sn = jnp.concatenate([jnp.zeros((n, QK_NOPE), F32), sin, sin, z32], axis=1)
    return cs, sn


def _tiles(seq):
    tm = 512 if seq % 512 == 0 else 256
    tq_dsa = 512 if seq % 512 == 0 else 256
    tq_mla = 512 if seq % 512 == 0 else 256
    return tm, tq_dsa, tq_mla


def kernel(x, c, positions, w_ada, b_ada, g_ffn1, w_ffn1_in, w_ffn1_down, g_mix, w_in,
           g_cq, w_uq, g_ckv, w_uk, w_uv, rel_bias, w_o_a, w_o_b, w_out,
           g_ffn2, w_ffn2_in, w_ffn2_down, g_final):
    batch, seq, d = x.shape
    depth = w_ada.shape[0]
    n = batch * seq
    topk = min(TOPK_MAX, seq // 4)
    tm, tq_dsa, tq_mla = _tiles(seq)
    assert seq % tq_dsa == 0 and seq % tm == 0 and d % LANES == 0
    assert depth == 1, "the tail kernel fuses the final norm into the single layer"

    xf = x.reshape(n, d)
    c_pad = jnp.pad(c, ((0, 8 - batch % 8 if batch % 8 else 0), (0, 0)))
    cs, sn = _rope_tables(positions)
    pos = positions.astype(I32)
    posq = pos.reshape(n // tq_dsa, 1, tq_dsa)
    posk = pos.reshape(batch, 1, seq)
    ptile = pos.reshape(n // tq_dsa, tq_dsa)
    pmin, pmax = ptile.min(axis=1), ptile.max(axis=1)
    tab = jnp.pad(rel_bias.T.astype(F32) * LOG2E, ((0, 0), (0, LANES - NUM_BUCKETS)))
    far = rel_bias[NUM_BUCKETS // 2 - 1].astype(F32) * LOG2E

    l = 0
    mod = _ada(c_pad, w_ada[l], b_ada[l])[:batch]
    mod3 = mod.reshape(batch, N_ADA, d)
    x1 = _ffn1(xf, mod3, g_ffn1[l].reshape(1, d), w_ffn1_in[l].astype(BF16),
               w_ffn1_down[l].astype(BF16), seq, tm)
    wq, wqr = _layout_w_uq(w_uq[l])
    qa, qi, sm, va, wi, qb, kb, vb, gates = _proj(
        x1, mod3, g_mix[l].reshape(1, d), _layout_w_in(w_in[l], d),
        g_cq[l].reshape(1, Q_LORA), wq, wqr, g_ckv[l].reshape(1, KV_LORA),
        _pad_heads(w_uk[l], B_HEADS, QK_NOPE).astype(BF16),
        _pad_heads(w_uv[l], B_HEADS, V_DIM).astype(BF16),
        cs, sn, seq, tm)
    vat = va.T
    vbt = vb.reshape(batch, seq, B_HEADS * LANES).transpose(0, 2, 1).reshape(-1, seq)
    oa = _dsa(qa, qi, wi, sm, vat, posq, posk, tab, far, pmin, pmax, batch, seq, tq_dsa, topk)
    ob = _mla(qb, kb, vbt, batch, seq, tq_mla)
    out = _tail(x1, oa, ob, gates, mod3,
                _pad_head_rows(w_o_a[l], A_HEADS, A_HEAD_DIM).astype(BF16),
                _pad_head_rows(w_o_b[l], B_HEADS, V_DIM).astype(BF16),
                w_out[l].astype(BF16), g_ffn2[l].reshape(1, d),
                w_ffn2_in[l].astype(BF16), w_ffn2_down[l].astype(BF16),
                g_final.reshape(1, d), seq, tm)
    return out.reshape(batch, seq, d)
```

```python
import functools
import math

import jax
import jax.numpy as jnp
from jax import lax
from jax.experimental import pallas as pl
from jax.experimental.pallas import tpu as pltpu

F32 = jnp.float32
BF16 = jnp.bfloat16
I32 = jnp.int32

CHUNK = 64
EPS = 1e-6
A_HEADS = 8
A_HEAD_DIM = 64
IDX_HEADS = 8
IDX_DIM = 64
TOPK_MAX = 256
B_HEADS = 8
QK_NOPE = 64
QK_ROPE = 32
V_DIM = 64
Q_LORA = 384
KV_LORA = 256
ROPE_THETA = 10000.0
NUM_BUCKETS = 32
MAX_DISTANCE = 128
N_ADA = 9

LANES = 128
SUBLANES = 8
VMEM_LIMIT = 56 * 1024 * 1024
INT_MIN = -2 ** 31
NEG_BIG = -1e30
LOG2E = math.log2(math.e)

C_QA = 0
C_QI = 512
C_SM = 1024
C_WI = 1664
C_CQ = 2048
C_CKV = C_CQ + Q_LORA
C_GT = C_CKV + KV_LORA


def _dot(a, b):
    return jnp.dot(a, b, preferred_element_type=F32)


def _dot_nt(a, b):
    return lax.dot_general(a, b, (((1,), (1,)), ((), ())), preferred_element_type=F32)


def _dot_tn(a, b):
    return lax.dot_general(a, b, (((0,), (0,)), ((), ())), preferred_element_type=F32)


def _rms(x, g):
    return x * lax.rsqrt(jnp.mean(x * x, axis=-1, keepdims=True) + EPS) * g


def _const_spec(shape):
    nd = len(shape)
    return pl.BlockSpec(shape, lambda *_: (0,) * nd, pipeline_mode=pl.Buffered(1))


def _ada_kernel(c_ref, w_ref, b_ref, o_ref):
    c = c_ref[...]
    s = c * jax.nn.sigmoid(c)
    o_ref[...] = _dot(s.astype(BF16), w_ref[...].astype(BF16)) + b_ref[...]


def _ada(c_pad, w_ada, b_ada):
    d = c_pad.shape[1]
    n_out = w_ada.shape[1]
    return pl.pallas_call(
        _ada_kernel,
        grid=(n_out // d,),
        in_specs=[pl.BlockSpec(c_pad.shape, lambda j: (0, 0)),
                  pl.BlockSpec((d, d), lambda j: (0, j)),
                  pl.BlockSpec((1, d), lambda j: (0, j))],
        out_specs=pl.BlockSpec(c_pad.shape, lambda j: (0, j)),
        out_shape=jax.ShapeDtypeStruct((c_pad.shape[0], n_out), F32),
        compiler_params=pltpu.CompilerParams(vmem_limit_bytes=VMEM_LIMIT),
        name="ada",
    )(c_pad, w_ada, b_ada.reshape(1, n_out))


def _swiglu_tile(hb, win_ref, wdn_ref, d_ff, chunk):
    acc = None
    for j in range(d_ff // chunk):
        g = _dot(hb, win_ref[:, j * chunk:(j + 1) * chunk])
        u = _dot(hb, win_ref[:, d_ff + j * chunk:d_ff + (j + 1) * chunk])
        a = (g * jax.nn.sigmoid(g) * u).astype(BF16)
        part = _dot(a, wdn_ref[j * chunk:(j + 1) * chunk, :])
        acc = part if acc is None else acc + part
    return acc


def _ffn_chunk(d_ff):
    for c in (512, 256, 128):
        if d_ff % c == 0:
            return c
    raise ValueError("d_ff must be a multiple of 128")


def _ffn1_kernel(x_ref, mod_ref, g_ref, win_ref, wdn_ref, o_ref, *, d_ff, chunk):
    x = x_ref[...]
    sh, sc, gt = mod_ref[0, 0:1, :], mod_ref[0, 1:2, :], mod_ref[0, 2:3, :]
    h = _rms(x, g_ref[...]) * (1.0 + sc) + sh
    y = _swiglu_tile(h.astype(BF16), win_ref, wdn_ref, d_ff, chunk)
    o_ref[...] = x + (0.5 * gt) * y


def _ffn1(x2, mod3, g, w_in, w_down, seq, tm):
    n, d = x2.shape
    d_ff = w_down.shape[0]
    tiles_per_batch = seq // tm
    kern = functools.partial(_ffn1_kernel, d_ff=d_ff, chunk=_ffn_chunk(d_ff))
    return pl.pallas_call(
        kern,
        grid=(n // tm,),
        in_specs=[pl.BlockSpec((tm, d), lambda i: (i, 0)),
                  pl.BlockSpec((1, N_ADA, d), lambda i: (i // tiles_per_batch, 0, 0)),
                  _const_spec((1, d)),
                  _const_spec(w_in.shape),
                  _const_spec(w_down.shape)],
        out_specs=pl.BlockSpec((tm, d), lambda i: (i, 0)),
        out_shape=jax.ShapeDtypeStruct((n, d), F32),
        compiler_params=pltpu.CompilerParams(vmem_limit_bytes=VMEM_LIMIT),
        name="ffn1",
    )(x2, mod3, g, w_in, w_down)


def _proj_kernel(x_ref, mod_ref, g_ref, w_ref, gcq_ref, wq_ref, wqr_ref, gckv_ref,
                 wk_ref, wv_ref, cs_ref, sn_ref,
                 qa_ref, qi_ref, sm_ref, va_ref, wi_ref, qb_ref, kb_ref, vb_ref, gt_ref, *, d):
    x = x_ref[...]
    sh, sc = mod_ref[0, 3:4, :], mod_ref[0, 4:5, :]
    hb = (_rms(x, g_ref[...]) * (1.0 + sc) + sh).astype(BF16)
    cs = cs_ref[...]
    sn = sn_ref[...]
    lane = lax.broadcasted_iota(I32, (1, LANES), 1)
    one_at_v = (lane == V_DIM).astype(F32)

    qa_ref[...] = _dot(hb, w_ref[:, C_QA:C_QA + 512]).astype(BF16)
    qi_ref[...] = _dot(hb, w_ref[:, C_QI:C_QI + 512]).astype(BF16)

    sm = _dot(hb, w_ref[:, C_SM:C_SM + 640])
    sm_ref[:, 0:256] = sm[:, 0:256].astype(BF16)
    sm_ref[:, 256:512] = sm[:, 384:640].astype(BF16)
    va_ref[...] = (sm[:, 256:384] + one_at_v).astype(BF16)

    misc = _dot(hb, w_ref[:, C_WI:C_WI + 384])
    wi_ref[...] = misc[:, 0:128]
    kpe = misc[:, 128:256] * cs + misc[:, 256:384] * sn

    lat = _dot(hb, w_ref[:, C_CQ:C_GT])
    cq = _rms(lat[:, 0:Q_LORA], gcq_ref[...]).astype(BF16)
    ckv = _rms(lat[:, Q_LORA:Q_LORA + KV_LORA], gckv_ref[...]).astype(BF16)
    qn = _dot(cq, wq_ref[...])
    qr = _dot(cq, wqr_ref[...])
    kn = _dot(ckv, wk_ref[...])
    vv = _dot(ckv, wv_ref[...])
    for h in range(B_HEADS):
        sl = slice(h * LANES, (h + 1) * LANES)
        qb_ref[:, sl] = (qn[:, sl] * cs + qr[:, sl] * sn).astype(BF16)
        kb_ref[:, sl] = (kn[:, sl] + kpe).astype(BF16)
        vb_ref[:, sl] = (vv[:, sl] + one_at_v).astype(BF16)

    gl = _dot(hb, w_ref[:, C_GT:C_GT + 2 * d])
    gt_ref[...] = jax.nn.sigmoid(gl).astype(BF16)


def _proj(x1, mod3, g, w_all, gcq, wq, wqr, gckv, wk, wv, cs, sn, seq, tm):
    n, d = x1.shape
    tiles_per_batch = seq // tm
    row = lambda w: pl.BlockSpec((tm, w), lambda i: (i, 0))
    out_widths = (512, 512, 512, 128, 128, 1024, 1024, 1024, 2 * d)
    out_dtypes = (BF16, BF16, BF16, BF16, F32, BF16, BF16, BF16, BF16)
    return pl.pallas_call(
        functools.partial(_proj_kernel, d=d),
        grid=(n // tm,),
        in_specs=[row(d),
                  pl.BlockSpec((1, N_ADA, d), lambda i: (i // tiles_per_batch, 0, 0)),
                  _const_spec((1, d)), _const_spec(w_all.shape),
                  _const_spec(gcq.shape), _const_spec(wq.shape), _const_spec(wqr.shape),
                  _const_spec(gckv.shape), _const_spec(wk.shape), _const_spec(wv.shape),
                  row(LANES), row(LANES)],
        out_specs=[row(w) for w in out_widths],
        out_shape=[jax.ShapeDtypeStruct((n, w), dt) for w, dt in zip(out_widths, out_dtypes)],
        compiler_params=pltpu.CompilerParams(vmem_limit_bytes=VMEM_LIMIT),
        name="proj",
    )(x1, mod3, g, w_all, gcq, wq, wqr, gckv, wk, wv, cs, sn)


DSA_HEAD_GROUP = 4
ZERO_BAND = 1 << 20
GROUPS = 256
COUNT_UNKNOWN = 1 << 30


def _rowsum8(x):
    v = x.reshape(x.shape[0] // SUBLANES, SUBLANES, x.shape[1])
    while v.shape[0] > 1:
        half = v.shape[0] // 2
        v = v[:half] + v[half:]
    return v[0]


def _t5_bias_tiles(posk, posq, tab_ref, heads):
    half = NUM_BUCKETS // 2
    max_exact = half // 2
    rel = posk - posq
    n = jnp.abs(rel)
    nf = jnp.maximum(n, 1).astype(F32)
    large = max_exact + (jnp.log(nf / max_exact) / math.log(MAX_DISTANCE / max_exact)
                         * (half - max_exact)).astype(I32)
    large = jnp.minimum(large, half - 1)
    bucket = jnp.where(rel > 0, half, 0) + jnp.where(n < max_exact, n, large)
    tk, tq = bucket.shape
    out = []
    for h in range(heads):
        tab = jnp.broadcast_to(tab_ref[h:h + 1, :], (tk, LANES))
        cols = [jnp.take_along_axis(tab, bucket[:, c * LANES:(c + 1) * LANES], axis=1,
                                    mode="promise_in_bounds")
                for c in range(tq // LANES)]
        out.append(jnp.concatenate(cols, axis=1))
    return out


def _dsa_kernel(pmin_ref, pmax_ref,
                qa_ref, qi_ref, wi_ref, sm_ref, vat_ref, posq_ref, posk_ref, tab_ref, far_ref,
                o_ref,
                keys_ref, gmax_ref, m_ref, acc_ref, s_ref, *, tq, topk, nq):
    b = pl.program_id(0)
    i = pl.program_id(1)
    tk = tq
    n_tiles = i + 1

    w_t = wi_ref[...].T[0:IDX_HEADS, :]
    key_l = lax.broadcasted_iota(I32, (tk, tq), 0)
    q_l = lax.broadcasted_iota(I32, (tk, tq), 1)
    diag_ok = key_l < (q_l // CHUNK + 1) * CHUNK

    def score_tile(j, diag):
        r0 = pl.multiple_of(j * tk, tk)
        ki_lo = sm_ref[pl.ds(r0, tk), 256:384]
        ki_hi = sm_ref[pl.ds(r0, tk), 384:512]
        score = None
        for p in range(IDX_HEADS // 2):
            qp = qi_ref[:, p * LANES:(p + 1) * LANES]
            for h, kk in ((2 * p, ki_lo), (2 * p + 1, ki_hi)):
                t = w_t[h:h + 1, :] * jnp.maximum(_dot_nt(kk, qp), 0.0)
                score = t if score is None else score + t
        score = score + 0.0
        bits = lax.bitcast_convert_type(score, I32)
        key = bits ^ ((bits >> 31) & 0x7FFFFFFF)
        key = jnp.where(key < 0, key - ZERO_BAND, key)
        key = jnp.where(score == 0.0, ~(key_l + j * tk), key)
        if diag:
            key = jnp.where(diag_ok, key, INT_MIN)
        keys_ref[pl.ds(r0, tk), :] = key
        gmax_ref[...] = jnp.maximum(gmax_ref[...],
                                    jnp.max(key.reshape(tk // GROUPS, GROUPS, tq), axis=0))

    gmax_ref[...] = jnp.full(gmax_ref.shape, INT_MIN, I32)
    lax.fori_loop(0, i, lambda j, c: (score_tile(j, False), c)[1], 0)
    score_tile(i, True)

    def count_tiles(pred):
        def body(j, acc):
            r0 = pl.multiple_of(j * tk, tk)
            return acc + _rowsum8(jnp.where(pred(keys_ref[pl.ds(r0, tk), :], j), 1, 0))
        acc = lax.fori_loop(0, n_tiles, body, jnp.zeros((SUBLANES, tq), I32))
        return jnp.sum(acc, axis=0, keepdims=True)

    q_row = lax.broadcasted_iota(I32, (1, tq), 1)
    n_adm = i * tq + (q_row // CHUNK + 1) * CHUNK

    gmax = gmax_ref[...]
    top = jnp.max(gmax, axis=0, keepdims=True)
    low = jnp.min(gmax, axis=0, keepdims=True)
    cnt_pos = count_tiles(lambda k_, j: k_ >= 1)
    cnt_zero = count_tiles(lambda k_, j: k_ >= -ZERO_BAND)
    few = n_adm <= topk
    in_pos = cnt_pos >= topk
    in_zero = jnp.logical_not(in_pos) & (cnt_zero >= topk)
    base_lo = jnp.where(in_pos, 1, jnp.where(in_zero, -ZERO_BAND, INT_MIN + 1))
    base_cnt = jnp.where(in_pos, cnt_pos, jnp.where(in_zero, cnt_zero, n_adm))
    hi0 = jnp.where(in_pos, top + 1, jnp.where(in_zero, 1, -ZERO_BAND))
    tighter = low > base_lo
    lo0 = jnp.where(few, INT_MIN + 1, jnp.where(tighter, low, base_lo))
    cnt0 = jnp.where(few, topk, jnp.where(tighter, COUNT_UNKNOWN, base_cnt))
    hi0 = jnp.where(few, INT_MIN + 2, hi0)

    def unsettled(lo, hi, cnt_lo):
        return jnp.max(jnp.where((cnt_lo != topk) & (hi - lo > 1), 1, 0))

    def bisect_cond(st):
        return (st[0] < 34) & (st[4] > 0)

    def bisect_step(st):
        it, lo, hi, cnt_lo, _ = st
        mid = lo + ((hi - lo) >> 1)
        cnt = count_tiles(lambda k_, j: k_ >= mid)
        take = cnt >= topk
        lo = jnp.where(take, mid, lo)
        hi = jnp.where(take, hi, mid)
        cnt_lo = jnp.where(take, cnt, cnt_lo)
        return it + 1, lo, hi, cnt_lo, unsettled(lo, hi, cnt_lo)

    _, tau, _, _, _ = lax.while_loop(
        bisect_cond, bisect_step, (jnp.int32(0), lo0, hi0, cnt0, unsettled(lo0, hi0, cnt0)))
    cnt_ge = count_tiles(lambda k_, j: k_ >= tau)
    over = (cnt_ge > topk) & jnp.logical_not(few)
    tau_sel = tau

    @pl.when(jnp.max(jnp.where(over, 1, 0)) > 0)
    def _ties():
        cnt_gt = count_tiles(lambda k_, j: k_ > tau)
        room = topk - cnt_gt

        def idx_step(p, jmax):
            cand = jmax | jnp.left_shift(jnp.int32(1), 14 - p)
            cnt = count_tiles(lambda k_, j: (k_ == tau) & (key_l + j * tk < cand))
            return jnp.where(cnt <= room, cand, jmax)

        jmax = lax.fori_loop(0, 15, idx_step, jnp.zeros((1, tq), I32))

        def demote(j, carry):
            r0 = pl.multiple_of(j * tk, tk)
            k_ = keys_ref[pl.ds(r0, tk), :]
            drop = (k_ == tau) & (key_l + j * tk >= jmax) & over
            keys_ref[pl.ds(r0, tk), :] = jnp.where(drop, k_ - 1, k_)
            return carry

        lax.fori_loop(0, n_tiles, demote, 0)

    m_ref[...] = jnp.full(m_ref.shape, NEG_BIG, F32)
    acc_ref[...] = jnp.zeros(acc_ref.shape, F32)
    posq = posq_ref[0]

    def attn_tile(j, near):
        r0 = pl.multiple_of(j * tk, tk)
        ka_lo = sm_ref[pl.ds(r0, tk), 0:128]
        ka_hi = sm_ref[pl.ds(r0, tk), 128:256]
        va_t = vat_ref[:, pl.ds(r0, tk)]
        sel = keys_ref[pl.ds(r0, tk), :] >= tau_sel
        if near:
            posk_row = lax.bitcast_convert_type(posk_ref[0, :, pl.ds(r0, tk)], F32)
            posk_col = lax.bitcast_convert_type(
                jnp.broadcast_to(posk_row, (LANES, tk)).T, I32)
            bias = _t5_bias_tiles(jnp.tile(posk_col, (1, tq // LANES)), posq, tab_ref, A_HEADS)
        m_prev = m_ref[...]
        m_next = []
        for g in range(A_HEADS // DSA_HEAD_GROUP):
            heads = range(g * DSA_HEAD_GROUP, (g + 1) * DSA_HEAD_GROUP)
            for h in heads:
                s = _dot_nt(ka_hi if h % 2 else ka_lo,
                            qa_ref[:, (h // 2) * LANES:(h // 2 + 1) * LANES])
                s = s + (bias[h] if near else far_ref[h])
                s = jnp.where(sel, s, NEG_BIG)
                s_ref[h % DSA_HEAD_GROUP] = s
                m_next.append(jnp.maximum(m_prev[h:h + 1, :],
                                          jnp.max(s, axis=0, keepdims=True)))
            for h in heads:
                alpha = jnp.exp2(m_prev[h:h + 1, :] - m_next[h])
                pr = jnp.exp2(s_ref[h % DSA_HEAD_GROUP] - m_next[h]).astype(BF16)
                acc_ref[h] = alpha * acc_ref[h] + _dot(va_t, pr)
        m_ref[...] = jnp.concatenate(m_next, axis=0)

    def attn_body(j, carry):
        is_far = pmax_ref[b * nq + j] - pmin_ref[b * nq + i] <= -MAX_DISTANCE
        lax.cond(is_far, lambda: attn_tile(j, False), lambda: attn_tile(j, True))
        return carry

    lax.fori_loop(0, n_tiles, attn_body, 0)

    for h in range(A_HEADS):
        a = acc_ref[h]
        o_ref[:, h * LANES:(h + 1) * LANES] = (a / a[V_DIM:V_DIM + 1, :]).T.astype(BF16)


def _dsa(qa, qi, wi, sm, vat, posq, posk, tab, far, pmin, pmax, batch, seq, tq, topk):
    n = qa.shape[0]
    nq = seq // tq
    row = lambda w: pl.BlockSpec((tq, w), lambda b, i, *_: (b * nq + i, 0))
    grid_spec = pltpu.PrefetchScalarGridSpec(
        num_scalar_prefetch=2,
        grid=(batch, nq),
        in_specs=[row(512), row(512), row(LANES),
                  pl.BlockSpec((seq, 512), lambda b, i, *_: (b, 0),
                               pipeline_mode=pl.Buffered(1)),
                  pl.BlockSpec((LANES, seq), lambda b, i, *_: (0, b)),
                  pl.BlockSpec((1, 1, tq), lambda b, i, *_: (b * nq + i, 0, 0)),
                  pl.BlockSpec((1, 1, seq), lambda b, i, *_: (b, 0, 0)),
                  pl.BlockSpec((A_HEADS, LANES), lambda b, i, *_: (0, 0)),
                  pl.BlockSpec(memory_space=pltpu.SMEM)],
        out_specs=row(A_HEADS * LANES),
        scratch_shapes=[pltpu.VMEM((seq, tq), I32),
                        pltpu.VMEM((GROUPS, tq), I32),
                        pltpu.VMEM((A_HEADS, tq), F32),
                        pltpu.VMEM((A_HEADS, LANES, tq), F32),
                        pltpu.VMEM((DSA_HEAD_GROUP, tq, tq), F32)],
    )
    return pl.pallas_call(
        functools.partial(_dsa_kernel, tq=tq, topk=topk, nq=nq),
        grid_spec=grid_spec,
        out_shape=jax.ShapeDtypeStruct((n, A_HEADS * LANES), BF16),
        compiler_params=pltpu.CompilerParams(vmem_limit_bytes=VMEM_LIMIT),
        name="dsa",
    )(pmin, pmax, qa, qi, wi, sm, vat, posq, posk, tab, far)


MLA_HEAD_GROUP = 4


def _mla_kernel(q_ref, k_ref, vt_ref, o_ref, acc_ref, s_ref, *, tq):
    i = pl.program_id(2)
    tk = tq
    hg = MLA_HEAD_GROUP
    acc_ref[...] = jnp.zeros(acc_ref.shape, F32)

    def tile(j, ms, diag):
        r0 = pl.multiple_of(j * tk, tk)
        if diag:
            key_l = lax.broadcasted_iota(I32, (tk, tq), 0)
            q_l = lax.broadcasted_iota(I32, (tk, tq), 1)
            ok = key_l < (q_l // CHUNK + 1) * CHUNK
        head = lambda h: slice(h * LANES, (h + 1) * LANES)
        out = []
        for h in range(hg):
            s = _dot_nt(k_ref[pl.ds(r0, tk), head(h)], q_ref[:, head(h)])
            if diag:
                s = jnp.where(ok, s, NEG_BIG)
            s_ref[h] = s
            out.append(jnp.maximum(ms[h], jnp.max(s, axis=0, keepdims=True)))
        for h in range(hg):
            alpha = jnp.exp2(ms[h] - out[h])
            pr = jnp.exp2(s_ref[h] - out[h]).astype(BF16)
            acc_ref[h] = alpha * acc_ref[h] + _dot(vt_ref[head(h), pl.ds(r0, tk)], pr)
        return tuple(out)

    m0 = tuple(jnp.full((1, tq), NEG_BIG, F32) for _ in range(hg))
    ms = lax.fori_loop(0, i, lambda j, m_: tile(j, m_, False), m0)
    tile(i, ms, True)
    for h in range(hg):
        a = acc_ref[h]
        o_ref[:, h * LANES:(h + 1) * LANES] = (a / a[V_DIM:V_DIM + 1, :]).T.astype(BF16)


def _mla(qb, kb, vbt, batch, seq, tq):
    n = qb.shape[0]
    nq = seq // tq
    hg = MLA_HEAD_GROUP
    ng = B_HEADS // hg
    w = hg * LANES
    return pl.pallas_call(
        functools.partial(_mla_kernel, tq=tq),
        grid=(batch, ng, nq),
        in_specs=[pl.BlockSpec((tq, w), lambda b, g, i: (b * nq + i, g)),
                  pl.BlockSpec((seq, w), lambda b, g, i: (b, g)),
                  pl.BlockSpec((w, seq), lambda b, g, i: (b * ng + g, 0))],
        out_specs=pl.BlockSpec((tq, w), lambda b, g, i: (b * nq + i, g)),
        out_shape=jax.ShapeDtypeStruct((n, B_HEADS * LANES), BF16),
        scratch_shapes=[pltpu.VMEM((hg, LANES, tq), F32), pltpu.VMEM((hg, tq, tq), F32)],
        compiler_params=pltpu.CompilerParams(vmem_limit_bytes=VMEM_LIMIT),
        name="mla",
    )(qb, kb, vbt)


def _tail_kernel(x_ref, oa_ref, ob_ref, gt_ref, mod_ref, woa_ref, wob_ref, wout_ref,
                 g2_ref, win_ref, wdn_ref, gf_ref, o_ref, *, d, d_ff, chunk):
    x = x_ref[...]
    gt2 = mod_ref[0, 5:6, :]
    sh3, sc3, gt3 = mod_ref[0, 6:7, :], mod_ref[0, 7:8, :], mod_ref[0, 8:9, :]
    ya = _dot(oa_ref[...], woa_ref[...])
    yb = _dot(ob_ref[...], wob_ref[...])
    y = gt_ref[:, 0:d].astype(F32) * ya + gt_ref[:, d:2 * d].astype(F32) * yb
    x = x + gt2 * _dot(y.astype(BF16), wout_ref[...])
    h = _rms(x, g2_ref[...]) * (1.0 + sc3) + sh3
    f = _swiglu_tile(h.astype(BF16), win_ref, wdn_ref, d_ff, chunk)
    x = x + (0.5 * gt3) * f
    o_ref[...] = _rms(x, gf_ref[...])


def _tail(x1, oa, ob, gates, mod3, woa, wob, wout, g2, w_in, w_down, gf, seq, tm):
    n, d = x1.shape
    d_ff = w_down.shape[0]
    tiles_per_batch = seq // tm
    row = lambda w: pl.BlockSpec((tm, w), lambda i: (i, 0))
    kern = functools.partial(_tail_kernel, d=d, d_ff=d_ff, chunk=_ffn_chunk(d_ff))
    return pl.pallas_call(
        kern,
        grid=(n // tm,),
        in_specs=[row(d), row(oa.shape[1]), row(ob.shape[1]), row(2 * d),
                  pl.BlockSpec((1, N_ADA, d), lambda i: (i // tiles_per_batch, 0, 0)),
                  _const_spec(woa.shape), _const_spec(wob.shape), _const_spec(wout.shape),
                  _const_spec((1, d)), _const_spec(w_in.shape), _const_spec(w_down.shape),
                  _const_spec((1, d))],
        out_specs=row(d),
        out_shape=jax.ShapeDtypeStruct((n, d), F32),
        compiler_params=pltpu.CompilerParams(vmem_limit_bytes=VMEM_LIMIT),
        name="tail",
    )(x1, oa, ob, gates, mod3, woa, wob, wout, g2, w_in, w_down, gf)


def _pad_heads(w, heads, width):
    k = w.shape[0]
    w3 = w.reshape(k, heads, width)
    return jnp.pad(w3, ((0, 0), (0, 0), (0, LANES - width))).reshape(k, heads * LANES)


def _pad_head_rows(w, heads, width):
    dd = w.shape[1]
    w3 = w.reshape(heads, width, dd)
    return jnp.pad(w3, ((0, 0), (0, LANES - width), (0, 0))).reshape(heads * LANES, dd)


def _layout_w_in(w_in, d):
    z = lambda c: jnp.zeros((d, c), w_in.dtype)
    o = 0
    qa = w_in[:, o:o + 512] * (A_HEAD_DIM ** -0.5 * LOG2E); o += 512
    ka = w_in[:, o:o + 64]; o += 64
    va = w_in[:, o:o + 64]; o += 64
    qi = w_in[:, o:o + 512]; o += 512
    ki = w_in[:, o:o + 64]; o += 64
    wi = w_in[:, o:o + IDX_HEADS]; o += IDX_HEADS
    cq = w_in[:, o:o + Q_LORA]; o += Q_LORA
    ckv = w_in[:, o:o + KV_LORA]; o += KV_LORA
    kr = w_in[:, o:o + QK_ROPE]; o += QK_ROPE
    gates = w_in[:, o:o + 2 * d]
    hr = QK_ROPE // 2
    kt = jnp.concatenate([-kr[:, hr:], kr[:, :hr]], axis=1)
    cols = [qa, qi,
            ka, z(64), z(64), ka, va, z(64), ki, z(64), z(64), ki,
            wi, z(LANES - IDX_HEADS),
            z(64), kr, z(32), z(64), kt, z(32),
            cq, ckv, gates]
    return jnp.concatenate(cols, axis=1).astype(BF16)


def _layout_w_uq(w_uq):
    scale = (QK_NOPE + QK_ROPE) ** -0.5 * LOG2E
    k = w_uq.shape[0]
    w3 = w_uq.reshape(k, B_HEADS, QK_NOPE + QK_ROPE) * scale
    nope, pe = w3[:, :, :QK_NOPE], w3[:, :, QK_NOPE:]
    hr = QK_ROPE // 2
    rot = jnp.concatenate([-pe[:, :, hr:], pe[:, :, :hr]], axis=2)
    zpad = jnp.zeros((k, B_HEADS, LANES - QK_NOPE - QK_ROPE), w_uq.dtype)
    plain = jnp.concatenate([nope, pe, zpad], axis=2).reshape(k, B_HEADS * LANES)
    rotw = jnp.concatenate([jnp.zeros_like(nope), rot, zpad], axis=2).reshape(k, B_HEADS * LANES)
    return plain.astype(BF16), rotw.astype(BF16)


def _rope_tables(positions):
    half = QK_ROPE // 2
    freqs = ROPE_THETA ** (-2.0 * jnp.arange(half, dtype=F32) / QK_ROPE)
    ang = positions.astype(F32).reshape(-1)[:, None] * freqs
    cos, sin = jnp.cos(ang), jnp.sin(ang)
    n = ang.shape[0]
    ones = jnp.ones((n, QK_NOPE), F32)
    z32 = jnp.zeros((n, LANES - QK_NOPE - QK_ROPE), F32)
    cs = jnp.concatenate([ones, cos, cos, z32], axis=1)
    sn = jnp.concatenate([jnp.zeros((n, QK_NOPE), F32), sin, sin, z32], axis=1)
    return cs, sn


def _tiles(seq):
    tm = 512 if seq % 512 == 0 else 256
    tq_dsa = 512 if seq % 512 == 0 else 256
    tq_mla = 512 if seq % 512 == 0 else 256
    return tm, tq_dsa, tq_mla


def kernel(x, c, positions, w_ada, b_ada, g_ffn1, w_ffn1_in, w_ffn1_down, g_mix, w_in,
           g_cq, w_uq, g_ckv, w_uk, w_uv, rel_bias, w_o_a, w_o_b, w_out,
           g_ffn2, w_ffn2_in, w_ffn2_down, g_final):
    batch, seq, d = x.shape
    depth = w_ada.shape[0]
    n = batch * seq
    topk = min(TOPK_MAX, seq // 4)
    tm, tq_dsa, tq_mla = _tiles(seq)
    assert seq % tq_dsa == 0 and seq % tm == 0 and d % LANES == 0 and seq <= ZERO_BAND
    assert topk <= GROUPS and tq_dsa % GROUPS == 0
    assert depth == 1, "the tail kernel fuses the final norm into the single layer"

    xf = x.reshape(n, d)
    c_pad = jnp.pad(c, ((0, 8 - batch % 8 if batch % 8 else 0), (0, 0)))
    cs, sn = _rope_tables(positions)
    pos = positions.astype(I32)
    posq = pos.reshape(n // tq_dsa, 1, tq_dsa)
    posk = pos.reshape(batch, 1, seq)
    ptile = pos.reshape(n // tq_dsa, tq_dsa)
    pmin, pmax = ptile.min(axis=1), ptile.max(axis=1)
    tab = jnp.pad(rel_bias.T.astype(F32) * LOG2E, ((0, 0), (0, LANES - NUM_BUCKETS)))
    far = rel_bias[NUM_BUCKETS // 2 - 1].astype(F32) * LOG2E

    l = 0
    mod = _ada(c_pad, w_ada[l], b_ada[l])[:batch]
    mod3 = mod.reshape(batch, N_ADA, d)
    x1 = _ffn1(xf, mod3, g_ffn1[l].reshape(1, d), w_ffn1_in[l].astype(BF16),
               w_ffn1_down[l].astype(BF16), seq, tm)
    wq, wqr = _layout_w_uq(w_uq[l])
    qa, qi, sm, va, wi, qb, kb, vb, gates = _proj(
        x1, mod3, g_mix[l].reshape(1, d), _layout_w_in(w_in[l], d),
        g_cq[l].reshape(1, Q_LORA), wq, wqr, g_ckv[l].reshape(1, KV_LORA),
        _pad_heads(w_uk[l], B_HEADS, QK_NOPE).astype(BF16),
        _pad_heads(w_uv[l], B_HEADS, V_DIM).astype(BF16),
        cs, sn, seq, tm)
    vat = va.T
    vbt = vb.reshape(batch, seq, B_HEADS * LANES).transpose(0, 2, 1).reshape(-1, seq)
    oa = _dsa(qa, qi, wi, sm, vat, posq, posk, tab, far, pmin, pmax, batch, seq, tq_dsa, topk)
    ob = _mla(qb, kb, vbt, batch, seq, tq_mla)
    out = _tail(x1, oa, ob, gates, mod3,
                _pad_head_rows(w_o_a[l], A_HEADS, A_HEAD_DIM).astype(BF16),
                _pad_head_rows(w_o_b[l], B_HEADS, V_DIM).astype(BF16),
                w_out[l].astype(BF16), g_ffn2[l].reshape(1, d),
                w_ffn2_in[l].astype(BF16), w_ffn2_down[l].astype(BF16),
                g_final.reshape(1, d), seq, tm)
    return out.reshape(batch, seq, d)
```

```python
import functools
import math

import jax
import jax.numpy as jnp
from jax import lax
from jax.experimental import pallas as pl
from jax.experimental.pallas import tpu as pltpu

F32 = jnp.float32
BF16 = jnp.bfloat16
I32 = jnp.int32

CHUNK = 64
EPS = 1e-6
A_HEADS = 8
A_HEAD_DIM = 64
IDX_HEADS = 8
IDX_DIM = 64
TOPK_MAX = 256
B_HEADS = 8
QK_NOPE = 64
QK_ROPE = 32
V_DIM = 64
Q_LORA = 384
KV_LORA = 256
ROPE_THETA = 10000.0
NUM_BUCKETS = 32
MAX_DISTANCE = 128
N_ADA = 9

LANES = 128
SUBLANES = 8
VMEM_LIMIT = 56 * 1024 * 1024
INT_MIN = -2 ** 31
INT_MAX = 2 ** 31 - 1
NEG_BIG = -1e30
LOG2E = math.log2(math.e)
BF16_ROWS = 16
PV_ROWS = -(-(V_DIM + 1) // BF16_ROWS) * BF16_ROWS

C_QA = 0
C_QI = 512
C_SM = 1024
C_WI = 1664
C_CQ = 2048
C_CKV = C_CQ + Q_LORA
C_GT = C_CKV + KV_LORA


def _dot(a, b):
    return jnp.dot(a, b, preferred_element_type=F32)


def _dot_nt(a, b):
    return lax.dot_general(a, b, (((1,), (1,)), ((), ())), preferred_element_type=F32)


def _dot_tn(a, b):
    return lax.dot_general(a, b, (((0,), (0,)), ((), ())), preferred_element_type=F32)


def _two_phase(n_groups, phase1, phase2):
    phase1(0)
    for g in range(n_groups):
        if g + 1 < n_groups:
            phase1(g + 1)
        phase2(g)


def _rms(x, g):
    return x * lax.rsqrt(jnp.mean(x * x, axis=-1, keepdims=True) + EPS) * g


def _const_spec(shape):
    nd = len(shape)
    return pl.BlockSpec(shape, lambda *_: (0,) * nd, pipeline_mode=pl.Buffered(1))


def _ada_kernel(c_ref, w_ref, b_ref, o_ref):
    c = c_ref[...]
    s = c * jax.nn.sigmoid(c)
    o_ref[...] = _dot(s.astype(BF16), w_ref[...].astype(BF16)) + b_ref[...]


def _ada(c_pad, w_ada, b_ada):
    d = c_pad.shape[1]
    n_out = w_ada.shape[1]
    return pl.pallas_call(
        _ada_kernel,
        grid=(n_out // d,),
        in_specs=[pl.BlockSpec(c_pad.shape, lambda j: (0, 0)),
                  pl.BlockSpec((d, d), lambda j: (0, j)),
                  pl.BlockSpec((1, d), lambda j: (0, j))],
        out_specs=pl.BlockSpec(c_pad.shape, lambda j: (0, j)),
        out_shape=jax.ShapeDtypeStruct((c_pad.shape[0], n_out), F32),
        compiler_params=pltpu.CompilerParams(vmem_limit_bytes=VMEM_LIMIT),
        name="ada",
    )(c_pad, w_ada, b_ada.reshape(1, n_out))


def _swiglu_tile(hb, win_ref, wdn_ref, d_ff, chunk):
    acc = None
    for j in range(d_ff // chunk):
        g = _dot(hb, win_ref[:, j * chunk:(j + 1) * chunk])
        u = _dot(hb, win_ref[:, d_ff + j * chunk:d_ff + (j + 1) * chunk])
        a = (g * jax.nn.sigmoid(g) * u).astype(BF16)
        part = _dot(a, wdn_ref[j * chunk:(j + 1) * chunk, :])
        acc = part if acc is None else acc + part
    return acc


def _ffn_chunk(d_ff):
    for c in (512, 256, 128):
        if d_ff % c == 0:
            return c
    raise ValueError("d_ff must be a multiple of 128")


def _ffn1_kernel(x_ref, mod_ref, g_ref, win_ref, wdn_ref, o_ref, *, d_ff, chunk):
    x = x_ref[...]
    sh, sc, gt = mod_ref[0, 0:1, :], mod_ref[0, 1:2, :], mod_ref[0, 2:3, :]
    h = _rms(x, g_ref[...]) * (1.0 + sc) + sh
    y = _swiglu_tile(h.astype(BF16), win_ref, wdn_ref, d_ff, chunk)
    o_ref[...] = x + (0.5 * gt) * y


def _ffn1(x2, mod3, g, w_in, w_down, seq, tm):
    n, d = x2.shape
    d_ff = w_down.shape[0]
    tiles_per_batch = seq // tm
    kern = functools.partial(_ffn1_kernel, d_ff=d_ff, chunk=_ffn_chunk(d_ff))
    return pl.pallas_call(
        kern,
        grid=(n // tm,),
        in_specs=[pl.BlockSpec((tm, d), lambda i: (i, 0)),
                  pl.BlockSpec((1, N_ADA, d), lambda i: (i // tiles_per_batch, 0, 0)),
                  _const_spec((1, d)),
                  _const_spec(w_in.shape),
                  _const_spec(w_down.shape)],
        out_specs=pl.BlockSpec((tm, d), lambda i: (i, 0)),
        out_shape=jax.ShapeDtypeStruct((n, d), F32),
        compiler_params=pltpu.CompilerParams(vmem_limit_bytes=VMEM_LIMIT),
        name="ffn1",
    )(x2, mod3, g, w_in, w_down)


def _proj_kernel(x_ref, mod_ref, g_ref, w_ref, gcq_ref, wq_ref, wqr_ref, gckv_ref,
                 wk_ref, wv_ref, cs_ref, sn_ref,
                 qa_ref, qi_ref, sm_ref, va_ref, wi_ref, qb_ref, kb_ref, vb_ref, gt_ref, *, d):
    x = x_ref[...]
    sh, sc = mod_ref[0, 3:4, :], mod_ref[0, 4:5, :]
    hb = (_rms(x, g_ref[...]) * (1.0 + sc) + sh).astype(BF16)
    cs = cs_ref[...]
    sn = sn_ref[...]
    lane = lax.broadcasted_iota(I32, (1, LANES), 1)
    one_at_v = (lane == V_DIM).astype(F32)

    qa_ref[...] = _dot(hb, w_ref[:, C_QA:C_QA + 512]).astype(BF16)
    qi_ref[...] = _dot(hb, w_ref[:, C_QI:C_QI + 512]).astype(BF16)

    sm = _dot(hb, w_ref[:, C_SM:C_SM + 640])
    sm_ref[:, 0:256] = sm[:, 0:256].astype(BF16)
    sm_ref[:, 256:512] = sm[:, 384:640].astype(BF16)
    va_ref[...] = (sm[:, 256:384] + one_at_v).astype(BF16)

    misc = _dot(hb, w_ref[:, C_WI:C_WI + 384])
    wi_ref[...] = misc[:, 0:128]
    kpe = misc[:, 128:256] * cs + misc[:, 256:384] * sn

    lat = _dot(hb, w_ref[:, C_CQ:C_GT])
    cq = _rms(lat[:, 0:Q_LORA], gcq_ref[...]).astype(BF16)
    ckv = _rms(lat[:, Q_LORA:Q_LORA + KV_LORA], gckv_ref[...]).astype(BF16)
    qn = _dot(cq, wq_ref[...])
    qr = _dot(cq, wqr_ref[...])
    kn = _dot(ckv, wk_ref[...])
    vv = _dot(ckv, wv_ref[...])
    for h in range(B_HEADS):
        sl = slice(h * LANES, (h + 1) * LANES)
        qb_ref[:, sl] = (qn[:, sl] * cs + qr[:, sl] * sn).astype(BF16)
        kb_ref[:, sl] = (kn[:, sl] + kpe).astype(BF16)
        vb_ref[:, sl] = (vv[:, sl] + one_at_v).astype(BF16)

    gl = _dot(hb, w_ref[:, C_GT:C_GT + 2 * d])
    gt_ref[...] = jax.nn.sigmoid(gl).astype(BF16)


def _proj(x1, mod3, g, w_all, gcq, wq, wqr, gckv, wk, wv, cs, sn, seq, tm):
    n, d = x1.shape
    tiles_per_batch = seq // tm
    row = lambda w: pl.BlockSpec((tm, w), lambda i: (i, 0))
    out_widths = (512, 512, 512, 128, 128, 1024, 1024, 1024, 2 * d)
    out_dtypes = (BF16, BF16, BF16, BF16, F32, BF16, BF16, BF16, BF16)
    return pl.pallas_call(
        functools.partial(_proj_kernel, d=d),
        grid=(n // tm,),
        in_specs=[row(d),
                  pl.BlockSpec((1, N_ADA, d), lambda i: (i // tiles_per_batch, 0, 0)),
                  _const_spec((1, d)), _const_spec(w_all.shape),
                  _const_spec(gcq.shape), _const_spec(wq.shape), _const_spec(wqr.shape),
                  _const_spec(gckv.shape), _const_spec(wk.shape), _const_spec(wv.shape),
                  row(LANES), row(LANES)],
        out_specs=[row(w) for w in out_widths],
        out_shape=[jax.ShapeDtypeStruct((n, w), dt) for w, dt in zip(out_widths, out_dtypes)],
        compiler_params=pltpu.CompilerParams(vmem_limit_bytes=VMEM_LIMIT),
        name="proj",
    )(x1, mod3, g, w_all, gcq, wq, wqr, gckv, wk, wv, cs, sn)


DSA_HEAD_GROUP = 2
ZERO_BAND = 1 << 20
GROUPS = 256
COUNT_UNKNOWN = 1 << 30
FINISH_STEPS = 2


def _rowsum8(x):
    v = x.reshape(x.shape[0] // SUBLANES, SUBLANES, x.shape[1])
    while v.shape[0] > 1:
        half = v.shape[0] // 2
        v = v[:half] + v[half:]
    return v[0]


def _t5_bias_tiles(posk, posq, tab_ref, heads):
    half = NUM_BUCKETS // 2
    max_exact = half // 2
    rel = posk - posq
    n = jnp.abs(rel)
    nf = jnp.maximum(n, 1).astype(F32)
    large = max_exact + (jnp.log(nf / max_exact) / math.log(MAX_DISTANCE / max_exact)
                         * (half - max_exact)).astype(I32)
    large = jnp.minimum(large, half - 1)
    bucket = jnp.where(rel > 0, half, 0) + jnp.where(n < max_exact, n, large)
    tk, tq = bucket.shape
    out = []
    for h in range(heads):
        tab = jnp.broadcast_to(tab_ref[h:h + 1, :], (tk, LANES))
        cols = [jnp.take_along_axis(tab, bucket[:, c * LANES:(c + 1) * LANES], axis=1,
                                    mode="promise_in_bounds")
                for c in range(tq // LANES)]
        out.append(jnp.concatenate(cols, axis=1))
    return out


def _dsa_kernel(pmin_ref, pmax_ref,
                qa_ref, qi_ref, wi_ref, sm_ref, vat_ref, posq_ref, posk_ref, tab_ref, far_ref,
                o_ref,
                keys_ref, gmax_ref, m_ref, acc_ref, s_ref, *, tq, topk, nq):
    b = pl.program_id(0)
    i = pl.program_id(1)
    tk = tq
    n_tiles = i + 1

    w_t = wi_ref[...].T[0:IDX_HEADS, :]
    key_l = lax.broadcasted_iota(I32, (tk, tq), 0)
    q_l = lax.broadcasted_iota(I32, (tk, tq), 1)
    diag_ok = key_l < (q_l // CHUNK + 1) * CHUNK
    not_key_l = ~key_l

    def score_tile(j, diag):
        r0 = pl.multiple_of(j * tk, tk)
        ki_lo = sm_ref[pl.ds(r0, tk), 256:384]
        ki_hi = sm_ref[pl.ds(r0, tk), 384:512]
        score = None
        for p in range(IDX_HEADS // 2):
            qp = qi_ref[:, p * LANES:(p + 1) * LANES]
            for h, kk in ((2 * p, ki_lo), (2 * p + 1, ki_hi)):
                t = w_t[h:h + 1, :] * jnp.maximum(_dot_nt(kk, qp), 0.0)
                score = t if score is None else score + t
        score = score + 0.0
        bits = lax.bitcast_convert_type(score, I32)
        key = jnp.where(bits < 0, (bits ^ 0x7FFFFFFF) - ZERO_BAND, bits)
        key = jnp.where(score == 0.0, not_key_l - j * tk, key)
        if diag:
            key = jnp.where(diag_ok, key, INT_MIN)
        keys_ref[pl.ds(r0, tk), :] = key
        gmax_ref[...] = jnp.maximum(gmax_ref[...],
                                    jnp.max(key.reshape(tk // GROUPS, GROUPS, tq), axis=0))

    gmax_ref[...] = jnp.full(gmax_ref.shape, INT_MIN, I32)
    lax.fori_loop(0, i, lambda j, c: (score_tile(j, False), c)[1], 0)
    score_tile(i, True)

    def count_tiles(pred):
        def body(j, acc):
            r0 = pl.multiple_of(j * tk, tk)
            return acc + _rowsum8(jnp.where(pred(keys_ref[pl.ds(r0, tk), :], j), 1, 0))
        acc = lax.fori_loop(0, n_tiles, body, jnp.zeros((SUBLANES, tq), I32))
        return jnp.sum(acc, axis=0, keepdims=True)

    q_row = lax.broadcasted_iota(I32, (1, tq), 1)
    n_adm = i * tq + (q_row // CHUNK + 1) * CHUNK

    gmax = gmax_ref[...]
    top = jnp.max(gmax, axis=0, keepdims=True)
    low = jnp.min(gmax, axis=0, keepdims=True)
    cnt_pos = count_tiles(lambda k_, j: k_ >= 1)
    cnt_zero = count_tiles(lambda k_, j: k_ >= -ZERO_BAND)
    few = n_adm <= topk
    in_pos = cnt_pos >= topk
    in_zero = jnp.logical_not(in_pos) & (cnt_zero >= topk)
    base_lo = jnp.where(in_pos, 1, jnp.where(in_zero, -ZERO_BAND, INT_MIN + 1))
    base_cnt = jnp.where(in_pos, cnt_pos, jnp.where(in_zero, cnt_zero, n_adm))
    hi0 = jnp.where(in_pos, top + 1, jnp.where(in_zero, 1, -ZERO_BAND))
    tighter = low > base_lo
    lo0 = jnp.where(few, INT_MIN + 1, jnp.where(tighter, low, base_lo))
    cnt0 = jnp.where(few, topk, jnp.where(tighter, COUNT_UNKNOWN, base_cnt))
    hi0 = jnp.where(few, INT_MIN + 2, hi0)

    def any_row(pred):
        return jnp.max(jnp.where(pred, 1, 0))

    def bisect(lo, hi, cnt_lo, slack):
        def pending(lo, hi, cnt_lo):
            return any_row((cnt_lo - topk > slack) & (hi - lo > 1))

        def step(st):
            it, lo, hi, cnt_lo, _ = st
            mid = lo + ((hi - lo) >> 1)
            cnt = count_tiles(lambda k_, j: k_ >= mid)
            take = cnt >= topk
            lo = jnp.where(take, mid, lo)
            hi = jnp.where(take, hi, mid)
            cnt_lo = jnp.where(take, cnt, cnt_lo)
            return it + 1, lo, hi, cnt_lo, pending(lo, hi, cnt_lo)

        return lax.while_loop(lambda st: (st[0] < 34) & (st[4] > 0), step,
                              (jnp.int32(0), lo, hi, cnt_lo, pending(lo, hi, cnt_lo)))

    _, lo1, hi1, cnt1, _ = bisect(lo0, hi0, cnt0, FINISH_STEPS)

    def surplus(lo, cnt_lo):
        return (cnt_lo > topk) & (cnt_lo - topk <= FINISH_STEPS) & (hi1 - lo > 1)

    def drop_step(st):
        it, lo, cnt_lo, _ = st
        def body(j, acc):
            r0 = pl.multiple_of(j * tk, tk)
            k_ = keys_ref[pl.ds(r0, tk), :]
            cand = jnp.where(k_ >= lo, k_, INT_MAX)
            v = cand.reshape(tk // SUBLANES, SUBLANES, tq)
            while v.shape[0] > 1:
                half = v.shape[0] // 2
                v = jnp.minimum(v[:half], v[half:])
            return jnp.minimum(acc, v[0])
        acc = lax.fori_loop(0, n_tiles, body, jnp.full((SUBLANES, tq), INT_MAX, I32))
        smallest = jnp.min(acc, axis=0, keepdims=True)
        go = surplus(lo, cnt_lo)
        lo = jnp.where(go, smallest + 1, lo)
        cnt_lo = jnp.where(go, cnt_lo - 1, cnt_lo)
        return it + 1, lo, cnt_lo, any_row(surplus(lo, cnt_lo))

    _, lo2, _, _ = lax.while_loop(lambda st: (st[0] < FINISH_STEPS) & (st[3] > 0), drop_step,
                                  (jnp.int32(0), lo1, cnt1, any_row(surplus(lo1, cnt1))))

    cnt2 = count_tiles(lambda k_, j: k_ >= lo2)
    redo = (cnt2 != topk) & jnp.logical_not(few)
    its, tau, _, _, _ = bisect(jnp.where(redo, lo1, lo2), jnp.where(redo, hi1, lo2 + 1),
                               jnp.where(redo, cnt1, topk), 0)
    cnt_ge = lax.cond(its > 0, lambda: count_tiles(lambda k_, j: k_ >= tau), lambda: cnt2)
    over = (cnt_ge > topk) & jnp.logical_not(few)
    tau_sel = tau

    @pl.when(jnp.max(jnp.where(over, 1, 0)) > 0)
    def _ties():
        cnt_gt = count_tiles(lambda k_, j: k_ > tau)
        room = topk - cnt_gt

        def idx_step(p, jmax):
            cand = jmax | jnp.left_shift(jnp.int32(1), 14 - p)
            cnt = count_tiles(lambda k_, j: (k_ == tau) & (key_l + j * tk < cand))
            return jnp.where(cnt <= room, cand, jmax)

        jmax = lax.fori_loop(0, 15, idx_step, jnp.zeros((1, tq), I32))

        def demote(j, carry):
            r0 = pl.multiple_of(j * tk, tk)
            k_ = keys_ref[pl.ds(r0, tk), :]
            drop = (k_ == tau) & (key_l + j * tk >= jmax) & over
            keys_ref[pl.ds(r0, tk), :] = jnp.where(drop, k_ - 1, k_)
            return carry

        lax.fori_loop(0, n_tiles, demote, 0)

    m_ref[...] = jnp.full(m_ref.shape, NEG_BIG, F32)
    acc_ref[...] = jnp.zeros(acc_ref.shape, F32)
    posq = posq_ref[0]

    def attn_tile(j, near):
        r0 = pl.multiple_of(j * tk, tk)
        ka_lo = sm_ref[pl.ds(r0, tk), 0:128]
        ka_hi = sm_ref[pl.ds(r0, tk), 128:256]
        va_t = vat_ref[0:PV_ROWS, pl.ds(r0, tk)]
        maskb = jnp.where(keys_ref[pl.ds(r0, tk), :] >= tau_sel, 0.0, NEG_BIG)
        if near:
            posk_row = lax.bitcast_convert_type(posk_ref[0, :, pl.ds(r0, tk)], F32)
            posk_col = lax.bitcast_convert_type(
                jnp.broadcast_to(posk_row, (LANES, tk)).T, I32)
            bias = _t5_bias_tiles(jnp.tile(posk_col, (1, tq // LANES)), posq, tab_ref, A_HEADS)
        m_prev = m_ref[...]
        m_next = [None] * A_HEADS
        slot = lambda h: (h // DSA_HEAD_GROUP % 2) * DSA_HEAD_GROUP + h % DSA_HEAD_GROUP

        def logits(g):
            for h in range(g * DSA_HEAD_GROUP, (g + 1) * DSA_HEAD_GROUP):
                s = _dot_nt(ka_hi if h % 2 else ka_lo,
                            qa_ref[:, (h // 2) * LANES:(h // 2 + 1) * LANES])
                s = s + ((bias[h] + maskb) if near else maskb)
                s_ref[slot(h)] = s
                s_max = jnp.max(s, axis=0, keepdims=True)
                m_next[h] = jnp.maximum(m_prev[h:h + 1, :],
                                        s_max if near else s_max + far_ref[h])

        def weighted_values(g):
            for h in range(g * DSA_HEAD_GROUP, (g + 1) * DSA_HEAD_GROUP):
                alpha = jnp.exp2(m_prev[h:h + 1, :] - m_next[h])
                shift = m_next[h] if near else m_next[h] - far_ref[h]
                pr = jnp.exp2(s_ref[slot(h)] - shift).astype(BF16)
                acc_ref[h, 0:PV_ROWS, :] = alpha * acc_ref[h, 0:PV_ROWS, :] + _dot(va_t, pr)

        _two_phase(A_HEADS // DSA_HEAD_GROUP, logits, weighted_values)
        m_ref[...] = jnp.concatenate(m_next, axis=0)

    def attn_body(j, carry):
        is_far = pmax_ref[b * nq + j] - pmin_ref[b * nq + i] <= -MAX_DISTANCE
        lax.cond(is_far, lambda: attn_tile(j, False), lambda: attn_tile(j, True))
        return carry

    lax.fori_loop(0, n_tiles, attn_body, 0)

    for h in range(A_HEADS):
        a = acc_ref[h]
        o_ref[:, h * LANES:(h + 1) * LANES] = (a / a[V_DIM:V_DIM + 1, :]).T.astype(BF16)


def _dsa(qa, qi, wi, sm, vat, posq, posk, tab, far, pmin, pmax, batch, seq, tq, topk):
    n = qa.shape[0]
    nq = seq // tq
    row = lambda w: pl.BlockSpec((tq, w), lambda b, i, *_: (b * nq + i, 0))
    grid_spec = pltpu.PrefetchScalarGridSpec(
        num_scalar_prefetch=2,
        grid=(batch, nq),
        in_specs=[row(512), row(512), row(LANES),
                  pl.BlockSpec((seq, 512), lambda b, i, *_: (b, 0),
                               pipeline_mode=pl.Buffered(1)),
                  pl.BlockSpec((LANES, seq), lambda b, i, *_: (0, b)),
                  pl.BlockSpec((1, 1, tq), lambda b, i, *_: (b * nq + i, 0, 0)),
                  pl.BlockSpec((1, 1, seq), lambda b, i, *_: (b, 0, 0)),
                  pl.BlockSpec((A_HEADS, LANES), lambda b, i, *_: (0, 0)),
                  pl.BlockSpec(memory_space=pltpu.SMEM)],
        out_specs=row(A_HEADS * LANES),
        scratch_shapes=[pltpu.VMEM((seq, tq), I32),
                        pltpu.VMEM((GROUPS, tq), I32),
                        pltpu.VMEM((A_HEADS, tq), F32),
                        pltpu.VMEM((A_HEADS, LANES, tq), F32),
                        pltpu.VMEM((2 * DSA_HEAD_GROUP, tq, tq), F32)],
    )
    return pl.pallas_call(
        functools.partial(_dsa_kernel, tq=tq, topk=topk, nq=nq),
        grid_spec=grid_spec,
        out_shape=jax.ShapeDtypeStruct((n, A_HEADS * LANES), BF16),
        compiler_params=pltpu.CompilerParams(vmem_limit_bytes=VMEM_LIMIT),
        name="dsa",
    )(pmin, pmax, qa, qi, wi, sm, vat, posq, posk, tab, far)


MLA_HEAD_GROUP = 8


def _mla_kernel(q_ref, k_ref, vt_ref, o_ref, acc_ref, s_ref, *, tq):
    i = pl.program_id(2)
    tk = tq
    hg = MLA_HEAD_GROUP
    acc_ref[...] = jnp.zeros(acc_ref.shape, F32)

    def tile(j, ms, diag):
        r0 = pl.multiple_of(j * tk, tk)
        if diag:
            key_l = lax.broadcasted_iota(I32, (tk, tq), 0)
            q_l = lax.broadcasted_iota(I32, (tk, tq), 1)
            ok = key_l < (q_l // CHUNK + 1) * CHUNK
        head = lambda h: slice(h * LANES, (h + 1) * LANES)
        out = []
        for h in range(hg):
            s = _dot_nt(k_ref[pl.ds(r0, tk), head(h)], q_ref[:, head(h)])
            if diag:
                s = jnp.where(ok, s, NEG_BIG)
            s_ref[h] = s
            out.append(jnp.maximum(ms[h], jnp.max(s, axis=0, keepdims=True)))
        for h in range(hg):
            alpha = jnp.exp2(ms[h] - out[h])
            pr = jnp.exp2(s_ref[h] - out[h]).astype(BF16)
            v_t = vt_ref[h * LANES:h * LANES + PV_ROWS, pl.ds(r0, tk)]
            acc_ref[h, 0:PV_ROWS, :] = alpha * acc_ref[h, 0:PV_ROWS, :] + _dot(v_t, pr)
        return tuple(out)

    m0 = tuple(jnp.full((1, tq), NEG_BIG, F32) for _ in range(hg))
    ms = lax.fori_loop(0, i, lambda j, m_: tile(j, m_, False), m0)
    tile(i, ms, True)
    for h in range(hg):
        a = acc_ref[h]
        o_ref[:, h * LANES:(h + 1) * LANES] = (a / a[V_DIM:V_DIM + 1, :]).T.astype(BF16)


def _mla(qb, kb, vbt, batch, seq, tq):
    n = qb.shape[0]
    nq = seq // tq
    hg = MLA_HEAD_GROUP
    ng = B_HEADS // hg
    w = hg * LANES
    return pl.pallas_call(
        functools.partial(_mla_kernel, tq=tq),
        grid=(batch, ng, nq),
        in_specs=[pl.BlockSpec((tq, w), lambda b, g, i: (b * nq + i, g)),
                  pl.BlockSpec((seq, w), lambda b, g, i: (b, g), pipeline_mode=pl.Buffered(1)),
                  pl.BlockSpec((w, seq), lambda b, g, i: (b * ng + g, 0),
                               pipeline_mode=pl.Buffered(1))],
        out_specs=pl.BlockSpec((tq, w), lambda b, g, i: (b * nq + i, g)),
        out_shape=jax.ShapeDtypeStruct((n, B_HEADS * LANES), BF16),
        scratch_shapes=[pltpu.VMEM((hg, LANES, tq), F32), pltpu.VMEM((hg, tq, tq), F32)],
        compiler_params=pltpu.CompilerParams(vmem_limit_bytes=VMEM_LIMIT),
        name="mla",
    )(qb, kb, vbt)


def _tail_kernel(x_ref, oa_ref, ob_ref, gt_ref, mod_ref, woa_ref, wob_ref, wout_ref,
                 g2_ref, win_ref, wdn_ref, gf_ref, o_ref, *, d, d_ff, chunk):
    x = x_ref[...]
    gt2 = mod_ref[0, 5:6, :]
    sh3, sc3, gt3 = mod_ref[0, 6:7, :], mod_ref[0, 7:8, :], mod_ref[0, 8:9, :]
    ya = _dot(oa_ref[...], woa_ref[...])
    yb = _dot(ob_ref[...], wob_ref[...])
    y = gt_ref[:, 0:d].astype(F32) * ya + gt_ref[:, d:2 * d].astype(F32) * yb
    x = x + gt2 * _dot(y.astype(BF16), wout_ref[...])
    h = _rms(x, g2_ref[...]) * (1.0 + sc3) + sh3
    f = _swiglu_tile(h.astype(BF16), win_ref, wdn_ref, d_ff, chunk)
    x = x + (0.5 * gt3) * f
    o_ref[...] = _rms(x, gf_ref[...])


def _tail(x1, oa, ob, gates, mod3, woa, wob, wout, g2, w_in, w_down, gf, seq, tm):
    n, d = x1.shape
    d_ff = w_down.shape[0]
    tiles_per_batch = seq // tm
    row = lambda w: pl.BlockSpec((tm, w), lambda i: (i, 0))
    kern = functools.partial(_tail_kernel, d=d, d_ff=d_ff, chunk=_ffn_chunk(d_ff))
    return pl.pallas_call(
        kern,
        grid=(n // tm,),
        in_specs=[row(d), row(oa.shape[1]), row(ob.shape[1]), row(2 * d),
                  pl.BlockSpec((1, N_ADA, d), lambda i: (i // tiles_per_batch, 0, 0)),
                  _const_spec(woa.shape), _const_spec(wob.shape), _const_spec(wout.shape),
                  _const_spec((1, d)), _const_spec(w_in.shape), _const_spec(w_down.shape),
                  _const_spec((1, d))],
        out_specs=row(d),
        out_shape=jax.ShapeDtypeStruct((n, d), F32),
        compiler_params=pltpu.CompilerParams(vmem_limit_bytes=VMEM_LIMIT),
        name="tail",
    )(x1, oa, ob, gates, mod3, woa, wob, wout, g2, w_in, w_down, gf)


def _pad_heads(w, heads, width):
    k = w.shape[0]
    w3 = w.reshape(k, heads, width)
    return jnp.pad(w3, ((0, 0), (0, 0), (0, LANES - width))).reshape(k, heads * LANES)


def _pad_head_rows(w, heads, width):
    dd = w.shape[1]
    w3 = w.reshape(heads, width, dd)
    return jnp.pad(w3, ((0, 0), (0, LANES - width), (0, 0))).reshape(heads * LANES, dd)


def _layout_w_in(w_in, d):
    z = lambda c: jnp.zeros((d, c), w_in.dtype)
    o = 0
    qa = w_in[:, o:o + 512] * (A_HEAD_DIM ** -0.5 * LOG2E); o += 512
    ka = w_in[:, o:o + 64]; o += 64
    va = w_in[:, o:o + 64]; o += 64
    qi = w_in[:, o:o + 512]; o += 512
    ki = w_in[:, o:o + 64]; o += 64
    wi = w_in[:, o:o + IDX_HEADS]; o += IDX_HEADS
    cq = w_in[:, o:o + Q_LORA]; o += Q_LORA
    ckv = w_in[:, o:o + KV_LORA]; o += KV_LORA
    kr = w_in[:, o:o + QK_ROPE]; o += QK_ROPE
    gates = w_in[:, o:o + 2 * d]
    hr = QK_ROPE // 2
    kt = jnp.concatenate([-kr[:, hr:], kr[:, :hr]], axis=1)
    cols = [qa, qi,
            ka, z(64), z(64), ka, va, z(64), ki, z(64), z(64), ki,
            wi, z(LANES - IDX_HEADS),
            z(64), kr, z(32), z(64), kt, z(32),
            cq, ckv, gates]
    return jnp.concatenate(cols, axis=1).astype(BF16)


def _layout_w_uq(w_uq):
    scale = (QK_NOPE + QK_ROPE) ** -0.5 * LOG2E
    k = w_uq.shape[0]
    w3 = w_uq.reshape(k, B_HEADS, QK_NOPE + QK_ROPE) * scale
    nope, pe = w3[:, :, :QK_NOPE], w3[:, :, QK_NOPE:]
    hr = QK_ROPE // 2
    rot = jnp.concatenate([-pe[:, :, hr:], pe[:, :, :hr]], axis=2)
    zpad = jnp.zeros((k, B_HEADS, LANES - QK_NOPE - QK_ROPE), w_uq.dtype)
    plain = jnp.concatenate([nope, pe, zpad], axis=2).reshape(k, B_HEADS * LANES)
    rotw = jnp.concatenate([jnp.zeros_like(nope), rot, zpad], axis=2).reshape(k, B_HEADS * LANES)
    return plain.astype(BF16), rotw.astype(BF16)


def _rope_tables(positions):
    half = QK_ROPE // 2
    freqs = ROPE_THETA ** (-2.0 * jnp.arange(half, dtype=F32) / QK_ROPE)
    ang = positions.astype(F32).reshape(-1)[:, None] * freqs
    cos, sin = jnp.cos(ang), jnp.sin(ang)
    n = ang.shape[0]
    ones = jnp.ones((n, QK_NOPE), F32)
    z32 = jnp.zeros((n, LANES - QK_NOPE - QK_ROPE), F32)
    cs = jnp.concatenate([ones, cos, cos, z32], axis=1)
    sn = jnp.concatenate([jnp.zeros((n, QK_NOPE), F32), sin, sin, z32], axis=1)
    return cs, sn


def _tiles(seq):
    tm = 512 if seq % 512 == 0 else 256
    tq_dsa = 512 if seq % 512 == 0 else 256
    tq_mla = 512 if seq % 512 == 0 else 256
    return tm, tq_dsa, tq_mla


def kernel(x, c, positions, w_ada, b_ada, g_ffn1, w_ffn1_in, w_ffn1_down, g_mix, w_in,
           g_cq, w_uq, g_ckv, w_uk, w_uv, rel_bias, w_o_a, w_o_b, w_out,
           g_ffn2, w_ffn2_in, w_ffn2_down, g_final):
    batch, seq, d = x.shape
    depth = w_ada.shape[0]
    n = batch * seq
    topk = min(TOPK_MAX, seq // 4)
    tm, tq_dsa, tq_mla = _tiles(seq)
    assert seq % tq_dsa == 0 and seq % tm == 0 and d % LANES == 0 and seq <= ZERO_BAND
    assert topk <= GROUPS and tq_dsa % GROUPS == 0
    assert depth == 1, "the tail kernel fuses the final norm into the single layer"

    xf = x.reshape(n, d)
    c_pad = jnp.pad(c, ((0, 8 - batch % 8 if batch % 8 else 0), (0, 0)))
    cs, sn = _rope_tables(positions)
    pos = positions.astype(I32)
    posq = pos.reshape(n // tq_dsa, 1, tq_dsa)
    posk = pos.reshape(batch, 1, seq)
    ptile = pos.reshape(n // tq_dsa, tq_dsa)
    pmin, pmax = ptile.min(axis=1), ptile.max(axis=1)
    tab = jnp.pad(rel_bias.T.astype(F32) * LOG2E, ((0, 0), (0, LANES - NUM_BUCKETS)))
    far = rel_bias[NUM_BUCKETS // 2 - 1].astype(F32) * LOG2E

    l = 0
    mod = _ada(c_pad, w_ada[l], b_ada[l])[:batch]
    mod3 = mod.reshape(batch, N_ADA, d)
    x1 = _ffn1(xf, mod3, g_ffn1[l].reshape(1, d), w_ffn1_in[l].astype(BF16),
               w_ffn1_down[l].astype(BF16), seq, tm)
    wq, wqr = _layout_w_uq(w_uq[l])
    qa, qi, sm, va, wi, qb, kb, vb, gates = _proj(
        x1, mod3, g_mix[l].reshape(1, d), _layout_w_in(w_in[l], d),
        g_cq[l].reshape(1, Q_LORA), wq, wqr, g_ckv[l].reshape(1, KV_LORA),
        _pad_heads(w_uk[l], B_HEADS, QK_NOPE).astype(BF16),
        _pad_heads(w_uv[l], B_HEADS, V_DIM).astype(BF16),
        cs, sn, seq, tm)
    vat = va.T
    vbt = vb.reshape(batch, seq, B_HEADS * LANES).transpose(0, 2, 1).reshape(-1, seq)
    oa = _dsa(qa, qi, wi, sm, vat, posq, posk, tab, far, pmin, pmax, batch, seq, tq_dsa, topk)
    ob = _mla(qb, kb, vbt, batch, seq, tq_mla)
    out = _tail(x1, oa, ob, gates, mod3,
                _pad_head_rows(w_o_a[l], A_HEADS, A_HEAD_DIM).astype(BF16),
                _pad_head_rows(w_o_b[l], B_HEADS, V_DIM).astype(BF16),
                w_out[l].astype(BF16), g_ffn2[l].reshape(1, d),
                w_ffn2_in[l].astype(BF16), w_ffn2_down[l].astype(BF16),
                g_final.reshape(1, d), seq, tm)
    return out.reshape(batch, seq, d)
```

```python
import functools
import math

import jax
import jax.numpy as jnp
from jax import lax
from jax.experimental import pallas as pl
from jax.experimental.pallas import tpu as pltpu

F32 = jnp.float32
BF16 = jnp.bfloat16
I32 = jnp.int32

CHUNK = 64
EPS = 1e-6
A_HEADS = 8
A_HEAD_DIM = 64
IDX_HEADS = 8
IDX_DIM = 64
TOPK_MAX = 256
B_HEADS = 8
QK_NOPE = 64
QK_ROPE = 32
V_DIM = 64
Q_LORA = 384
KV_LORA = 256
ROPE_THETA = 10000.0
NUM_BUCKETS = 32
MAX_DISTANCE = 128
N_ADA = 9

LANES = 128
SUBLANES = 8
VMEM_LIMIT = 56 * 1024 * 1024
INT_MIN = -2 ** 31
INT_MAX = 2 ** 31 - 1
NEG_BIG = -1e30
LOG2E = math.log2(math.e)
BF16_ROWS = 16
PV_ROWS = -(-(V_DIM + 1) // BF16_ROWS) * BF16_ROWS

C_QA = 0
C_QI = 512
C_SM = 1024
C_WI = 1664
C_CQ = 2048
C_CKV = C_CQ + Q_LORA
C_GT = C_CKV + KV_LORA


def _dot(a, b):
    return jnp.dot(a, b, preferred_element_type=F32)


def _dot_nt(a, b):
    return lax.dot_general(a, b, (((1,), (1,)), ((), ())), preferred_element_type=F32)


def _dot_tn(a, b):
    return lax.dot_general(a, b, (((0,), (0,)), ((), ())), preferred_element_type=F32)


def _two_phase(n_groups, phase1, phase2):
    phase1(0)
    for g in range(n_groups):
        if g + 1 < n_groups:
            phase1(g + 1)
        phase2(g)


def _rms(x, g):
    return x * lax.rsqrt(jnp.mean(x * x, axis=-1, keepdims=True) + EPS) * g


def _const_spec(shape):
    nd = len(shape)
    return pl.BlockSpec(shape, lambda *_: (0,) * nd, pipeline_mode=pl.Buffered(1))


def _ada_kernel(c_ref, w_ref, b_ref, o_ref):
    c = c_ref[...]
    s = c * jax.nn.sigmoid(c)
    o_ref[...] = _dot(s.astype(BF16), w_ref[...].astype(BF16)) + b_ref[...]


def _ada(c_pad, w_ada, b_ada):
    d = c_pad.shape[1]
    n_out = w_ada.shape[1]
    return pl.pallas_call(
        _ada_kernel,
        grid=(n_out // d,),
        in_specs=[pl.BlockSpec(c_pad.shape, lambda j: (0, 0)),
                  pl.BlockSpec((d, d), lambda j: (0, j)),
                  pl.BlockSpec((1, d), lambda j: (0, j))],
        out_specs=pl.BlockSpec(c_pad.shape, lambda j: (0, j)),
        out_shape=jax.ShapeDtypeStruct((c_pad.shape[0], n_out), F32),
        compiler_params=pltpu.CompilerParams(vmem_limit_bytes=VMEM_LIMIT),
        name="ada",
    )(c_pad, w_ada, b_ada.reshape(1, n_out))


def _swiglu_tile(hb, win_ref, wdn_ref, d_ff, chunk):
    acc = None
    for j in range(d_ff // chunk):
        g = _dot(hb, win_ref[:, j * chunk:(j + 1) * chunk])
        u = _dot(hb, win_ref[:, d_ff + j * chunk:d_ff + (j + 1) * chunk])
        a = (g * jax.nn.sigmoid(g) * u).astype(BF16)
        part = _dot(a, wdn_ref[j * chunk:(j + 1) * chunk, :])
        acc = part if acc is None else acc + part
    return acc


def _ffn_chunk(d_ff):
    for c in (512, 256, 128):
        if d_ff % c == 0:
            return c
    raise ValueError("d_ff must be a multiple of 128")


def _ffn1_kernel(x_ref, mod_ref, g_ref, win_ref, wdn_ref, o_ref, *, d_ff, chunk):
    x = x_ref[...]
    sh, sc, gt = mod_ref[0, 0:1, :], mod_ref[0, 1:2, :], mod_ref[0, 2:3, :]
    h = _rms(x, g_ref[...]) * (1.0 + sc) + sh
    y = _swiglu_tile(h.astype(BF16), win_ref, wdn_ref, d_ff, chunk)
    o_ref[...] = x + (0.5 * gt) * y


def _ffn1(x2, mod3, g, w_in, w_down, seq, tm):
    n, d = x2.shape
    d_ff = w_down.shape[0]
    tiles_per_batch = seq // tm
    kern = functools.partial(_ffn1_kernel, d_ff=d_ff, chunk=_ffn_chunk(d_ff))
    return pl.pallas_call(
        kern,
        grid=(n // tm,),
        in_specs=[pl.BlockSpec((tm, d), lambda i: (i, 0)),
                  pl.BlockSpec((1, N_ADA, d), lambda i: (i // tiles_per_batch, 0, 0)),
                  _const_spec((1, d)),
                  _const_spec(w_in.shape),
                  _const_spec(w_down.shape)],
        out_specs=pl.BlockSpec((tm, d), lambda i: (i, 0)),
        out_shape=jax.ShapeDtypeStruct((n, d), F32),
        compiler_params=pltpu.CompilerParams(vmem_limit_bytes=VMEM_LIMIT),
        name="ffn1",
    )(x2, mod3, g, w_in, w_down)


def _proj_kernel(x_ref, mod_ref, g_ref, w_ref, gcq_ref, wq_ref, wqr_ref, gckv_ref,
                 wk_ref, wv_ref, cs_ref, sn_ref,
                 qa_ref, qi_ref, sm_ref, va_ref, wi_ref, qb_ref, kb_ref, vb_ref, gt_ref, *, d):
    x = x_ref[...]
    sh, sc = mod_ref[0, 3:4, :], mod_ref[0, 4:5, :]
    hb = (_rms(x, g_ref[...]) * (1.0 + sc) + sh).astype(BF16)
    cs = cs_ref[...]
    sn = sn_ref[...]
    lane = lax.broadcasted_iota(I32, (1, LANES), 1)
    one_at_v = (lane == V_DIM).astype(F32)

    qa_ref[...] = _dot(hb, w_ref[:, C_QA:C_QA + 512]).astype(BF16)
    qi_ref[...] = _dot(hb, w_ref[:, C_QI:C_QI + 512]).astype(BF16)

    sm = _dot(hb, w_ref[:, C_SM:C_SM + 640])
    sm_ref[:, 0:256] = sm[:, 0:256].astype(BF16)
    sm_ref[:, 256:512] = sm[:, 384:640].astype(BF16)
    va_ref[...] = (sm[:, 256:384] + one_at_v).astype(BF16)

    misc = _dot(hb, w_ref[:, C_WI:C_WI + 384])
    wi_ref[...] = misc[:, 0:128]
    kpe = misc[:, 128:256] * cs + misc[:, 256:384] * sn

    lat = _dot(hb, w_ref[:, C_CQ:C_GT])
    cq = _rms(lat[:, 0:Q_LORA], gcq_ref[...]).astype(BF16)
    ckv = _rms(lat[:, Q_LORA:Q_LORA + KV_LORA], gckv_ref[...]).astype(BF16)
    qn = _dot(cq, wq_ref[...])
    qr = _dot(cq, wqr_ref[...])
    kn = _dot(ckv, wk_ref[...])
    vv = _dot(ckv, wv_ref[...])
    for h in range(B_HEADS):
        sl = slice(h * LANES, (h + 1) * LANES)
        qb_ref[:, sl] = (qn[:, sl] * cs + qr[:, sl] * sn).astype(BF16)
        kb_ref[:, sl] = (kn[:, sl] + kpe).astype(BF16)
        vb_ref[:, sl] = (vv[:, sl] + one_at_v).astype(BF16)

    gl = _dot(hb, w_ref[:, C_GT:C_GT + 2 * d])
    gt_ref[...] = jax.nn.sigmoid(gl).astype(BF16)


def _proj(x1, mod3, g, w_all, gcq, wq, wqr, gckv, wk, wv, cs, sn, seq, tm):
    n, d = x1.shape
    tiles_per_batch = seq // tm
    row = lambda w: pl.BlockSpec((tm, w), lambda i: (i, 0))
    out_widths = (512, 512, 512, 128, 128, 1024, 1024, 1024, 2 * d)
    out_dtypes = (BF16, BF16, BF16, BF16, F32, BF16, BF16, BF16, BF16)
    return pl.pallas_call(
        functools.partial(_proj_kernel, d=d),
        grid=(n // tm,),
        in_specs=[row(d),
                  pl.BlockSpec((1, N_ADA, d), lambda i: (i // tiles_per_batch, 0, 0)),
                  _const_spec((1, d)), _const_spec(w_all.shape),
                  _const_spec(gcq.shape), _const_spec(wq.shape), _const_spec(wqr.shape),
                  _const_spec(gckv.shape), _const_spec(wk.shape), _const_spec(wv.shape),
                  row(LANES), row(LANES)],
        out_specs=[row(w) for w in out_widths],
        out_shape=[jax.ShapeDtypeStruct((n, w), dt) for w, dt in zip(out_widths, out_dtypes)],
        compiler_params=pltpu.CompilerParams(vmem_limit_bytes=VMEM_LIMIT),
        name="proj",
    )(x1, mod3, g, w_all, gcq, wq, wqr, gckv, wk, wv, cs, sn)


DSA_HEAD_GROUP = 2
ZERO_BAND = 1 << 20
GROUPS = 256
COUNT_UNKNOWN = 1 << 30
FINISH_STEPS = 2


def _rowsum8(x):
    v = x.reshape(x.shape[0] // SUBLANES, SUBLANES, x.shape[1])
    while v.shape[0] > 1:
        half = v.shape[0] // 2
        v = v[:half] + v[half:]
    return v[0]


def _t5_bias_tiles(posk, posq, tab_ref, heads):
    return _t5_bias_of_rel(posk - posq, tab_ref, heads)


def _t5_bias_of_rel(rel, tab_ref, heads):
    half = NUM_BUCKETS // 2
    max_exact = half // 2
    n = jnp.abs(rel)
    nf = jnp.maximum(n, 1).astype(F32)
    large = max_exact + (jnp.log(nf / max_exact) / math.log(MAX_DISTANCE / max_exact)
                         * (half - max_exact)).astype(I32)
    large = jnp.minimum(large, half - 1)
    bucket = jnp.where(rel > 0, half, 0) + jnp.where(n < max_exact, n, large)
    tk, tq = bucket.shape
    out = []
    for h in range(heads):
        tab = jnp.broadcast_to(tab_ref[h:h + 1, :], (tk, LANES))
        cols = [jnp.take_along_axis(tab, bucket[:, c * LANES:(c + 1) * LANES], axis=1,
                                    mode="promise_in_bounds")
                for c in range(tq // LANES)]
        out.append(jnp.concatenate(cols, axis=1))
    return out


def _dsa_kernel(pmin_ref, pmax_ref, prun_ref,
                qa_ref, qi_ref, wi_ref, sm_ref, vat_ref, posq_ref, posk_ref, tab_ref, far_ref,
                o_ref,
                keys_ref, gmax_ref, m_ref, acc_ref, s_ref, *, tq, topk, nq):
    b = pl.program_id(0)
    i = pl.program_id(1)
    tk = tq
    n_tiles = i + 1

    w_t = wi_ref[...].T[0:IDX_HEADS, :]
    key_l = lax.broadcasted_iota(I32, (tk, tq), 0)
    q_l = lax.broadcasted_iota(I32, (tk, tq), 1)
    diag_ok = key_l < (q_l // CHUNK + 1) * CHUNK
    not_key_l = ~key_l

    def score_tile(j, diag):
        r0 = pl.multiple_of(j * tk, tk)
        ki_lo = sm_ref[pl.ds(r0, tk), 256:384]
        ki_hi = sm_ref[pl.ds(r0, tk), 384:512]
        score = None
        for p in range(IDX_HEADS // 2):
            qp = qi_ref[:, p * LANES:(p + 1) * LANES]
            for h, kk in ((2 * p, ki_lo), (2 * p + 1, ki_hi)):
                t = w_t[h:h + 1, :] * jnp.maximum(_dot_nt(kk, qp), 0.0)
                score = t if score is None else score + t
        score = score + 0.0
        bits = lax.bitcast_convert_type(score, I32)
        key = jnp.where(bits < 0, (bits ^ 0x7FFFFFFF) - ZERO_BAND, bits)
        key = jnp.where(score == 0.0, not_key_l - j * tk, key)
        if diag:
            key = jnp.where(diag_ok, key, INT_MIN)
        keys_ref[pl.ds(r0, tk), :] = key
        gmax_ref[...] = jnp.maximum(gmax_ref[...],
                                    jnp.max(key.reshape(tk // GROUPS, GROUPS, tq), axis=0))

    gmax_ref[...] = jnp.full(gmax_ref.shape, INT_MIN, I32)
    lax.fori_loop(0, i, lambda j, c: (score_tile(j, False), c)[1], 0)
    score_tile(i, True)

    def count_tiles(pred):
        def body(j, acc):
            r0 = pl.multiple_of(j * tk, tk)
            return acc + _rowsum8(jnp.where(pred(keys_ref[pl.ds(r0, tk), :], j), 1, 0))
        acc = lax.fori_loop(0, n_tiles, body, jnp.zeros((SUBLANES, tq), I32))
        return jnp.sum(acc, axis=0, keepdims=True)

    q_row = lax.broadcasted_iota(I32, (1, tq), 1)
    n_adm = i * tq + (q_row // CHUNK + 1) * CHUNK

    gmax = gmax_ref[...]
    top = jnp.max(gmax, axis=0, keepdims=True)
    low = jnp.min(gmax, axis=0, keepdims=True)
    cnt_pos = count_tiles(lambda k_, j: k_ >= 1)
    cnt_zero = count_tiles(lambda k_, j: k_ >= -ZERO_BAND)
    few = n_adm <= topk
    in_pos = cnt_pos >= topk
    in_zero = jnp.logical_not(in_pos) & (cnt_zero >= topk)
    base_lo = jnp.where(in_pos, 1, jnp.where(in_zero, -ZERO_BAND, INT_MIN + 1))
    base_cnt = jnp.where(in_pos, cnt_pos, jnp.where(in_zero, cnt_zero, n_adm))
    hi0 = jnp.where(in_pos, top + 1, jnp.where(in_zero, 1, -ZERO_BAND))
    tighter = low > base_lo
    lo0 = jnp.where(few, INT_MIN + 1, jnp.where(tighter, low, base_lo))
    cnt0 = jnp.where(few, topk, jnp.where(tighter, COUNT_UNKNOWN, base_cnt))
    hi0 = jnp.where(few, INT_MIN + 2, hi0)

    def any_row(pred):
        return jnp.max(jnp.where(pred, 1, 0))

    def bisect(lo, hi, cnt_lo, slack):
        def pending(lo, hi, cnt_lo):
            return any_row((cnt_lo - topk > slack) & (hi - lo > 1))

        def step(st):
            it, lo, hi, cnt_lo, _ = st
            mid = lo + ((hi - lo) >> 1)
            cnt = count_tiles(lambda k_, j: k_ >= mid)
            take = cnt >= topk
            lo = jnp.where(take, mid, lo)
            hi = jnp.where(take, hi, mid)
            cnt_lo = jnp.where(take, cnt, cnt_lo)
            return it + 1, lo, hi, cnt_lo, pending(lo, hi, cnt_lo)

        return lax.while_loop(lambda st: (st[0] < 34) & (st[4] > 0), step,
                              (jnp.int32(0), lo, hi, cnt_lo, pending(lo, hi, cnt_lo)))

    _, lo1, hi1, cnt1, _ = bisect(lo0, hi0, cnt0, FINISH_STEPS)

    def surplus(lo, cnt_lo):
        return (cnt_lo > topk) & (cnt_lo - topk <= FINISH_STEPS) & (hi1 - lo > 1)

    def drop_step(st):
        it, lo, cnt_lo, _ = st
        def body(j, acc):
            r0 = pl.multiple_of(j * tk, tk)
            k_ = keys_ref[pl.ds(r0, tk), :]
            cand = jnp.where(k_ >= lo, k_, INT_MAX)
            v = cand.reshape(tk // SUBLANES, SUBLANES, tq)
            while v.shape[0] > 1:
                half = v.shape[0] // 2
                v = jnp.minimum(v[:half], v[half:])
            return jnp.minimum(acc, v[0])
        acc = lax.fori_loop(0, n_tiles, body, jnp.full((SUBLANES, tq), INT_MAX, I32))
        smallest = jnp.min(acc, axis=0, keepdims=True)
        go = surplus(lo, cnt_lo)
        lo = jnp.where(go, smallest + 1, lo)
        cnt_lo = jnp.where(go, cnt_lo - 1, cnt_lo)
        return it + 1, lo, cnt_lo, any_row(surplus(lo, cnt_lo))

    _, lo2, _, _ = lax.while_loop(lambda st: (st[0] < FINISH_STEPS) & (st[3] > 0), drop_step,
                                  (jnp.int32(0), lo1, cnt1, any_row(surplus(lo1, cnt1))))

    cnt2 = count_tiles(lambda k_, j: k_ >= lo2)
    redo = (cnt2 != topk) & jnp.logical_not(few)
    its, tau, _, _, _ = bisect(jnp.where(redo, lo1, lo2), jnp.where(redo, hi1, lo2 + 1),
                               jnp.where(redo, cnt1, topk), 0)
    cnt_ge = lax.cond(its > 0, lambda: count_tiles(lambda k_, j: k_ >= tau), lambda: cnt2)
    over = (cnt_ge > topk) & jnp.logical_not(few)
    tau_sel = tau

    @pl.when(jnp.max(jnp.where(over, 1, 0)) > 0)
    def _ties():
        cnt_gt = count_tiles(lambda k_, j: k_ > tau)
        room = topk - cnt_gt

        def idx_step(p, jmax):
            cand = jmax | jnp.left_shift(jnp.int32(1), 14 - p)
            cnt = count_tiles(lambda k_, j: (k_ == tau) & (key_l + j * tk < cand))
            return jnp.where(cnt <= room, cand, jmax)

        jmax = lax.fori_loop(0, 15, idx_step, jnp.zeros((1, tq), I32))

        def demote(j, carry):
            r0 = pl.multiple_of(j * tk, tk)
            k_ = keys_ref[pl.ds(r0, tk), :]
            drop = (k_ == tau) & (key_l + j * tk >= jmax) & over
            keys_ref[pl.ds(r0, tk), :] = jnp.where(drop, k_ - 1, k_)
            return carry

        lax.fori_loop(0, n_tiles, demote, 0)

    m_ref[...] = jnp.full(m_ref.shape, NEG_BIG, F32)
    acc_ref[...] = jnp.zeros(acc_ref.shape, F32)
    posq = posq_ref[0]

    def attn_tile(j, mode):
        near = mode != "far"
        r0 = pl.multiple_of(j * tk, tk)
        ka_lo = sm_ref[pl.ds(r0, tk), 0:128]
        ka_hi = sm_ref[pl.ds(r0, tk), 128:256]
        va_t = vat_ref[0:PV_ROWS, pl.ds(r0, tk)]
        maskb = jnp.where(keys_ref[pl.ds(r0, tk), :] >= tau_sel, 0.0, NEG_BIG)
        if mode == "any":
            posk_row = lax.bitcast_convert_type(posk_ref[0, :, pl.ds(r0, tk)], F32)
            posk_col = lax.bitcast_convert_type(
                jnp.broadcast_to(posk_row, (LANES, tk)).T, I32)
            bias = _t5_bias_tiles(jnp.tile(posk_col, (1, tq // LANES)), posq, tab_ref, A_HEADS)
        elif mode == "run":
            nb = tq // LANES
            shift_max = (nb - 1) * LANES
            d0 = pmin_ref[b * nq + j] - pmin_ref[b * nq + i]
            x = lax.broadcasted_iota(I32, (tk + shift_max, LANES), 0)
            l = lax.broadcasted_iota(I32, (tk + shift_max, LANES), 1)
            panel = _t5_bias_of_rel(x - l + (d0 - shift_max), tab_ref, A_HEADS)
            bias = [jnp.concatenate([p[shift_max - cb * LANES:shift_max - cb * LANES + tk]
                                     for cb in range(nb)], axis=1) for p in panel]
        m_prev = m_ref[...]
        m_next = [None] * A_HEADS
        slot = lambda h: (h // DSA_HEAD_GROUP % 2) * DSA_HEAD_GROUP + h % DSA_HEAD_GROUP

        def logits(g):
            for h in range(g * DSA_HEAD_GROUP, (g + 1) * DSA_HEAD_GROUP):
                s = _dot_nt(ka_hi if h % 2 else ka_lo,
                            qa_ref[:, (h // 2) * LANES:(h // 2 + 1) * LANES])
                s = s + ((bias[h] + maskb) if near else maskb)
                s_ref[slot(h)] = s
                s_max = jnp.max(s, axis=0, keepdims=True)
                m_next[h] = jnp.maximum(m_prev[h:h + 1, :],
                                        s_max if near else s_max + far_ref[h])

        def weighted_values(g):
            for h in range(g * DSA_HEAD_GROUP, (g + 1) * DSA_HEAD_GROUP):
                alpha = jnp.exp2(m_prev[h:h + 1, :] - m_next[h])
                shift = m_next[h] if near else m_next[h] - far_ref[h]
                pr = jnp.exp2(s_ref[slot(h)] - shift).astype(BF16)
                acc_ref[h, 0:PV_ROWS, :] = alpha * acc_ref[h, 0:PV_ROWS, :] + _dot(va_t, pr)

        _two_phase(A_HEADS // DSA_HEAD_GROUP, logits, weighted_values)
        m_ref[...] = jnp.concatenate(m_next, axis=0)

    def attn_body(j, carry):
        is_far = pmax_ref[b * nq + j] - pmin_ref[b * nq + i] <= -MAX_DISTANCE
        is_run = (prun_ref[b * nq + j] == 1) & (prun_ref[b * nq + i] == 1)
        lax.cond(is_far, lambda: attn_tile(j, "far"),
                 lambda: lax.cond(is_run, lambda: attn_tile(j, "run"),
                                  lambda: attn_tile(j, "any")))
        return carry

    lax.fori_loop(0, n_tiles, attn_body, 0)

    for h in range(A_HEADS):
        a = acc_ref[h]
        o_ref[:, h * LANES:(h + 1) * LANES] = (a / a[V_DIM:V_DIM + 1, :]).T.astype(BF16)


def _dsa(qa, qi, wi, sm, vat, posq, posk, tab, far, pmin, pmax, prun, batch, seq, tq, topk):
    n = qa.shape[0]
    nq = seq // tq
    row = lambda w: pl.BlockSpec((tq, w), lambda b, i, *_: (b * nq + i, 0))
    grid_spec = pltpu.PrefetchScalarGridSpec(
        num_scalar_prefetch=3,
        grid=(batch, nq),
        in_specs=[row(512), row(512), row(LANES),
                  pl.BlockSpec((seq, 512), lambda b, i, *_: (b, 0),
                               pipeline_mode=pl.Buffered(1)),
                  pl.BlockSpec((LANES, seq), lambda b, i, *_: (0, b)),
                  pl.BlockSpec((1, 1, tq), lambda b, i, *_: (b * nq + i, 0, 0)),
                  pl.BlockSpec((1, 1, seq), lambda b, i, *_: (b, 0, 0)),
                  pl.BlockSpec((A_HEADS, LANES), lambda b, i, *_: (0, 0)),
                  pl.BlockSpec(memory_space=pltpu.SMEM)],
        out_specs=row(A_HEADS * LANES),
        scratch_shapes=[pltpu.VMEM((seq, tq), I32),
                        pltpu.VMEM((GROUPS, tq), I32),
                        pltpu.VMEM((A_HEADS, tq), F32),
                        pltpu.VMEM((A_HEADS, LANES, tq), F32),
                        pltpu.VMEM((2 * DSA_HEAD_GROUP, tq, tq), F32)],
    )
    return pl.pallas_call(
        functools.partial(_dsa_kernel, tq=tq, topk=topk, nq=nq),
        grid_spec=grid_spec,
        out_shape=jax.ShapeDtypeStruct((n, A_HEADS * LANES), BF16),
        compiler_params=pltpu.CompilerParams(vmem_limit_bytes=VMEM_LIMIT),
        name="dsa",
    )(pmin, pmax, prun, qa, qi, wi, sm, vat, posq, posk, tab, far)


MLA_HEAD_GROUP = 8


def _mla_kernel(q_ref, k_ref, vt_ref, o_ref, acc_ref, *s_refs, tq):
    i = pl.program_id(2)
    tk = tq
    hg = MLA_HEAD_GROUP
    acc_ref[...] = jnp.zeros(acc_ref.shape, F32)

    def tile(j, ms, diag):
        r0 = pl.multiple_of(j * tk, tk)
        if diag:
            key_l = lax.broadcasted_iota(I32, (tk, tq), 0)
            q_l = lax.broadcasted_iota(I32, (tk, tq), 1)
            ok = key_l < (q_l // CHUNK + 1) * CHUNK
        head = lambda h: slice(h * LANES, (h + 1) * LANES)
        out = []
        for h in range(hg):
            s = _dot_nt(k_ref[pl.ds(r0, tk), head(h)], q_ref[:, head(h)])
            if diag:
                s = jnp.where(ok, s, NEG_BIG)
            s_refs[h][...] = s
            out.append(jnp.maximum(ms[h], jnp.max(s, axis=0, keepdims=True)))
        for h in range(hg):
            alpha = jnp.exp2(ms[h] - out[h])
            pr = jnp.exp2(s_refs[h][...] - out[h]).astype(BF16)
            v_t = vt_ref[h * LANES:h * LANES + PV_ROWS, pl.ds(r0, tk)]
            acc_ref[h, 0:PV_ROWS, :] = alpha * acc_ref[h, 0:PV_ROWS, :] + _dot(v_t, pr)
        return tuple(out)

    m0 = tuple(jnp.full((1, tq), NEG_BIG, F32) for _ in range(hg))
    ms = lax.fori_loop(0, i, lambda j, m_: tile(j, m_, False), m0)
    tile(i, ms, True)
    for h in range(hg):
        a = acc_ref[h]
        o_ref[:, h * LANES:(h + 1) * LANES] = (a / a[V_DIM:V_DIM + 1, :]).T.astype(BF16)


def _mla(qb, kb, vbt, batch, seq, tq):
    n = qb.shape[0]
    nq = seq // tq
    hg = MLA_HEAD_GROUP
    ng = B_HEADS // hg
    w = hg * LANES
    return pl.pallas_call(
        functools.partial(_mla_kernel, tq=tq),
        grid=(batch, ng, nq),
        in_specs=[pl.BlockSpec((tq, w), lambda b, g, i: (b * nq + i, g)),
                  pl.BlockSpec((seq, w), lambda b, g, i: (b, g), pipeline_mode=pl.Buffered(1)),
                  pl.BlockSpec((w, seq), lambda b, g, i: (b * ng + g, 0),
                               pipeline_mode=pl.Buffered(1))],
        out_specs=pl.BlockSpec((tq, w), lambda b, g, i: (b * nq + i, g)),
        out_shape=jax.ShapeDtypeStruct((n, B_HEADS * LANES), BF16),
        scratch_shapes=[pltpu.VMEM((hg, LANES, tq), F32)]
        + [pltpu.VMEM((tq, tq), F32) for _ in range(hg)],
        compiler_params=pltpu.CompilerParams(vmem_limit_bytes=VMEM_LIMIT),
        name="mla",
    )(qb, kb, vbt)


def _tail_kernel(x_ref, oa_ref, ob_ref, gt_ref, mod_ref, woa_ref, wob_ref, wout_ref,
                 g2_ref, win_ref, wdn_ref, gf_ref, o_ref, *, d, d_ff, chunk):
    x = x_ref[...]
    gt2 = mod_ref[0, 5:6, :]
    sh3, sc3, gt3 = mod_ref[0, 6:7, :], mod_ref[0, 7:8, :], mod_ref[0, 8:9, :]
    ya = _dot(oa_ref[...], woa_ref[...])
    yb = _dot(ob_ref[...], wob_ref[...])
    y = gt_ref[:, 0:d].astype(F32) * ya + gt_ref[:, d:2 * d].astype(F32) * yb
    x = x + gt2 * _dot(y.astype(BF16), wout_ref[...])
    h = _rms(x, g2_ref[...]) * (1.0 + sc3) + sh3
    f = _swiglu_tile(h.astype(BF16), win_ref, wdn_ref, d_ff, chunk)
    x = x + (0.5 * gt3) * f
    o_ref[...] = _rms(x, gf_ref[...])


def _tail(x1, oa, ob, gates, mod3, woa, wob, wout, g2, w_in, w_down, gf, seq, tm):
    n, d = x1.shape
    d_ff = w_down.shape[0]
    tiles_per_batch = seq // tm
    row = lambda w: pl.BlockSpec((tm, w), lambda i: (i, 0))
    kern = functools.partial(_tail_kernel, d=d, d_ff=d_ff, chunk=_ffn_chunk(d_ff))
    return pl.pallas_call(
        kern,
        grid=(n // tm,),
        in_specs=[row(d), row(oa.shape[1]), row(ob.shape[1]), row(2 * d),
                  pl.BlockSpec((1, N_ADA, d), lambda i: (i // tiles_per_batch, 0, 0)),
                  _const_spec(woa.shape), _const_spec(wob.shape), _const_spec(wout.shape),
                  _const_spec((1, d)), _const_spec(w_in.shape), _const_spec(w_down.shape),
                  _const_spec((1, d))],
        out_specs=row(d),
        out_shape=jax.ShapeDtypeStruct((n, d), F32),
        compiler_params=pltpu.CompilerParams(vmem_limit_bytes=VMEM_LIMIT),
        name="tail",
    )(x1, oa, ob, gates, mod3, woa, wob, wout, g2, w_in, w_down, gf)


def _pad_heads(w, heads, width):
    k = w.shape[0]
    w3 = w.reshape(k, heads, width)
    return jnp.pad(w3, ((0, 0), (0, 0), (0, LANES - width))).reshape(k, heads * LANES)


def _pad_head_rows(w, heads, width):
    dd = w.shape[1]
    w3 = w.reshape(heads, width, dd)
    return jnp.pad(w3, ((0, 0), (0, LANES - width), (0, 0))).reshape(heads * LANES, dd)


def _layout_w_in(w_in, d):
    z = lambda c: jnp.zeros((d, c), w_in.dtype)
    o = 0
    qa = w_in[:, o:o + 512] * (A_HEAD_DIM ** -0.5 * LOG2E); o += 512
    ka = w_in[:, o:o + 64]; o += 64
    va = w_in[:, o:o + 64]; o += 64
    qi = w_in[:, o:o + 512]; o += 512
    ki = w_in[:, o:o + 64]; o += 64
    wi = w_in[:, o:o + IDX_HEADS]; o += IDX_HEADS
    cq = w_in[:, o:o + Q_LORA]; o += Q_LORA
    ckv = w_in[:, o:o + KV_LORA]; o += KV_LORA
    kr = w_in[:, o:o + QK_ROPE]; o += QK_ROPE
    gates = w_in[:, o:o + 2 * d]
    hr = QK_ROPE // 2
    kt = jnp.concatenate([-kr[:, hr:], kr[:, :hr]], axis=1)
    cols = [qa, qi,
            ka, z(64), z(64), ka, va, z(64), ki, z(64), z(64), ki,
            wi, z(LANES - IDX_HEADS),
            z(64), kr, z(32), z(64), kt, z(32),
            cq, ckv, gates]
    return jnp.concatenate(cols, axis=1).astype(BF16)


def _layout_w_uq(w_uq):
    scale = (QK_NOPE + QK_ROPE) ** -0.5 * LOG2E
    k = w_uq.shape[0]
    w3 = w_uq.reshape(k, B_HEADS, QK_NOPE + QK_ROPE) * scale
    nope, pe = w3[:, :, :QK_NOPE], w3[:, :, QK_NOPE:]
    hr = QK_ROPE // 2
    rot = jnp.concatenate([-pe[:, :, hr:], pe[:, :, :hr]], axis=2)
    zpad = jnp.zeros((k, B_HEADS, LANES - QK_NOPE - QK_ROPE), w_uq.dtype)
    plain = jnp.concatenate([nope, pe, zpad], axis=2).reshape(k, B_HEADS * LANES)
    rotw = jnp.concatenate([jnp.zeros_like(nope), rot, zpad], axis=2).reshape(k, B_HEADS * LANES)
    return plain.astype(BF16), rotw.astype(BF16)


def _rope_tables(positions):
    half = QK_ROPE // 2
    freqs = ROPE_THETA ** (-2.0 * jnp.arange(half, dtype=F32) / QK_ROPE)
    ang = positions.astype(F32).reshape(-1)[:, None] * freqs
    cos, sin = jnp.cos(ang), jnp.sin(ang)
    n = ang.shape[0]
    ones = jnp.ones((n, QK_NOPE), F32)
    z32 = jnp.zeros((n, LANES - QK_NOPE - QK_ROPE), F32)
    cs = jnp.concatenate([ones, cos, cos, z32], axis=1)
    sn = jnp.concatenate([jnp.zeros((n, QK_NOPE), F32), sin, sin, z32], axis=1)
    return cs, sn


def _tiles(seq):
    tm = 512 if seq % 512 == 0 else 256
    tq_dsa = 512 if seq % 512 == 0 else 256
    tq_mla = 512 if seq % 512 == 0 else 256
    return tm, tq_dsa, tq_mla


def kernel(x, c, positions, w_ada, b_ada, g_ffn1, w_ffn1_in, w_ffn1_down, g_mix, w_in,
           g_cq, w_uq, g_ckv, w_uk, w_uv, rel_bias, w_o_a, w_o_b, w_out,
           g_ffn2, w_ffn2_in, w_ffn2_down, g_final):
    batch, seq, d = x.shape
    depth = w_ada.shape[0]
    n = batch * seq
    topk = min(TOPK_MAX, seq // 4)
    tm, tq_dsa, tq_mla = _tiles(seq)
    assert seq % tq_dsa == 0 and seq % tm == 0 and d % LANES == 0 and seq <= ZERO_BAND
    assert topk <= GROUPS and tq_dsa % GROUPS == 0
    assert depth == 1, "the tail kernel fuses the final norm into the single layer"

    xf = x.reshape(n, d)
    c_pad = jnp.pad(c, ((0, 8 - batch % 8 if batch % 8 else 0), (0, 0)))
    cs, sn = _rope_tables(positions)
    pos = positions.astype(I32)
    posq = pos.reshape(n // tq_dsa, 1, tq_dsa)
    posk = pos.reshape(batch, 1, seq)
    ptile = pos.reshape(n // tq_dsa, tq_dsa)
    pmin, pmax = ptile.min(axis=1), ptile.max(axis=1)
    prun = jnp.all(ptile[:, 1:] - ptile[:, :-1] == 1, axis=1).astype(I32)
    tab = jnp.pad(rel_bias.T.astype(F32) * LOG2E, ((0, 0), (0, LANES - NUM_BUCKETS)))
    far = rel_bias[NUM_BUCKETS // 2 - 1].astype(F32) * LOG2E

    l = 0
    mod = _ada(c_pad, w_ada[l], b_ada[l])[:batch]
    mod3 = mod.reshape(batch, N_ADA, d)
    x1 = _ffn1(xf, mod3, g_ffn1[l].reshape(1, d), w_ffn1_in[l].astype(BF16),
               w_ffn1_down[l].astype(BF16), seq, tm)
    wq, wqr = _layout_w_uq(w_uq[l])
    qa, qi, sm, va, wi, qb, kb, vb, gates = _proj(
        x1, mod3, g_mix[l].reshape(1, d), _layout_w_in(w_in[l], d),
        g_cq[l].reshape(1, Q_LORA), wq, wqr, g_ckv[l].reshape(1, KV_LORA),
        _pad_heads(w_uk[l], B_HEADS, QK_NOPE).astype(BF16),
        _pad_heads(w_uv[l], B_HEADS, V_DIM).astype(BF16),
        cs, sn, seq, tm)
    vat = va.T
    vbt = vb.reshape(batch, seq, B_HEADS * LANES).transpose(0, 2, 1).reshape(-1, seq)
    oa = _dsa(qa, qi, wi, sm, vat, posq, posk, tab, far, pmin, pmax, prun, batch, seq, tq_dsa,
              topk)
    ob = _mla(qb, kb, vbt, batch, seq, tq_mla)
    out = _tail(x1, oa, ob, gates, mod3,
                _pad_head_rows(w_o_a[l], A_HEADS, A_HEAD_DIM).astype(BF16),
                _pad_head_rows(w_o_b[l], B_HEADS, V_DIM).astype(BF16),
                w_out[l].astype(BF16), g_ffn2[l].reshape(1, d),
                w_ffn2_in[l].astype(BF16), w_ffn2_down[l].astype(BF16),
                g_final.reshape(1, d), seq, tm)
    return out.reshape(batch, seq, d)
```

```python
import functools
import math

import jax
import jax.numpy as jnp
import numpy as np
from jax import lax
from jax.experimental import pallas as pl
from jax.experimental.pallas import tpu as pltpu

F32 = jnp.float32
BF16 = jnp.bfloat16
I32 = jnp.int32

CHUNK = 64
EPS = 1e-6
A_HEADS = 8
A_HEAD_DIM = 64
IDX_HEADS = 8
IDX_DIM = 64
TOPK_MAX = 256
B_HEADS = 8
QK_NOPE = 64
QK_ROPE = 32
V_DIM = 64
Q_LORA = 384
KV_LORA = 256
ROPE_THETA = 10000.0
NUM_BUCKETS = 32
MAX_DISTANCE = 128
N_ADA = 9

LANES = 128
SUBLANES = 8
VMEM_LIMIT = 56 * 1024 * 1024
INT_MIN = -2 ** 31
INT_MAX = 2 ** 31 - 1
NEG_BIG = -1e30
LOG2E = math.log2(math.e)
BF16_ROWS = 16
PV_ROWS = -(-(V_DIM + 1) // BF16_ROWS) * BF16_ROWS

C_QA = 0
C_QI = 512
C_SM = 1024
C_WI = 1664
C_CQ = 2048
C_CKV = C_CQ + Q_LORA
C_GT = C_CKV + KV_LORA


def _dot(a, b):
    return jnp.dot(a, b, preferred_element_type=F32)


def _dot_nt(a, b):
    return lax.dot_general(a, b, (((1,), (1,)), ((), ())), preferred_element_type=F32)


def _dot_tn(a, b):
    return lax.dot_general(a, b, (((0,), (0,)), ((), ())), preferred_element_type=F32)


def _two_phase(n_groups, phase1, phase2, ahead=1):
    for g in range(min(ahead, n_groups)):
        phase1(g)
    for g in range(n_groups):
        if g + ahead < n_groups:
            phase1(g + ahead)
        phase2(g)


def _rms(x, g):
    return x * lax.rsqrt(jnp.mean(x * x, axis=-1, keepdims=True) + EPS) * g


def _const_spec(shape):
    nd = len(shape)
    return pl.BlockSpec(shape, lambda *_: (0,) * nd, pipeline_mode=pl.Buffered(1))


def _ada_kernel(c_ref, w_ref, b_ref, o_ref):
    c = c_ref[...]
    s = c * jax.nn.sigmoid(c)
    o_ref[...] = _dot(s.astype(BF16), w_ref[...].astype(BF16)) + b_ref[...]


def _ada(c_pad, w_ada, b_ada):
    d = c_pad.shape[1]
    n_out = w_ada.shape[1]
    return pl.pallas_call(
        _ada_kernel,
        grid=(n_out // d,),
        in_specs=[pl.BlockSpec(c_pad.shape, lambda j: (0, 0)),
                  pl.BlockSpec((d, d), lambda j: (0, j)),
                  pl.BlockSpec((1, d), lambda j: (0, j))],
        out_specs=pl.BlockSpec(c_pad.shape, lambda j: (0, j)),
        out_shape=jax.ShapeDtypeStruct((c_pad.shape[0], n_out), F32),
        compiler_params=pltpu.CompilerParams(vmem_limit_bytes=VMEM_LIMIT),
        name="ada",
    )(c_pad, w_ada, b_ada.reshape(1, n_out))


def _swiglu_tile(hb, win_ref, wdn_ref, d_ff, chunk):
    acc = None
    for j in range(d_ff // chunk):
        g = _dot(hb, win_ref[:, j * chunk:(j + 1) * chunk])
        u = _dot(hb, win_ref[:, d_ff + j * chunk:d_ff + (j + 1) * chunk])
        a = (g * jax.nn.sigmoid(g) * u).astype(BF16)
        part = _dot(a, wdn_ref[j * chunk:(j + 1) * chunk, :])
        acc = part if acc is None else acc + part
    return acc


def _ffn_chunk(d_ff):
    for c in (512, 256, 128):
        if d_ff % c == 0:
            return c
    raise ValueError("d_ff must be a multiple of 128")


def _ffn1_kernel(x_ref, mod_ref, g_ref, win_ref, wdn_ref, o_ref, *, d_ff, chunk):
    x = x_ref[...]
    sh, sc, gt = mod_ref[0, 0:1, :], mod_ref[0, 1:2, :], mod_ref[0, 2:3, :]
    h = _rms(x, g_ref[...]) * (1.0 + sc) + sh
    y = _swiglu_tile(h.astype(BF16), win_ref, wdn_ref, d_ff, chunk)
    o_ref[...] = x + (0.5 * gt) * y


def _ffn1(x2, mod3, g, w_in, w_down, seq, tm):
    n, d = x2.shape
    d_ff = w_down.shape[0]
    tiles_per_batch = seq // tm
    kern = functools.partial(_ffn1_kernel, d_ff=d_ff, chunk=_ffn_chunk(d_ff))
    return pl.pallas_call(
        kern,
        grid=(n // tm,),
        in_specs=[pl.BlockSpec((tm, d), lambda i: (i, 0)),
                  pl.BlockSpec((1, N_ADA, d), lambda i: (i // tiles_per_batch, 0, 0)),
                  _const_spec((1, d)),
                  _const_spec(w_in.shape),
                  _const_spec(w_down.shape)],
        out_specs=pl.BlockSpec((tm, d), lambda i: (i, 0)),
        out_shape=jax.ShapeDtypeStruct((n, d), F32),
        compiler_params=pltpu.CompilerParams(vmem_limit_bytes=VMEM_LIMIT),
        name="ffn1",
    )(x2, mod3, g, w_in, w_down)


def _proj_kernel(x_ref, mod_ref, g_ref, w_ref, gcq_ref, wq_ref, wqr_ref, gckv_ref,
                 wk_ref, wv_ref, pos_ref, freq_ref,
                 qa_ref, qi_ref, sm_ref, va_ref, wi_ref, qb_ref, kb_ref, vb_ref, gt_ref, *, d):
    x = x_ref[...]
    sh, sc = mod_ref[0, 3:4, :], mod_ref[0, 4:5, :]
    hb = (_rms(x, g_ref[...]) * (1.0 + sc) + sh).astype(BF16)
    lane = lax.broadcasted_iota(I32, (1, LANES), 1)
    one_at_v = (lane == V_DIM).astype(F32)
    ang = pos_ref[...].astype(F32) * freq_ref[...]
    cs = jnp.where(lane < QK_NOPE + QK_ROPE, jnp.cos(ang), 0.0)
    sn = jnp.sin(ang)

    qa_ref[...] = _dot(hb, w_ref[:, C_QA:C_QA + 512]).astype(BF16)
    qi_ref[...] = _dot(hb, w_ref[:, C_QI:C_QI + 512]).astype(BF16)

    sm = _dot(hb, w_ref[:, C_SM:C_SM + 640])
    sm_ref[:, 0:256] = sm[:, 0:256].astype(BF16)
    sm_ref[:, 256:512] = sm[:, 384:640].astype(BF16)
    va_ref[...] = (sm[:, 256:384] + one_at_v).astype(BF16)

    misc = _dot(hb, w_ref[:, C_WI:C_WI + 384])
    wi_ref[...] = misc[:, 0:128]
    kpe = misc[:, 128:256] * cs + misc[:, 256:384] * sn

    lat = _dot(hb, w_ref[:, C_CQ:C_GT])
    cq = _rms(lat[:, 0:Q_LORA], gcq_ref[...]).astype(BF16)
    ckv = _rms(lat[:, Q_LORA:Q_LORA + KV_LORA], gckv_ref[...]).astype(BF16)
    qn = _dot(cq, wq_ref[...])
    qr = _dot(cq, wqr_ref[...])
    kn = _dot(ckv, wk_ref[...])
    vv = _dot(ckv, wv_ref[...])
    for h in range(B_HEADS):
        sl = slice(h * LANES, (h + 1) * LANES)
        qb_ref[:, sl] = (qn[:, sl] * cs + qr[:, sl] * sn).astype(BF16)
        kb_ref[:, sl] = (kn[:, sl] + kpe).astype(BF16)
        vb_ref[:, sl] = (vv[:, sl] + one_at_v).astype(BF16)

    gl = _dot(hb, w_ref[:, C_GT:C_GT + 2 * d])
    gt_ref[...] = jax.nn.sigmoid(gl).astype(BF16)


def _proj(x1, mod3, g, w_all, gcq, wq, wqr, gckv, wk, wv, pos_col, freq_row, seq, tm):
    n, d = x1.shape
    tiles_per_batch = seq // tm
    row = lambda w: pl.BlockSpec((tm, w), lambda i: (i, 0))
    out_widths = (512, 512, 512, 128, 128, 1024, 1024, 1024, 2 * d)
    out_dtypes = (BF16, BF16, BF16, BF16, F32, BF16, BF16, BF16, BF16)
    return pl.pallas_call(
        functools.partial(_proj_kernel, d=d),
        grid=(n // tm,),
        in_specs=[row(d),
                  pl.BlockSpec((1, N_ADA, d), lambda i: (i // tiles_per_batch, 0, 0)),
                  _const_spec((1, d)), _const_spec(w_all.shape),
                  _const_spec(gcq.shape), _const_spec(wq.shape), _const_spec(wqr.shape),
                  _const_spec(gckv.shape), _const_spec(wk.shape), _const_spec(wv.shape),
                  row(1), _const_spec((1, LANES))],
        out_specs=[row(w) for w in out_widths],
        out_shape=[jax.ShapeDtypeStruct((n, w), dt) for w, dt in zip(out_widths, out_dtypes)],
        compiler_params=pltpu.CompilerParams(vmem_limit_bytes=VMEM_LIMIT),
        name="proj",
    )(x1, mod3, g, w_all, gcq, wq, wqr, gckv, wk, wv, pos_col, freq_row)


DSA_HEAD_GROUP = 2
ZERO_BAND = 1 << 20
GROUPS = 256
COUNT_UNKNOWN = 1 << 30
FINISH_STEPS = 2


def _rowsum8(x):
    v = x.reshape(x.shape[0] // SUBLANES, SUBLANES, x.shape[1])
    while v.shape[0] > 1:
        half = v.shape[0] // 2
        v = v[:half] + v[half:]
    return v[0]


def _t5_bias_tiles(posk, posq, tab_ref, heads):
    return _t5_bias_of_rel(posk - posq, tab_ref, heads)


def _t5_bias_of_rel(rel, tab_ref, heads):
    half = NUM_BUCKETS // 2
    max_exact = half // 2
    n = jnp.abs(rel)
    nf = jnp.maximum(n, 1).astype(F32)
    large = max_exact + (jnp.log(nf / max_exact) / math.log(MAX_DISTANCE / max_exact)
                         * (half - max_exact)).astype(I32)
    large = jnp.minimum(large, half - 1)
    bucket = jnp.where(rel > 0, half, 0) + jnp.where(n < max_exact, n, large)
    tk, tq = bucket.shape
    out = []
    for h in range(heads):
        tab = jnp.broadcast_to(tab_ref[h:h + 1, :], (tk, LANES))
        cols = [jnp.take_along_axis(tab, bucket[:, c * LANES:(c + 1) * LANES], axis=1,
                                    mode="promise_in_bounds")
                for c in range(tq // LANES)]
        out.append(jnp.concatenate(cols, axis=1))
    return out


def _dsa_kernel(pmin_ref, pmax_ref, prun_ref,
                qa_ref, qi_ref, wi_ref, sm_ref, vat_ref, posq_ref, posk_ref, tab_ref, far_ref,
                o_ref,
                keys_ref, gmax_ref, m_ref, acc_ref, s_ref, *, tq, topk, nq):
    b = pl.program_id(0)
    i = pl.program_id(1)
    tk = tq
    n_tiles = i + 1

    w_t = wi_ref[...].T[0:IDX_HEADS, :]
    key_l = lax.broadcasted_iota(I32, (tk, tq), 0)
    q_l = lax.broadcasted_iota(I32, (tk, tq), 1)
    diag_ok = key_l < (q_l // CHUNK + 1) * CHUNK
    not_key_l = ~key_l

    def score_tile(j, diag):
        r0 = pl.multiple_of(j * tk, tk)
        ki_lo = sm_ref[pl.ds(r0, tk), 256:384]
        ki_hi = sm_ref[pl.ds(r0, tk), 384:512]
        score = None
        for p in range(IDX_HEADS // 2):
            qp = qi_ref[:, p * LANES:(p + 1) * LANES]
            for h, kk in ((2 * p, ki_lo), (2 * p + 1, ki_hi)):
                t = w_t[h:h + 1, :] * jnp.maximum(_dot_nt(kk, qp), 0.0)
                score = t if score is None else score + t
        score = score + 0.0
        bits = lax.bitcast_convert_type(score, I32)
        key = jnp.where(bits < 0, (bits ^ 0x7FFFFFFF) - ZERO_BAND, bits)
        key = jnp.where(score == 0.0, not_key_l - j * tk, key)
        if diag:
            key = jnp.where(diag_ok, key, INT_MIN)
        keys_ref[pl.ds(r0, tk), :] = key
        gmax_ref[...] = jnp.maximum(gmax_ref[...],
                                    jnp.max(key.reshape(tk // GROUPS, GROUPS, tq), axis=0))

    gmax_ref[...] = jnp.full(gmax_ref.shape, INT_MIN, I32)
    lax.fori_loop(0, i, lambda j, c: (score_tile(j, False), c)[1], 0)
    score_tile(i, True)

    def count_tiles(pred):
        def body(j, acc):
            r0 = pl.multiple_of(j * tk, tk)
            return acc + _rowsum8(jnp.where(pred(keys_ref[pl.ds(r0, tk), :], j), 1, 0))
        acc = lax.fori_loop(0, n_tiles, body, jnp.zeros((SUBLANES, tq), I32))
        return jnp.sum(acc, axis=0, keepdims=True)

    q_row = lax.broadcasted_iota(I32, (1, tq), 1)
    n_adm = i * tq + (q_row // CHUNK + 1) * CHUNK

    gmax = gmax_ref[...]
    top = jnp.max(gmax, axis=0, keepdims=True)
    low = jnp.min(gmax, axis=0, keepdims=True)
    cnt_pos = count_tiles(lambda k_, j: k_ >= 1)
    cnt_zero = count_tiles(lambda k_, j: k_ >= -ZERO_BAND)
    few = n_adm <= topk
    in_pos = cnt_pos >= topk
    in_zero = jnp.logical_not(in_pos) & (cnt_zero >= topk)
    base_lo = jnp.where(in_pos, 1, jnp.where(in_zero, -ZERO_BAND, INT_MIN + 1))
    base_cnt = jnp.where(in_pos, cnt_pos, jnp.where(in_zero, cnt_zero, n_adm))
    hi0 = jnp.where(in_pos, top + 1, jnp.where(in_zero, 1, -ZERO_BAND))
    tighter = low > base_lo
    lo0 = jnp.where(few, INT_MIN + 1, jnp.where(tighter, low, base_lo))
    cnt0 = jnp.where(few, topk, jnp.where(tighter, COUNT_UNKNOWN, base_cnt))
    hi0 = jnp.where(few, INT_MIN + 2, hi0)

    def any_row(pred):
        return jnp.max(jnp.where(pred, 1, 0))

    def bisect(lo, hi, cnt_lo, slack):
        def pending(lo, hi, cnt_lo):
            return any_row((cnt_lo - topk > slack) & (hi - lo > 1))

        def step(st):
            it, lo, hi, cnt_lo, _ = st
            mid = lo + ((hi - lo) >> 1)
            cnt = count_tiles(lambda k_, j: k_ >= mid)
            take = cnt >= topk
            lo = jnp.where(take, mid, lo)
            hi = jnp.where(take, hi, mid)
            cnt_lo = jnp.where(take, cnt, cnt_lo)
            return it + 1, lo, hi, cnt_lo, pending(lo, hi, cnt_lo)

        return lax.while_loop(lambda st: (st[0] < 34) & (st[4] > 0), step,
                              (jnp.int32(0), lo, hi, cnt_lo, pending(lo, hi, cnt_lo)))

    _, lo1, hi1, cnt1, _ = bisect(lo0, hi0, cnt0, FINISH_STEPS)

    def surplus(lo, cnt_lo):
        return (cnt_lo > topk) & (cnt_lo - topk <= FINISH_STEPS) & (hi1 - lo > 1)

    def drop_step(st):
        it, lo, cnt_lo, _ = st
        def body(j, acc):
            r0 = pl.multiple_of(j * tk, tk)
            k_ = keys_ref[pl.ds(r0, tk), :]
            cand = jnp.where(k_ >= lo, k_, INT_MAX)
            v = cand.reshape(tk // SUBLANES, SUBLANES, tq)
            while v.shape[0] > 1:
                half = v.shape[0] // 2
                v = jnp.minimum(v[:half], v[half:])
            return jnp.minimum(acc, v[0])
        acc = lax.fori_loop(0, n_tiles, body, jnp.full((SUBLANES, tq), INT_MAX, I32))
        smallest = jnp.min(acc, axis=0, keepdims=True)
        go = surplus(lo, cnt_lo)
        lo = jnp.where(go, smallest + 1, lo)
        cnt_lo = jnp.where(go, cnt_lo - 1, cnt_lo)
        return it + 1, lo, cnt_lo, any_row(surplus(lo, cnt_lo))

    _, lo2, _, _ = lax.while_loop(lambda st: (st[0] < FINISH_STEPS) & (st[3] > 0), drop_step,
                                  (jnp.int32(0), lo1, cnt1, any_row(surplus(lo1, cnt1))))

    cnt2 = count_tiles(lambda k_, j: k_ >= lo2)
    redo = (cnt2 != topk) & jnp.logical_not(few)
    its, tau, _, _, _ = bisect(jnp.where(redo, lo1, lo2), jnp.where(redo, hi1, lo2 + 1),
                               jnp.where(redo, cnt1, topk), 0)
    cnt_ge = lax.cond(its > 0, lambda: count_tiles(lambda k_, j: k_ >= tau), lambda: cnt2)
    over = (cnt_ge > topk) & jnp.logical_not(few)
    tau_sel = tau

    @pl.when(jnp.max(jnp.where(over, 1, 0)) > 0)
    def _ties():
        cnt_gt = count_tiles(lambda k_, j: k_ > tau)
        room = topk - cnt_gt

        def idx_step(p, jmax):
            cand = jmax | jnp.left_shift(jnp.int32(1), 14 - p)
            cnt = count_tiles(lambda k_, j: (k_ == tau) & (key_l + j * tk < cand))
            return jnp.where(cnt <= room, cand, jmax)

        jmax = lax.fori_loop(0, 15, idx_step, jnp.zeros((1, tq), I32))

        def demote(j, carry):
            r0 = pl.multiple_of(j * tk, tk)
            k_ = keys_ref[pl.ds(r0, tk), :]
            drop = (k_ == tau) & (key_l + j * tk >= jmax) & over
            keys_ref[pl.ds(r0, tk), :] = jnp.where(drop, k_ - 1, k_)
            return carry

        lax.fori_loop(0, n_tiles, demote, 0)

    m_ref[...] = jnp.full(m_ref.shape, NEG_BIG, F32)
    acc_ref[...] = jnp.zeros(acc_ref.shape, F32)
    posq = posq_ref[0]

    def attn_tile(j, mode):
        near = mode != "far"
        r0 = pl.multiple_of(j * tk, tk)
        ka_lo = sm_ref[pl.ds(r0, tk), 0:128]
        ka_hi = sm_ref[pl.ds(r0, tk), 128:256]
        va_t = vat_ref[0:PV_ROWS, pl.ds(r0, tk)]
        maskb = jnp.where(keys_ref[pl.ds(r0, tk), :] >= tau_sel, 0.0, NEG_BIG)
        if mode == "any":
            posk_row = lax.bitcast_convert_type(posk_ref[0, :, pl.ds(r0, tk)], F32)
            posk_col = lax.bitcast_convert_type(
                jnp.broadcast_to(posk_row, (LANES, tk)).T, I32)
            bias = _t5_bias_tiles(jnp.tile(posk_col, (1, tq // LANES)), posq, tab_ref, A_HEADS)
        elif mode == "run":
            nb = tq // LANES
            shift_max = (nb - 1) * LANES
            d0 = pmin_ref[b * nq + j] - pmin_ref[b * nq + i]
            x = lax.broadcasted_iota(I32, (tk + shift_max, LANES), 0)
            l = lax.broadcasted_iota(I32, (tk + shift_max, LANES), 1)
            panel = _t5_bias_of_rel(x - l + (d0 - shift_max), tab_ref, A_HEADS)
            bias = [jnp.concatenate([p[shift_max - cb * LANES:shift_max - cb * LANES + tk]
                                     for cb in range(nb)], axis=1) for p in panel]
        m_prev = m_ref[...]
        m_next = [None] * A_HEADS
        slot = lambda h: (h // DSA_HEAD_GROUP % 2) * DSA_HEAD_GROUP + h % DSA_HEAD_GROUP

        def logits(g):
            for h in range(g * DSA_HEAD_GROUP, (g + 1) * DSA_HEAD_GROUP):
                s = _dot_nt(ka_hi if h % 2 else ka_lo,
                            qa_ref[:, (h // 2) * LANES:(h // 2 + 1) * LANES])
                s = s + ((bias[h] + maskb) if near else maskb)
                s_ref[slot(h)] = s
                s_max = jnp.max(s, axis=0, keepdims=True)
                m_next[h] = jnp.maximum(m_prev[h:h + 1, :],
                                        s_max if near else s_max + far_ref[h])

        def weighted_values(g):
            for h in range(g * DSA_HEAD_GROUP, (g + 1) * DSA_HEAD_GROUP):
                alpha = jnp.exp2(m_prev[h:h + 1, :] - m_next[h])
                shift = m_next[h] if near else m_next[h] - far_ref[h]
                pr = jnp.exp2(s_ref[slot(h)] - shift).astype(BF16)
                acc_ref[h, 0:PV_ROWS, :] = alpha * acc_ref[h, 0:PV_ROWS, :] + _dot(va_t, pr)

        _two_phase(A_HEADS // DSA_HEAD_GROUP, logits, weighted_values)
        m_ref[...] = jnp.concatenate(m_next, axis=0)

    def attn_body(j, carry):
        is_far = pmax_ref[b * nq + j] - pmin_ref[b * nq + i] <= -MAX_DISTANCE
        is_run = (prun_ref[b * nq + j] == 1) & (prun_ref[b * nq + i] == 1)
        lax.cond(is_far, lambda: attn_tile(j, "far"),
                 lambda: lax.cond(is_run, lambda: attn_tile(j, "run"),
                                  lambda: attn_tile(j, "any")))
        return carry

    lax.fori_loop(0, n_tiles, attn_body, 0)

    for h in range(A_HEADS):
        a = acc_ref[h]
        o_ref[:, h * LANES:(h + 1) * LANES] = (a / a[V_DIM:V_DIM + 1, :]).T.astype(BF16)


def _dsa(qa, qi, wi, sm, vat, posq, posk, tab, far, pmin, pmax, prun, batch, seq, tq, topk):
    n = qa.shape[0]
    nq = seq // tq
    row = lambda w: pl.BlockSpec((tq, w), lambda b, i, *_: (b * nq + i, 0))
    grid_spec = pltpu.PrefetchScalarGridSpec(
        num_scalar_prefetch=3,
        grid=(batch, nq),
        in_specs=[row(512), row(512), row(LANES),
                  pl.BlockSpec((seq, 512), lambda b, i, *_: (b, 0),
                               pipeline_mode=pl.Buffered(1)),
                  pl.BlockSpec((LANES, seq), lambda b, i, *_: (0, b)),
                  pl.BlockSpec((1, 1, tq), lambda b, i, *_: (b * nq + i, 0, 0)),
                  pl.BlockSpec((1, 1, seq), lambda b, i, *_: (b, 0, 0)),
                  pl.BlockSpec((A_HEADS, LANES), lambda b, i, *_: (0, 0)),
                  pl.BlockSpec(memory_space=pltpu.SMEM)],
        out_specs=row(A_HEADS * LANES),
        scratch_shapes=[pltpu.VMEM((seq, tq), I32),
                        pltpu.VMEM((GROUPS, tq), I32),
                        pltpu.VMEM((A_HEADS, tq), F32),
                        pltpu.VMEM((A_HEADS, LANES, tq), F32),
                        pltpu.VMEM((2 * DSA_HEAD_GROUP, tq, tq), F32)],
    )
    return pl.pallas_call(
        functools.partial(_dsa_kernel, tq=tq, topk=topk, nq=nq),
        grid_spec=grid_spec,
        out_shape=jax.ShapeDtypeStruct((n, A_HEADS * LANES), BF16),
        compiler_params=pltpu.CompilerParams(vmem_limit_bytes=VMEM_LIMIT),
        name="dsa",
    )(pmin, pmax, prun, qa, qi, wi, sm, vat, posq, posk, tab, far)


MLA_HEAD_GROUP = 8
MLA_AHEAD = 2


def _mla_kernel(q_ref, k_ref, vt_ref, o_ref, acc_ref, *s_refs, tq):
    i = pl.program_id(2)
    tk = tq
    hg = MLA_HEAD_GROUP
    acc_ref[...] = jnp.zeros(acc_ref.shape, F32)

    def tile(j, ms, diag):
        r0 = pl.multiple_of(j * tk, tk)
        if diag:
            key_l = lax.broadcasted_iota(I32, (tk, tq), 0)
            q_l = lax.broadcasted_iota(I32, (tk, tq), 1)
            ok = key_l < (q_l // CHUNK + 1) * CHUNK
        head = lambda h: slice(h * LANES, (h + 1) * LANES)
        out = [None] * hg

        def logits(h):
            s = _dot_nt(k_ref[pl.ds(r0, tk), head(h)], q_ref[:, head(h)])
            if diag:
                s = jnp.where(ok, s, NEG_BIG)
            s_refs[h][...] = s
            out[h] = jnp.maximum(ms[h], jnp.max(s, axis=0, keepdims=True))

        def weighted_values(h):
            alpha = jnp.exp2(ms[h] - out[h])
            pr = jnp.exp2(s_refs[h][...] - out[h]).astype(BF16)
            v_t = vt_ref[h * LANES:h * LANES + PV_ROWS, pl.ds(r0, tk)]
            acc_ref[h, 0:PV_ROWS, :] = alpha * acc_ref[h, 0:PV_ROWS, :] + _dot(v_t, pr)

        _two_phase(hg, logits, weighted_values, ahead=MLA_AHEAD)
        return tuple(out)

    m0 = tuple(jnp.full((1, tq), NEG_BIG, F32) for _ in range(hg))
    ms = lax.fori_loop(0, i, lambda j, m_: tile(j, m_, False), m0)
    tile(i, ms, True)
    for h in range(hg):
        a = acc_ref[h]
        o_ref[:, h * LANES:(h + 1) * LANES] = (a / a[V_DIM:V_DIM + 1, :]).T.astype(BF16)


def _mla(qb, kb, vbt, batch, seq, tq):
    n = qb.shape[0]
    nq = seq // tq
    hg = MLA_HEAD_GROUP
    ng = B_HEADS // hg
    w = hg * LANES
    return pl.pallas_call(
        functools.partial(_mla_kernel, tq=tq),
        grid=(batch, ng, nq),
        in_specs=[pl.BlockSpec((tq, w), lambda b, g, i: (b * nq + i, g)),
                  pl.BlockSpec((seq, w), lambda b, g, i: (b, g), pipeline_mode=pl.Buffered(1)),
                  pl.BlockSpec((w, seq), lambda b, g, i: (b * ng + g, 0),
                               pipeline_mode=pl.Buffered(1))],
        out_specs=pl.BlockSpec((tq, w), lambda b, g, i: (b * nq + i, g)),
        out_shape=jax.ShapeDtypeStruct((n, B_HEADS * LANES), BF16),
        scratch_shapes=[pltpu.VMEM((hg, LANES, tq), F32)]
        + [pltpu.VMEM((tq, tq), F32) for _ in range(hg)],
        compiler_params=pltpu.CompilerParams(vmem_limit_bytes=VMEM_LIMIT),
        name="mla",
    )(qb, kb, vbt)


def _tail_kernel(x_ref, oa_ref, ob_ref, gt_ref, mod_ref, woa_ref, wob_ref, wout_ref,
                 g2_ref, win_ref, wdn_ref, gf_ref, o_ref, *, d, d_ff, chunk):
    x = x_ref[...]
    gt2 = mod_ref[0, 5:6, :]
    sh3, sc3, gt3 = mod_ref[0, 6:7, :], mod_ref[0, 7:8, :], mod_ref[0, 8:9, :]
    ya = _dot(oa_ref[...], woa_ref[...])
    yb = _dot(ob_ref[...], wob_ref[...])
    y = gt_ref[:, 0:d].astype(F32) * ya + gt_ref[:, d:2 * d].astype(F32) * yb
    x = x + gt2 * _dot(y.astype(BF16), wout_ref[...])
    h = _rms(x, g2_ref[...]) * (1.0 + sc3) + sh3
    f = _swiglu_tile(h.astype(BF16), win_ref, wdn_ref, d_ff, chunk)
    x = x + (0.5 * gt3) * f
    o_ref[...] = _rms(x, gf_ref[...])


def _tail(x1, oa, ob, gates, mod3, woa, wob, wout, g2, w_in, w_down, gf, seq, tm):
    n, d = x1.shape
    d_ff = w_down.shape[0]
    tiles_per_batch = seq // tm
    row = lambda w: pl.BlockSpec((tm, w), lambda i: (i, 0))
    kern = functools.partial(_tail_kernel, d=d, d_ff=d_ff, chunk=_ffn_chunk(d_ff))
    return pl.pallas_call(
        kern,
        grid=(n // tm,),
        in_specs=[row(d), row(oa.shape[1]), row(ob.shape[1]), row(2 * d),
                  pl.BlockSpec((1, N_ADA, d), lambda i: (i // tiles_per_batch, 0, 0)),
                  _const_spec(woa.shape), _const_spec(wob.shape), _const_spec(wout.shape),
                  _const_spec((1, d)), _const_spec(w_in.shape), _const_spec(w_down.shape),
                  _const_spec((1, d))],
        out_specs=row(d),
        out_shape=jax.ShapeDtypeStruct((n, d), F32),
        compiler_params=pltpu.CompilerParams(vmem_limit_bytes=VMEM_LIMIT),
        name="tail",
    )(x1, oa, ob, gates, mod3, woa, wob, wout, g2, w_in, w_down, gf)


def _pad_heads(w, heads, width):
    k = w.shape[0]
    w3 = w.reshape(k, heads, width)
    return jnp.pad(w3, ((0, 0), (0, 0), (0, LANES - width))).reshape(k, heads * LANES)


def _pad_head_rows(w, heads, width):
    dd = w.shape[1]
    w3 = w.reshape(heads, width, dd)
    return jnp.pad(w3, ((0, 0), (0, LANES - width), (0, 0))).reshape(heads * LANES, dd)


def _layout_w_in(w_in, d):
    z = lambda c: jnp.zeros((d, c), w_in.dtype)
    o = 0
    qa = w_in[:, o:o + 512] * (A_HEAD_DIM ** -0.5 * LOG2E); o += 512
    ka = w_in[:, o:o + 64]; o += 64
    va = w_in[:, o:o + 64]; o += 64
    qi = w_in[:, o:o + 512]; o += 512
    ki = w_in[:, o:o + 64]; o += 64
    wi = w_in[:, o:o + IDX_HEADS]; o += IDX_HEADS
    cq = w_in[:, o:o + Q_LORA]; o += Q_LORA
    ckv = w_in[:, o:o + KV_LORA]; o += KV_LORA
    kr = w_in[:, o:o + QK_ROPE]; o += QK_ROPE
    gates = w_in[:, o:o + 2 * d]
    hr = QK_ROPE // 2
    kt = jnp.concatenate([-kr[:, hr:], kr[:, :hr]], axis=1)
    cols = [qa, qi,
            ka, z(64), z(64), ka, va, z(64), ki, z(64), z(64), ki,
            wi, z(LANES - IDX_HEADS),
            z(64), kr, z(32), z(64), kt, z(32),
            cq, ckv, gates]
    return jnp.concatenate(cols, axis=1).astype(BF16)


def _layout_w_uq(w_uq):
    scale = (QK_NOPE + QK_ROPE) ** -0.5 * LOG2E
    dh = QK_NOPE + QK_ROPE
    hr = QK_ROPE // 2
    src_plain = np.zeros(B_HEADS * LANES, np.int32)
    mul_plain = np.zeros(B_HEADS * LANES, np.float32)
    src_rot = np.zeros(B_HEADS * LANES, np.int32)
    mul_rot = np.zeros(B_HEADS * LANES, np.float32)
    for h in range(B_HEADS):
        for c in range(dh):
            src_plain[h * LANES + c] = h * dh + c
            mul_plain[h * LANES + c] = scale
        for c in range(QK_ROPE):
            first = c < hr
            src_rot[h * LANES + QK_NOPE + c] = h * dh + QK_NOPE + (c + hr if first else c - hr)
            mul_rot[h * LANES + QK_NOPE + c] = -scale if first else scale
    plain = jnp.take(w_uq, src_plain, axis=1) * mul_plain
    rotw = jnp.take(w_uq, src_rot, axis=1) * mul_rot
    return plain.astype(BF16), rotw.astype(BF16)


def _rope_freq_row():
    half = QK_ROPE // 2
    freqs = ROPE_THETA ** (-2.0 * jnp.arange(half, dtype=F32) / QK_ROPE)
    return jnp.concatenate([jnp.zeros((QK_NOPE,), F32), freqs, freqs,
                            jnp.zeros((LANES - QK_NOPE - QK_ROPE,), F32)]).reshape(1, LANES)


def _tiles(seq):
    tm = 512 if seq % 512 == 0 else 256
    tq_dsa = 512 if seq % 512 == 0 else 256
    tq_mla = 512 if seq % 512 == 0 else 256
    return tm, tq_dsa, tq_mla


def kernel(x, c, positions, w_ada, b_ada, g_ffn1, w_ffn1_in, w_ffn1_down, g_mix, w_in,
           g_cq, w_uq, g_ckv, w_uk, w_uv, rel_bias, w_o_a, w_o_b, w_out,
           g_ffn2, w_ffn2_in, w_ffn2_down, g_final):
    batch, seq, d = x.shape
    depth = w_ada.shape[0]
    n = batch * seq
    topk = min(TOPK_MAX, seq // 4)
    tm, tq_dsa, tq_mla = _tiles(seq)
    assert seq % tq_dsa == 0 and seq % tm == 0 and d % LANES == 0 and seq <= ZERO_BAND
    assert topk <= GROUPS and tq_dsa % GROUPS == 0
    assert depth == 1, "the tail kernel fuses the final norm into the single layer"

    xf = x.reshape(n, d)
    c_pad = jnp.pad(c, ((0, 8 - batch % 8 if batch % 8 else 0), (0, 0)))
    pos = positions.astype(I32)
    posq = pos.reshape(n // tq_dsa, 1, tq_dsa)
    posk = pos.reshape(batch, 1, seq)
    ptile = pos.reshape(n // tq_dsa, tq_dsa)
    pmin, pmax = ptile.min(axis=1), ptile.max(axis=1)
    prun = jnp.all(ptile[:, 1:] - ptile[:, :-1] == 1, axis=1).astype(I32)
    tab = jnp.pad(rel_bias.T.astype(F32) * LOG2E, ((0, 0), (0, LANES - NUM_BUCKETS)))
    far = rel_bias[NUM_BUCKETS // 2 - 1].astype(F32) * LOG2E

    l = 0
    mod = _ada(c_pad, w_ada[l], b_ada[l])[:batch]
    mod3 = mod.reshape(batch, N_ADA, d)
    x1 = _ffn1(xf, mod3, g_ffn1[l].reshape(1, d), w_ffn1_in[l].astype(BF16),
               w_ffn1_down[l].astype(BF16), seq, tm)
    wq, wqr = _layout_w_uq(w_uq[l])
    qa, qi, sm, va, wi, qb, kb, vb, gates = _proj(
        x1, mod3, g_mix[l].reshape(1, d), _layout_w_in(w_in[l], d),
        g_cq[l].reshape(1, Q_LORA), wq, wqr, g_ckv[l].reshape(1, KV_LORA),
        _pad_heads(w_uk[l], B_HEADS, QK_NOPE).astype(BF16),
        _pad_heads(w_uv[l], B_HEADS, V_DIM).astype(BF16),
        pos.reshape(n, 1), _rope_freq_row(), seq, tm)
    vat = va.T
    vbt = vb.reshape(batch, seq, B_HEADS * LANES).transpose(0, 2, 1).reshape(-1, seq)
    oa = _dsa(qa, qi, wi, sm, vat, posq, posk, tab, far, pmin, pmax, prun, batch, seq, tq_dsa,
              topk)
    ob = _mla(qb, kb, vbt, batch, seq, tq_mla)
    out = _tail(x1, oa, ob, gates, mod3,
                _pad_head_rows(w_o_a[l], A_HEADS, A_HEAD_DIM).astype(BF16),
                _pad_head_rows(w_o_b[l], B_HEADS, V_DIM).astype(BF16),
                w_out[l].astype(BF16), g_ffn2[l].reshape(1, d),
                w_ffn2_in[l].astype(BF16), w_ffn2_down[l].astype(BF16),
                g_final.reshape(1, d), seq, tm)
    return out.reshape(batch, seq, d)
```

```python
import functools
import math

import jax
import jax.numpy as jnp
import numpy as np
from jax import lax
from jax.experimental import pallas as pl
from jax.experimental.pallas import tpu as pltpu

F32 = jnp.float32
BF16 = jnp.bfloat16
I32 = jnp.int32

CHUNK = 64
EPS = 1e-6
A_HEADS = 8
A_HEAD_DIM = 64
IDX_HEADS = 8
IDX_DIM = 64
TOPK_MAX = 256
B_HEADS = 8
QK_NOPE = 64
QK_ROPE = 32
V_DIM = 64
Q_LORA = 384
KV_LORA = 256
ROPE_THETA = 10000.0
NUM_BUCKETS = 32
MAX_DISTANCE = 128
N_ADA = 9

LANES = 128
SUBLANES = 8
VMEM_LIMIT = 56 * 1024 * 1024
INT_MIN = -2 ** 31
INT_MAX = 2 ** 31 - 1
NEG_BIG = -1e30
LOG2E = math.log2(math.e)
BF16_ROWS = 16
PV_ROWS = -(-(V_DIM + 1) // BF16_ROWS) * BF16_ROWS

C_QA = 0
C_QI = 512
C_SM = 1024
C_WI = 1664
C_CQ = 2048
C_CKV = C_CQ + Q_LORA
C_GT = C_CKV + KV_LORA


def _dot(a, b):
    return jnp.dot(a, b, preferred_element_type=F32)


def _dot_nt(a, b):
    return lax.dot_general(a, b, (((1,), (1,)), ((), ())), preferred_element_type=F32)


def _dot_tn(a, b):
    return lax.dot_general(a, b, (((0,), (0,)), ((), ())), preferred_element_type=F32)


def _two_phase(n_groups, phase1, phase2, ahead=1):
    for g in range(min(ahead, n_groups)):
        phase1(g)
    for g in range(n_groups):
        if g + ahead < n_groups:
            phase1(g + ahead)
        phase2(g)


def _rms(x, g):
    return x * lax.rsqrt(jnp.mean(x * x, axis=-1, keepdims=True) + EPS) * g


def _const_spec(shape):
    nd = len(shape)
    return pl.BlockSpec(shape, lambda *_: (0,) * nd, pipeline_mode=pl.Buffered(1))


def _ada_kernel(c_ref, w_ref, b_ref, o_ref):
    c = c_ref[...]
    s = c * jax.nn.sigmoid(c)
    o_ref[...] = _dot(s.astype(BF16), w_ref[...].astype(BF16)) + b_ref[...]


def _ada(c_pad, w_ada, b_ada):
    d = c_pad.shape[1]
    n_out = w_ada.shape[1]
    return pl.pallas_call(
        _ada_kernel,
        grid=(n_out // d,),
        in_specs=[pl.BlockSpec(c_pad.shape, lambda j: (0, 0)),
                  pl.BlockSpec((d, d), lambda j: (0, j)),
                  pl.BlockSpec((1, d), lambda j: (0, j))],
        out_specs=pl.BlockSpec(c_pad.shape, lambda j: (0, j)),
        out_shape=jax.ShapeDtypeStruct((c_pad.shape[0], n_out), F32),
        compiler_params=pltpu.CompilerParams(vmem_limit_bytes=VMEM_LIMIT),
        name="ada",
    )(c_pad, w_ada, b_ada.reshape(1, n_out))


def _swiglu_tile(hb, win_ref, wdn_ref, d_ff, chunk):
    acc = None
    for j in range(d_ff // chunk):
        g = _dot(hb, win_ref[:, j * chunk:(j + 1) * chunk])
        u = _dot(hb, win_ref[:, d_ff + j * chunk:d_ff + (j + 1) * chunk])
        a = (g * jax.nn.sigmoid(g) * u).astype(BF16)
        part = _dot(a, wdn_ref[j * chunk:(j + 1) * chunk, :])
        acc = part if acc is None else acc + part
    return acc


def _ffn_chunk(d_ff):
    for c in (512, 256, 128):
        if d_ff % c == 0:
            return c
    raise ValueError("d_ff must be a multiple of 128")


def _ffn1_kernel(x_ref, mod_ref, g_ref, win_ref, wdn_ref, o_ref, *, d_ff, chunk):
    x = x_ref[...]
    sh, sc, gt = mod_ref[0, 0:1, :], mod_ref[0, 1:2, :], mod_ref[0, 2:3, :]
    h = _rms(x, g_ref[...]) * (1.0 + sc) + sh
    y = _swiglu_tile(h.astype(BF16), win_ref, wdn_ref, d_ff, chunk)
    o_ref[...] = x + (0.5 * gt) * y


def _ffn1(x2, mod3, g, w_in, w_down, seq, tm):
    n, d = x2.shape
    d_ff = w_down.shape[0]
    tiles_per_batch = seq // tm
    kern = functools.partial(_ffn1_kernel, d_ff=d_ff, chunk=_ffn_chunk(d_ff))
    return pl.pallas_call(
        kern,
        grid=(n // tm,),
        in_specs=[pl.BlockSpec((tm, d), lambda i: (i, 0)),
                  pl.BlockSpec((1, N_ADA, d), lambda i: (i // tiles_per_batch, 0, 0)),
                  _const_spec((1, d)),
                  _const_spec(w_in.shape),
                  _const_spec(w_down.shape)],
        out_specs=pl.BlockSpec((tm, d), lambda i: (i, 0)),
        out_shape=jax.ShapeDtypeStruct((n, d), F32),
        compiler_params=pltpu.CompilerParams(vmem_limit_bytes=VMEM_LIMIT),
        name="ffn1",
    )(x2, mod3, g, w_in, w_down)


def _proj_kernel(x_ref, mod_ref, g_ref, w_ref, gcq_ref, wq_ref, wqr_ref, gckv_ref,
                 wk_ref, wv_ref, pos_ref, freq_ref,
                 qa_ref, qi_ref, sm_ref, va_ref, wi_ref, qb_ref, kb_ref, vb_ref, gt_ref, *, d):
    x = x_ref[...]
    sh, sc = mod_ref[0, 3:4, :], mod_ref[0, 4:5, :]
    hb = (_rms(x, g_ref[...]) * (1.0 + sc) + sh).astype(BF16)
    lane = lax.broadcasted_iota(I32, (1, LANES), 1)
    one_at_v = (lane == V_DIM).astype(F32)
    ang = pos_ref[...].astype(F32) * freq_ref[...]
    cs = jnp.where(lane < QK_NOPE + QK_ROPE, jnp.cos(ang), 0.0)
    sn = jnp.sin(ang)

    qa_ref[...] = _dot(hb, w_ref[:, C_QA:C_QA + 512]).astype(BF16)
    qi_ref[...] = _dot(hb, w_ref[:, C_QI:C_QI + 512]).astype(BF16)

    sm = _dot(hb, w_ref[:, C_SM:C_SM + 640])
    sm_ref[:, 0:256] = sm[:, 0:256].astype(BF16)
    sm_ref[:, 256:512] = sm[:, 384:640].astype(BF16)
    va_ref[...] = (sm[:, 256:384] + one_at_v).astype(BF16)

    misc = _dot(hb, w_ref[:, C_WI:C_WI + 384])
    wi_ref[...] = misc[:, 0:128]
    kpe = misc[:, 128:256] * cs + misc[:, 256:384] * sn

    lat = _dot(hb, w_ref[:, C_CQ:C_GT])
    cq = _rms(lat[:, 0:Q_LORA], gcq_ref[...]).astype(BF16)
    ckv = _rms(lat[:, Q_LORA:Q_LORA + KV_LORA], gckv_ref[...]).astype(BF16)
    qn = _dot(cq, wq_ref[...])
    qr = _dot(cq, wqr_ref[...])
    kn = _dot(ckv, wk_ref[...])
    vv = _dot(ckv, wv_ref[...])
    for h in range(B_HEADS):
        sl = slice(h * LANES, (h + 1) * LANES)
        qb_ref[:, sl] = (qn[:, sl] * cs + qr[:, sl] * sn).astype(BF16)
        kb_ref[:, sl] = (kn[:, sl] + kpe).astype(BF16)
        vb_ref[:, sl] = (vv[:, sl] + one_at_v).astype(BF16)

    gl = _dot(hb, w_ref[:, C_GT:C_GT + 2 * d])
    gt_ref[...] = jax.nn.sigmoid(gl).astype(BF16)


def _proj(x1, mod3, g, w_all, gcq, wq, wqr, gckv, wk, wv, pos_col, freq_row, seq, tm):
    n, d = x1.shape
    tiles_per_batch = seq // tm
    row = lambda w: pl.BlockSpec((tm, w), lambda i: (i, 0))
    out_widths = (512, 512, 512, 128, 128, 1024, 1024, 1024, 2 * d)
    out_dtypes = (BF16, BF16, BF16, BF16, F32, BF16, BF16, BF16, BF16)
    return pl.pallas_call(
        functools.partial(_proj_kernel, d=d),
        grid=(n // tm,),
        in_specs=[row(d),
                  pl.BlockSpec((1, N_ADA, d), lambda i: (i // tiles_per_batch, 0, 0)),
                  _const_spec((1, d)), _const_spec(w_all.shape),
                  _const_spec(gcq.shape), _const_spec(wq.shape), _const_spec(wqr.shape),
                  _const_spec(gckv.shape), _const_spec(wk.shape), _const_spec(wv.shape),
                  row(1), _const_spec((1, LANES))],
        out_specs=[row(w) for w in out_widths],
        out_shape=[jax.ShapeDtypeStruct((n, w), dt) for w, dt in zip(out_widths, out_dtypes)],
        compiler_params=pltpu.CompilerParams(vmem_limit_bytes=VMEM_LIMIT),
        name="proj",
    )(x1, mod3, g, w_all, gcq, wq, wqr, gckv, wk, wv, pos_col, freq_row)


DSA_HEAD_GROUP = 4
DSA_AHEAD = 0
DSA_SLOTS = (DSA_AHEAD + 1) * DSA_HEAD_GROUP
ZERO_BAND = 1 << 20
GROUPS = 256
COUNT_UNKNOWN = 1 << 30
FINISH_STEPS = 2


def _rowsum8(x):
    v = x.reshape(x.shape[0] // SUBLANES, SUBLANES, x.shape[1])
    while v.shape[0] > 1:
        half = v.shape[0] // 2
        v = v[:half] + v[half:]
    return v[0]


def _t5_bias_tiles(posk, posq, tab_ref, heads):
    return _t5_bias_of_rel(posk - posq, tab_ref, heads)


def _t5_bias_of_rel(rel, tab_ref, heads):
    half = NUM_BUCKETS // 2
    max_exact = half // 2
    n = jnp.abs(rel)
    nf = jnp.maximum(n, 1).astype(F32)
    large = max_exact + (jnp.log(nf / max_exact) / math.log(MAX_DISTANCE / max_exact)
                         * (half - max_exact)).astype(I32)
    large = jnp.minimum(large, half - 1)
    bucket = jnp.where(rel > 0, half, 0) + jnp.where(n < max_exact, n, large)
    tk, tq = bucket.shape
    out = []
    for h in range(heads):
        tab = jnp.broadcast_to(tab_ref[h:h + 1, :], (tk, LANES))
        cols = [jnp.take_along_axis(tab, bucket[:, c * LANES:(c + 1) * LANES], axis=1,
                                    mode="promise_in_bounds")
                for c in range(tq // LANES)]
        out.append(jnp.concatenate(cols, axis=1))
    return out


def _dsa_kernel(pmin_ref, pmax_ref, prun_ref,
                qa_ref, qi_ref, wi_ref, sm_ref, vat_ref, posq_ref, posk_ref, tab_ref, far_ref,
                o_ref,
                keys_ref, gmax_ref, m_ref, acc_ref, s_ref, *, tq, topk, nq):
    b = pl.program_id(0)
    i = pl.program_id(1)
    tk = tq
    n_tiles = i + 1

    w_t = wi_ref[...].T[0:IDX_HEADS, :]
    key_l = lax.broadcasted_iota(I32, (tk, tq), 0)
    q_l = lax.broadcasted_iota(I32, (tk, tq), 1)
    diag_ok = key_l < (q_l // CHUNK + 1) * CHUNK
    not_key_l = ~key_l

    def score_tile(j, diag):
        r0 = pl.multiple_of(j * tk, tk)
        ki_lo = sm_ref[pl.ds(r0, tk), 256:384]
        ki_hi = sm_ref[pl.ds(r0, tk), 384:512]
        score = None
        for p in range(IDX_HEADS // 2):
            qp = qi_ref[:, p * LANES:(p + 1) * LANES]
            for h, kk in ((2 * p, ki_lo), (2 * p + 1, ki_hi)):
                t = w_t[h:h + 1, :] * jnp.maximum(_dot_nt(kk, qp), 0.0)
                score = t if score is None else score + t
        score = score + 0.0
        bits = lax.bitcast_convert_type(score, I32)
        key = jnp.where(bits < 0, (bits ^ 0x7FFFFFFF) - ZERO_BAND, bits)
        key = jnp.where(score == 0.0, not_key_l - j * tk, key)
        if diag:
            key = jnp.where(diag_ok, key, INT_MIN)
        keys_ref[pl.ds(r0, tk), :] = key
        gmax_ref[...] = jnp.maximum(gmax_ref[...],
                                    jnp.max(key.reshape(tk // GROUPS, GROUPS, tq), axis=0))

    gmax_ref[...] = jnp.full(gmax_ref.shape, INT_MIN, I32)
    lax.fori_loop(0, i, lambda j, c: (score_tile(j, False), c)[1], 0)
    score_tile(i, True)

    def count_tiles(pred):
        def body(j, acc):
            r0 = pl.multiple_of(j * tk, tk)
            return acc + _rowsum8(jnp.where(pred(keys_ref[pl.ds(r0, tk), :], j), 1, 0))
        acc = lax.fori_loop(0, n_tiles, body, jnp.zeros((SUBLANES, tq), I32))
        return jnp.sum(acc, axis=0, keepdims=True)

    q_row = lax.broadcasted_iota(I32, (1, tq), 1)
    n_adm = i * tq + (q_row // CHUNK + 1) * CHUNK

    gmax = gmax_ref[...]
    top = jnp.max(gmax, axis=0, keepdims=True)
    low = jnp.min(gmax, axis=0, keepdims=True)
    cnt_pos = count_tiles(lambda k_, j: k_ >= 1)
    cnt_zero = count_tiles(lambda k_, j: k_ >= -ZERO_BAND)
    few = n_adm <= topk
    in_pos = cnt_pos >= topk
    in_zero = jnp.logical_not(in_pos) & (cnt_zero >= topk)
    base_lo = jnp.where(in_pos, 1, jnp.where(in_zero, -ZERO_BAND, INT_MIN + 1))
    base_cnt = jnp.where(in_pos, cnt_pos, jnp.where(in_zero, cnt_zero, n_adm))
    hi0 = jnp.where(in_pos, top + 1, jnp.where(in_zero, 1, -ZERO_BAND))
    tighter = low > base_lo
    lo0 = jnp.where(few, INT_MIN + 1, jnp.where(tighter, low, base_lo))
    cnt0 = jnp.where(few, topk, jnp.where(tighter, COUNT_UNKNOWN, base_cnt))
    hi0 = jnp.where(few, INT_MIN + 2, hi0)

    def any_row(pred):
        return jnp.max(jnp.where(pred, 1, 0))

    def bisect(lo, hi, cnt_lo, slack):
        def pending(lo, hi, cnt_lo):
            return any_row((cnt_lo - topk > slack) & (hi - lo > 1))

        def step(st):
            it, lo, hi, cnt_lo, _ = st
            mid = lo + ((hi - lo) >> 1)
            cnt = count_tiles(lambda k_, j: k_ >= mid)
            take = cnt >= topk
            lo = jnp.where(take, mid, lo)
            hi = jnp.where(take, hi, mid)
            cnt_lo = jnp.where(take, cnt, cnt_lo)
            return it + 1, lo, hi, cnt_lo, pending(lo, hi, cnt_lo)

        return lax.while_loop(lambda st: (st[0] < 34) & (st[4] > 0), step,
                              (jnp.int32(0), lo, hi, cnt_lo, pending(lo, hi, cnt_lo)))

    _, lo1, hi1, cnt1, _ = bisect(lo0, hi0, cnt0, FINISH_STEPS)

    def surplus(lo, cnt_lo):
        return (cnt_lo > topk) & (cnt_lo - topk <= FINISH_STEPS) & (hi1 - lo > 1)

    def drop_step(st):
        it, lo, cnt_lo, _ = st
        def body(j, acc):
            r0 = pl.multiple_of(j * tk, tk)
            k_ = keys_ref[pl.ds(r0, tk), :]
            cand = jnp.where(k_ >= lo, k_, INT_MAX)
            v = cand.reshape(tk // SUBLANES, SUBLANES, tq)
            while v.shape[0] > 1:
                half = v.shape[0] // 2
                v = jnp.minimum(v[:half], v[half:])
            return jnp.minimum(acc, v[0])
        acc = lax.fori_loop(0, n_tiles, body, jnp.full((SUBLANES, tq), INT_MAX, I32))
        smallest = jnp.min(acc, axis=0, keepdims=True)
        go = surplus(lo, cnt_lo)
        lo = jnp.where(go, smallest + 1, lo)
        cnt_lo = jnp.where(go, cnt_lo - 1, cnt_lo)
        return it + 1, lo, cnt_lo, any_row(surplus(lo, cnt_lo))

    _, lo2, _, _ = lax.while_loop(lambda st: (st[0] < FINISH_STEPS) & (st[3] > 0), drop_step,
                                  (jnp.int32(0), lo1, cnt1, any_row(surplus(lo1, cnt1))))

    cnt2 = count_tiles(lambda k_, j: k_ >= lo2)
    redo = (cnt2 != topk) & jnp.logical_not(few)
    its, tau, _, _, _ = bisect(jnp.where(redo, lo1, lo2), jnp.where(redo, hi1, lo2 + 1),
                               jnp.where(redo, cnt1, topk), 0)
    cnt_ge = lax.cond(its > 0, lambda: count_tiles(lambda k_, j: k_ >= tau), lambda: cnt2)
    over = (cnt_ge > topk) & jnp.logical_not(few)
    tau_sel = tau

    @pl.when(jnp.max(jnp.where(over, 1, 0)) > 0)
    def _ties():
        cnt_gt = count_tiles(lambda k_, j: k_ > tau)
        room = topk - cnt_gt

        def idx_step(p, jmax):
            cand = jmax | jnp.left_shift(jnp.int32(1), 14 - p)
            cnt = count_tiles(lambda k_, j: (k_ == tau) & (key_l + j * tk < cand))
            return jnp.where(cnt <= room, cand, jmax)

        jmax = lax.fori_loop(0, 15, idx_step, jnp.zeros((1, tq), I32))

        def demote(j, carry):
            r0 = pl.multiple_of(j * tk, tk)
            k_ = keys_ref[pl.ds(r0, tk), :]
            drop = (k_ == tau) & (key_l + j * tk >= jmax) & over
            keys_ref[pl.ds(r0, tk), :] = jnp.where(drop, k_ - 1, k_)
            return carry

        lax.fori_loop(0, n_tiles, demote, 0)

    m_ref[...] = jnp.full(m_ref.shape, NEG_BIG, F32)
    acc_ref[...] = jnp.zeros(acc_ref.shape, F32)
    posq = posq_ref[0]

    def attn_tile(j, mode):
        near = mode != "far"
        r0 = pl.multiple_of(j * tk, tk)
        ka_lo = sm_ref[pl.ds(r0, tk), 0:128]
        ka_hi = sm_ref[pl.ds(r0, tk), 128:256]
        va_t = vat_ref[0:PV_ROWS, pl.ds(r0, tk)]
        maskb = jnp.where(keys_ref[pl.ds(r0, tk), :] >= tau_sel, 0.0, NEG_BIG)
        if mode == "any":
            posk_row = lax.bitcast_convert_type(posk_ref[0, :, pl.ds(r0, tk)], F32)
            posk_col = lax.bitcast_convert_type(
                jnp.broadcast_to(posk_row, (LANES, tk)).T, I32)
            bias = _t5_bias_tiles(jnp.tile(posk_col, (1, tq // LANES)), posq, tab_ref, A_HEADS)
        elif mode == "run":
            nb = tq // LANES
            shift_max = (nb - 1) * LANES
            d0 = pmin_ref[b * nq + j] - pmin_ref[b * nq + i]
            x = lax.broadcasted_iota(I32, (tk + shift_max, LANES), 0)
            l = lax.broadcasted_iota(I32, (tk + shift_max, LANES), 1)
            panel = _t5_bias_of_rel(x - l + (d0 - shift_max), tab_ref, A_HEADS)
            bias = [jnp.concatenate([p[shift_max - cb * LANES:shift_max - cb * LANES + tk]
                                     for cb in range(nb)], axis=1) for p in panel]
        m_prev = m_ref[...]
        m_next = [None] * A_HEADS
        slot = lambda h: h % DSA_SLOTS

        def logits(g):
            for h in range(g * DSA_HEAD_GROUP, (g + 1) * DSA_HEAD_GROUP):
                s = _dot_nt(ka_hi if h % 2 else ka_lo,
                            qa_ref[:, (h // 2) * LANES:(h // 2 + 1) * LANES])
                s = s + ((bias[h] + maskb) if near else maskb)
                s_ref[slot(h)] = s
                s_max = jnp.max(s, axis=0, keepdims=True)
                m_next[h] = jnp.maximum(m_prev[h:h + 1, :],
                                        s_max if near else s_max + far_ref[h])

        def weighted_values(g):
            for h in range(g * DSA_HEAD_GROUP, (g + 1) * DSA_HEAD_GROUP):
                alpha = jnp.exp2(m_prev[h:h + 1, :] - m_next[h])
                shift = m_next[h] if near else m_next[h] - far_ref[h]
                pr = jnp.exp2(s_ref[slot(h)] - shift).astype(BF16)
                acc_ref[h, 0:PV_ROWS, :] = alpha * acc_ref[h, 0:PV_ROWS, :] + _dot(va_t, pr)

        _two_phase(A_HEADS // DSA_HEAD_GROUP, logits, weighted_values, ahead=DSA_AHEAD)
        m_ref[...] = jnp.concatenate(m_next, axis=0)

    def attn_body(j, carry):
        is_far = pmax_ref[b * nq + j] - pmin_ref[b * nq + i] <= -MAX_DISTANCE
        is_run = (prun_ref[b * nq + j] == 1) & (prun_ref[b * nq + i] == 1)
        lax.cond(is_far, lambda: attn_tile(j, "far"),
                 lambda: lax.cond(is_run, lambda: attn_tile(j, "run"),
                                  lambda: attn_tile(j, "any")))
        return carry

    lax.fori_loop(0, n_tiles, attn_body, 0)

    for h in range(A_HEADS):
        a = acc_ref[h]
        o_ref[:, h * LANES:(h + 1) * LANES] = (a / a[V_DIM:V_DIM + 1, :]).T.astype(BF16)


def _dsa(qa, qi, wi, sm, vat, posq, posk, tab, far, pmin, pmax, prun, batch, seq, tq, topk):
    n = qa.shape[0]
    nq = seq // tq
    row = lambda w: pl.BlockSpec((tq, w), lambda b, i, *_: (b * nq + i, 0))
    grid_spec = pltpu.PrefetchScalarGridSpec(
        num_scalar_prefetch=3,
        grid=(batch, nq),
        in_specs=[row(512), row(512), row(LANES),
                  pl.BlockSpec((seq, 512), lambda b, i, *_: (b, 0),
                               pipeline_mode=pl.Buffered(1)),
                  pl.BlockSpec((LANES, seq), lambda b, i, *_: (0, b)),
                  pl.BlockSpec((1, 1, tq), lambda b, i, *_: (b * nq + i, 0, 0)),
                  pl.BlockSpec((1, 1, seq), lambda b, i, *_: (b, 0, 0)),
                  pl.BlockSpec((A_HEADS, LANES), lambda b, i, *_: (0, 0)),
                  pl.BlockSpec(memory_space=pltpu.SMEM)],
        out_specs=row(A_HEADS * LANES),
        scratch_shapes=[pltpu.VMEM((seq, tq), I32),
                        pltpu.VMEM((GROUPS, tq), I32),
                        pltpu.VMEM((A_HEADS, tq), F32),
                        pltpu.VMEM((A_HEADS, LANES, tq), F32),
                        pltpu.VMEM((DSA_SLOTS, tq, tq), F32)],
    )
    return pl.pallas_call(
        functools.partial(_dsa_kernel, tq=tq, topk=topk, nq=nq),
        grid_spec=grid_spec,
        out_shape=jax.ShapeDtypeStruct((n, A_HEADS * LANES), BF16),
        compiler_params=pltpu.CompilerParams(vmem_limit_bytes=VMEM_LIMIT),
        name="dsa",
    )(pmin, pmax, prun, qa, qi, wi, sm, vat, posq, posk, tab, far)


MLA_HEAD_GROUP = 8


def _mla_kernel(q_ref, k_ref, vt_ref, o_ref, acc_ref, s_ref, *, tq):
    i = pl.program_id(2)
    tk = tq
    hg = MLA_HEAD_GROUP
    acc_ref[...] = jnp.zeros(acc_ref.shape, F32)

    def tile(j, ms, diag):
        r0 = pl.multiple_of(j * tk, tk)
        if diag:
            key_l = lax.broadcasted_iota(I32, (tk, tq), 0)
            q_l = lax.broadcasted_iota(I32, (tk, tq), 1)
            ok = key_l < (q_l // CHUNK + 1) * CHUNK
        head = lambda h: slice(h * LANES, (h + 1) * LANES)
        out = []
        for h in range(hg):
            s = _dot_nt(k_ref[pl.ds(r0, tk), head(h)], q_ref[:, head(h)])
            if diag:
                s = jnp.where(ok, s, NEG_BIG)
            s_ref[h] = s
            out.append(jnp.maximum(ms[h], jnp.max(s, axis=0, keepdims=True)))
        for h in range(hg):
            alpha = jnp.exp2(ms[h] - out[h])
            pr = jnp.exp2(s_ref[h] - out[h]).astype(BF16)
            v_t = vt_ref[h * LANES:h * LANES + PV_ROWS, pl.ds(r0, tk)]
            acc_ref[h, 0:PV_ROWS, :] = alpha * acc_ref[h, 0:PV_ROWS, :] + _dot(v_t, pr)
        return tuple(out)

    m0 = tuple(jnp.full((1, tq), NEG_BIG, F32) for _ in range(hg))
    ms = lax.fori_loop(0, i, lambda j, m_: tile(j, m_, False), m0)
    tile(i, ms, True)
    for h in range(hg):
        a = acc_ref[h]
        o_ref[:, h * LANES:(h + 1) * LANES] = (a / a[V_DIM:V_DIM + 1, :]).T.astype(BF16)


def _mla(qb, kb, vbt, batch, seq, tq):
    n = qb.shape[0]
    nq = seq // tq
    hg = MLA_HEAD_GROUP
    ng = B_HEADS // hg
    w = hg * LANES
    return pl.pallas_call(
        functools.partial(_mla_kernel, tq=tq),
        grid=(batch, ng, nq),
        in_specs=[pl.BlockSpec((tq, w), lambda b, g, i: (b * nq + i, g)),
                  pl.BlockSpec((seq, w), lambda b, g, i: (b, g), pipeline_mode=pl.Buffered(1)),
                  pl.BlockSpec((w, seq), lambda b, g, i: (b * ng + g, 0),
                               pipeline_mode=pl.Buffered(1))],
        out_specs=pl.BlockSpec((tq, w), lambda b, g, i: (b * nq + i, g)),
        out_shape=jax.ShapeDtypeStruct((n, B_HEADS * LANES), BF16),
        scratch_shapes=[pltpu.VMEM((hg, LANES, tq), F32), pltpu.VMEM((hg, tq, tq), F32)],
        compiler_params=pltpu.CompilerParams(vmem_limit_bytes=VMEM_LIMIT),
        name="mla",
    )(qb, kb, vbt)


def _tail_kernel(x_ref, oa_ref, ob_ref, gt_ref, mod_ref, woa_ref, wob_ref, wout_ref,
                 g2_ref, win_ref, wdn_ref, gf_ref, o_ref, *, d, d_ff, chunk):
    x = x_ref[...]
    gt2 = mod_ref[0, 5:6, :]
    sh3, sc3, gt3 = mod_ref[0, 6:7, :], mod_ref[0, 7:8, :], mod_ref[0, 8:9, :]
    ya = _dot(oa_ref[...], woa_ref[...])
    yb = _dot(ob_ref[...], wob_ref[...])
    y = gt_ref[:, 0:d].astype(F32) * ya + gt_ref[:, d:2 * d].astype(F32) * yb
    x = x + gt2 * _dot(y.astype(BF16), wout_ref[...])
    h = _rms(x, g2_ref[...]) * (1.0 + sc3) + sh3
    f = _swiglu_tile(h.astype(BF16), win_ref, wdn_ref, d_ff, chunk)
    x = x + (0.5 * gt3) * f
    o_ref[...] = _rms(x, gf_ref[...])


def _tail(x1, oa, ob, gates, mod3, woa, wob, wout, g2, w_in, w_down, gf, seq, tm):
    n, d = x1.shape
    d_ff = w_down.shape[0]
    tiles_per_batch = seq // tm
    row = lambda w: pl.BlockSpec((tm, w), lambda i: (i, 0))
    kern = functools.partial(_tail_kernel, d=d, d_ff=d_ff, chunk=_ffn_chunk(d_ff))
    return pl.pallas_call(
        kern,
        grid=(n // tm,),
        in_specs=[row(d), row(oa.shape[1]), row(ob.shape[1]), row(2 * d),
                  pl.BlockSpec((1, N_ADA, d), lambda i: (i // tiles_per_batch, 0, 0)),
                  _const_spec(woa.shape), _const_spec(wob.shape), _const_spec(wout.shape),
                  _const_spec((1, d)), _const_spec(w_in.shape), _const_spec(w_down.shape),
                  _const_spec((1, d))],
        out_specs=row(d),
        out_shape=jax.ShapeDtypeStruct((n, d), F32),
        compiler_params=pltpu.CompilerParams(vmem_limit_bytes=VMEM_LIMIT),
        name="tail",
    )(x1, oa, ob, gates, mod3, woa, wob, wout, g2, w_in, w_down, gf)


def _pad_heads(w, heads, width):
    k = w.shape[0]
    w3 = w.reshape(k, heads, width)
    return jnp.pad(w3, ((0, 0), (0, 0), (0, LANES - width))).reshape(k, heads * LANES)


def _pad_head_rows(w, heads, width):
    dd = w.shape[1]
    w3 = w.reshape(heads, width, dd)
    return jnp.pad(w3, ((0, 0), (0, LANES - width), (0, 0))).reshape(heads * LANES, dd)


def _layout_w_in(w_in, d):
    z = lambda c: jnp.zeros((d, c), w_in.dtype)
    o = 0
    qa = w_in[:, o:o + 512] * (A_HEAD_DIM ** -0.5 * LOG2E); o += 512
    ka = w_in[:, o:o + 64]; o += 64
    va = w_in[:, o:o + 64]; o += 64
    qi = w_in[:, o:o + 512]; o += 512
    ki = w_in[:, o:o + 64]; o += 64
    wi = w_in[:, o:o + IDX_HEADS]; o += IDX_HEADS
    cq = w_in[:, o:o + Q_LORA]; o += Q_LORA
    ckv = w_in[:, o:o + KV_LORA]; o += KV_LORA
    kr = w_in[:, o:o + QK_ROPE]; o += QK_ROPE
    gates = w_in[:, o:o + 2 * d]
    hr = QK_ROPE // 2
    kt = jnp.concatenate([-kr[:, hr:], kr[:, :hr]], axis=1)
    cols = [qa, qi,
            ka, z(64), z(64), ka, va, z(64), ki, z(64), z(64), ki,
            wi, z(LANES - IDX_HEADS),
            z(64), kr, z(32), z(64), kt, z(32),
            cq, ckv, gates]
    return jnp.concatenate(cols, axis=1).astype(BF16)


def _layout_w_uq(w_uq):
    scale = (QK_NOPE + QK_ROPE) ** -0.5 * LOG2E
    dh = QK_NOPE + QK_ROPE
    hr = QK_ROPE // 2
    src_plain = np.zeros(B_HEADS * LANES, np.int32)
    mul_plain = np.zeros(B_HEADS * LANES, np.float32)
    src_rot = np.zeros(B_HEADS * LANES, np.int32)
    mul_rot = np.zeros(B_HEADS * LANES, np.float32)
    for h in range(B_HEADS):
        for c in range(dh):
            src_plain[h * LANES + c] = h * dh + c
            mul_plain[h * LANES + c] = scale
        for c in range(QK_ROPE):
            first = c < hr
            src_rot[h * LANES + QK_NOPE + c] = h * dh + QK_NOPE + (c + hr if first else c - hr)
            mul_rot[h * LANES + QK_NOPE + c] = -scale if first else scale
    plain = jnp.take(w_uq, src_plain, axis=1) * mul_plain
    rotw = jnp.take(w_uq, src_rot, axis=1) * mul_rot
    return plain.astype(BF16), rotw.astype(BF16)


def _rope_freq_row():
    half = QK_ROPE // 2
    freqs = ROPE_THETA ** (-2.0 * jnp.arange(half, dtype=F32) / QK_ROPE)
    return jnp.concatenate([jnp.zeros((QK_NOPE,), F32), freqs, freqs,
                            jnp.zeros((LANES - QK_NOPE - QK_ROPE,), F32)]).reshape(1, LANES)


def _tiles(seq):
    tm = 512 if seq % 512 == 0 else 256
    tq_dsa = 512 if seq % 512 == 0 else 256
    tq_mla = 512 if seq % 512 == 0 else 256
    return tm, tq_dsa, tq_mla


def kernel(x, c, positions, w_ada, b_ada, g_ffn1, w_ffn1_in, w_ffn1_down, g_mix, w_in,
           g_cq, w_uq, g_ckv, w_uk, w_uv, rel_bias, w_o_a, w_o_b, w_out,
           g_ffn2, w_ffn2_in, w_ffn2_down, g_final):
    batch, seq, d = x.shape
    depth = w_ada.shape[0]
    n = batch * seq
    topk = min(TOPK_MAX, seq // 4)
    tm, tq_dsa, tq_mla = _tiles(seq)
    assert seq % tq_dsa == 0 and seq % tm == 0 and d % LANES == 0 and seq <= ZERO_BAND
    assert topk <= GROUPS and tq_dsa % GROUPS == 0
    assert depth == 1, "the tail kernel fuses the final norm into the single layer"

    xf = x.reshape(n, d)
    c_pad = jnp.pad(c, ((0, 8 - batch % 8 if batch % 8 else 0), (0, 0)))
    pos = positions.astype(I32)
    posq = pos.reshape(n // tq_dsa, 1, tq_dsa)
    posk = pos.reshape(batch, 1, seq)
    ptile = pos.reshape(n // tq_dsa, tq_dsa)
    pmin, pmax = ptile.min(axis=1), ptile.max(axis=1)
    prun = jnp.all(ptile[:, 1:] - ptile[:, :-1] == 1, axis=1).astype(I32)
    tab = jnp.pad(rel_bias.T.astype(F32) * LOG2E, ((0, 0), (0, LANES - NUM_BUCKETS)))
    far = rel_bias[NUM_BUCKETS // 2 - 1].astype(F32) * LOG2E

    l = 0
    mod = _ada(c_pad, w_ada[l], b_ada[l])[:batch]
    mod3 = mod.reshape(batch, N_ADA, d)
    x1 = _ffn1(xf, mod3, g_ffn1[l].reshape(1, d), w_ffn1_in[l].astype(BF16),
               w_ffn1_down[l].astype(BF16), seq, tm)
    wq, wqr = _layout_w_uq(w_uq[l])
    qa, qi, sm, va, wi, qb, kb, vb, gates = _proj(
        x1, mod3, g_mix[l].reshape(1, d), _layout_w_in(w_in[l], d),
        g_cq[l].reshape(1, Q_LORA), wq, wqr, g_ckv[l].reshape(1, KV_LORA),
        _pad_heads(w_uk[l], B_HEADS, QK_NOPE).astype(BF16),
        _pad_heads(w_uv[l], B_HEADS, V_DIM).astype(BF16),
        pos.reshape(n, 1), _rope_freq_row(), seq, tm)
    vat = va.T
    vbt = vb.reshape(batch, seq, B_HEADS * LANES).transpose(0, 2, 1).reshape(-1, seq)
    oa = _dsa(qa, qi, wi, sm, vat, posq, posk, tab, far, pmin, pmax, prun, batch, seq, tq_dsa,
              topk)
    ob = _mla(qb, kb, vbt, batch, seq, tq_mla)
    out = _tail(x1, oa, ob, gates, mod3,
                _pad_head_rows(w_o_a[l], A_HEADS, A_HEAD_DIM).astype(BF16),
                _pad_head_rows(w_o_b[l], B_HEADS, V_DIM).astype(BF16),
                w_out[l].astype(BF16), g_ffn2[l].reshape(1, d),
                w_ffn2_in[l].astype(BF16), w_ffn2_down[l].astype(BF16),
                g_final.reshape(1, d), seq, tm)
    return out.reshape(batch, seq, d)
```

```python
import functools
import math

import jax
import jax.numpy as jnp
import numpy as np
from jax import lax
from jax.experimental import pallas as pl
from jax.experimental.pallas import tpu as pltpu

F32 = jnp.float32
BF16 = jnp.bfloat16
I32 = jnp.int32

CHUNK = 64
EPS = 1e-6
A_HEADS = 8
A_HEAD_DIM = 64
IDX_HEADS = 8
IDX_DIM = 64
TOPK_MAX = 256
B_HEADS = 8
QK_NOPE = 64
QK_ROPE = 32
V_DIM = 64
Q_LORA = 384
KV_LORA = 256
ROPE_THETA = 10000.0
NUM_BUCKETS = 32
MAX_DISTANCE = 128
N_ADA = 9

LANES = 128
SUBLANES = 8
VMEM_LIMIT = 56 * 1024 * 1024
INT_MIN = -2 ** 31
INT_MAX = 2 ** 31 - 1
NEG_BIG = -1e30
LOG2E = math.log2(math.e)
BF16_ROWS = 16
PV_ROWS = -(-(V_DIM + 1) // BF16_ROWS) * BF16_ROWS

C_QA = 0
C_QI = 512
C_SM = 1024
C_WI = 1664
C_CQ = 2048
C_CKV = C_CQ + Q_LORA
C_GT = C_CKV + KV_LORA


def _dot(a, b):
    return jnp.dot(a, b, preferred_element_type=F32)


def _dot_nt(a, b):
    return lax.dot_general(a, b, (((1,), (1,)), ((), ())), preferred_element_type=F32)


def _dot_tn(a, b):
    return lax.dot_general(a, b, (((0,), (0,)), ((), ())), preferred_element_type=F32)


def _two_phase(n_groups, phase1, phase2, ahead=1):
    for g in range(min(ahead, n_groups)):
        phase1(g)
    for g in range(n_groups):
        if g + ahead < n_groups:
            phase1(g + ahead)
        phase2(g)


def _rms(x, g):
    return x * lax.rsqrt(jnp.mean(x * x, axis=-1, keepdims=True) + EPS) * g


def _const_spec(shape):
    nd = len(shape)
    return pl.BlockSpec(shape, lambda *_: (0,) * nd, pipeline_mode=pl.Buffered(1))


def _ada_kernel(c_ref, w_ref, b_ref, o_ref):
    c = c_ref[...]
    s = c * jax.nn.sigmoid(c)
    o_ref[...] = _dot(s.astype(BF16), w_ref[...].astype(BF16)) + b_ref[...]


def _ada(c_pad, w_ada, b_ada):
    d = c_pad.shape[1]
    n_out = w_ada.shape[1]
    return pl.pallas_call(
        _ada_kernel,
        grid=(n_out // d,),
        in_specs=[pl.BlockSpec(c_pad.shape, lambda j: (0, 0)),
                  pl.BlockSpec((d, d), lambda j: (0, j)),
                  pl.BlockSpec((1, d), lambda j: (0, j))],
        out_specs=pl.BlockSpec(c_pad.shape, lambda j: (0, j)),
        out_shape=jax.ShapeDtypeStruct((c_pad.shape[0], n_out), F32),
        compiler_params=pltpu.CompilerParams(vmem_limit_bytes=VMEM_LIMIT),
        name="ada",
    )(c_pad, w_ada, b_ada.reshape(1, n_out))


def _swiglu_tile(hb, win_ref, wdn_ref, d_ff, chunk):
    acc = None
    for j in range(d_ff // chunk):
        g = _dot(hb, win_ref[:, j * chunk:(j + 1) * chunk])
        u = _dot(hb, win_ref[:, d_ff + j * chunk:d_ff + (j + 1) * chunk])
        a = (g * jax.nn.sigmoid(g) * u).astype(BF16)
        part = _dot(a, wdn_ref[j * chunk:(j + 1) * chunk, :])
        acc = part if acc is None else acc + part
    return acc


def _ffn_chunk(d_ff):
    for c in (512, 256, 128):
        if d_ff % c == 0:
            return c
    raise ValueError("d_ff must be a multiple of 128")


def _ffn1_kernel(x_ref, mod_ref, g_ref, win_ref, wdn_ref, o_ref, *, d_ff, chunk):
    x = x_ref[...]
    sh, sc, gt = mod_ref[0, 0:1, :], mod_ref[0, 1:2, :], mod_ref[0, 2:3, :]
    h = _rms(x, g_ref[...]) * (1.0 + sc) + sh
    y = _swiglu_tile(h.astype(BF16), win_ref, wdn_ref, d_ff, chunk)
    o_ref[...] = x + (0.5 * gt) * y


def _ffn1(x2, mod3, g, w_in, w_down, seq, tm):
    n, d = x2.shape
    d_ff = w_down.shape[0]
    tiles_per_batch = seq // tm
    kern = functools.partial(_ffn1_kernel, d_ff=d_ff, chunk=_ffn_chunk(d_ff))
    return pl.pallas_call(
        kern,
        grid=(n // tm,),
        in_specs=[pl.BlockSpec((tm, d), lambda i: (i, 0)),
                  pl.BlockSpec((1, N_ADA, d), lambda i: (i // tiles_per_batch, 0, 0)),
                  _const_spec((1, d)),
                  _const_spec(w_in.shape),
                  _const_spec(w_down.shape)],
        out_specs=pl.BlockSpec((tm, d), lambda i: (i, 0)),
        out_shape=jax.ShapeDtypeStruct((n, d), F32),
        compiler_params=pltpu.CompilerParams(vmem_limit_bytes=VMEM_LIMIT),
        name="ffn1",
    )(x2, mod3, g, w_in, w_down)


def _proj_kernel(x_ref, mod_ref, g_ref, w_ref, gcq_ref, wq_ref, wqr_ref, gckv_ref,
                 wk_ref, wv_ref, pos_ref, freq_ref,
                 qa_ref, qi_ref, sm_ref, va_ref, wi_ref, qb_ref, kb_ref, vb_ref, gt_ref, *, d):
    x = x_ref[...]
    sh, sc = mod_ref[0, 3:4, :], mod_ref[0, 4:5, :]
    hb = (_rms(x, g_ref[...]) * (1.0 + sc) + sh).astype(BF16)
    lane = lax.broadcasted_iota(I32, (1, LANES), 1)
    one_at_v = (lane == V_DIM).astype(F32)
    tm = x.shape[0]
    half = QK_ROPE // 2
    ang = pos_ref[...].astype(F32) * freq_ref[...]

    def spread(dense):
        rows = jnp.broadcast_to(dense[:, None, :], (tm // SUBLANES, SUBLANES, LANES))
        top = pltpu.roll(rows.reshape(tm, LANES), 0, 1, stride=half, stride_axis=0)
        first = pltpu.roll(top, QK_NOPE + half, 1)
        return first, pltpu.roll(first, half, 1)

    in_first = (lane >= QK_NOPE) & (lane < QK_NOPE + half)
    in_second = (lane >= QK_NOPE + half) & (lane < QK_NOPE + QK_ROPE)
    c1, c2 = spread(jnp.cos(ang))
    s1, s2 = spread(jnp.sin(ang))
    cs = jnp.where(lane < QK_NOPE, 1.0, jnp.where(in_first, c1, jnp.where(in_second, c2, 0.0)))
    sn = jnp.where(in_first, s1, jnp.where(in_second, s2, 0.0))

    qa_ref[...] = _dot(hb, w_ref[:, C_QA:C_QA + 512]).astype(BF16)
    qi_ref[...] = _dot(hb, w_ref[:, C_QI:C_QI + 512]).astype(BF16)

    sm = _dot(hb, w_ref[:, C_SM:C_SM + 640])
    sm_ref[:, 0:256] = sm[:, 0:256].astype(BF16)
    sm_ref[:, 256:512] = sm[:, 384:640].astype(BF16)
    va_ref[...] = (sm[:, 256:384] + one_at_v).astype(BF16)

    misc = _dot(hb, w_ref[:, C_WI:C_WI + 384])
    wi_ref[...] = misc[:, 0:128]
    kpe = misc[:, 128:256] * cs + misc[:, 256:384] * sn

    lat = _dot(hb, w_ref[:, C_CQ:C_GT])
    cq = _rms(lat[:, 0:Q_LORA], gcq_ref[...]).astype(BF16)
    ckv = _rms(lat[:, Q_LORA:Q_LORA + KV_LORA], gckv_ref[...]).astype(BF16)
    qn = _dot(cq, wq_ref[...])
    qr = _dot(cq, wqr_ref[...])
    kn = _dot(ckv, wk_ref[...])
    vv = _dot(ckv, wv_ref[...])
    for h in range(B_HEADS):
        sl = slice(h * LANES, (h + 1) * LANES)
        qb_ref[:, sl] = (qn[:, sl] * cs + qr[:, sl] * sn).astype(BF16)
        kb_ref[:, sl] = (kn[:, sl] + kpe).astype(BF16)
        vb_ref[:, sl] = (vv[:, sl] + one_at_v).astype(BF16)

    gl = _dot(hb, w_ref[:, C_GT:C_GT + 2 * d])
    gt_ref[...] = jax.nn.sigmoid(gl).astype(BF16)


def _proj(x1, mod3, g, w_all, gcq, wq, wqr, gckv, wk, wv, pos_col, freq_row, seq, tm):
    n, d = x1.shape
    tiles_per_batch = seq // tm
    row = lambda w: pl.BlockSpec((tm, w), lambda i: (i, 0))
    out_widths = (512, 512, 512, 128, 128, 1024, 1024, 1024, 2 * d)
    out_dtypes = (BF16, BF16, BF16, BF16, F32, BF16, BF16, BF16, BF16)
    return pl.pallas_call(
        functools.partial(_proj_kernel, d=d),
        grid=(n // tm,),
        in_specs=[row(d),
                  pl.BlockSpec((1, N_ADA, d), lambda i: (i // tiles_per_batch, 0, 0)),
                  _const_spec((1, d)), _const_spec(w_all.shape),
                  _const_spec(gcq.shape), _const_spec(wq.shape), _const_spec(wqr.shape),
                  _const_spec(gckv.shape), _const_spec(wk.shape), _const_spec(wv.shape),
                  pl.BlockSpec((tm // SUBLANES, LANES), lambda i: (i, 0)),
                  _const_spec((1, LANES))],
        out_specs=[row(w) for w in out_widths],
        out_shape=[jax.ShapeDtypeStruct((n, w), dt) for w, dt in zip(out_widths, out_dtypes)],
        compiler_params=pltpu.CompilerParams(vmem_limit_bytes=VMEM_LIMIT),
        name="proj",
    )(x1, mod3, g, w_all, gcq, wq, wqr, gckv, wk, wv, pos_col, freq_row)


DSA_HEAD_GROUP = 4
DSA_AHEAD = 0
DSA_SLOTS = (DSA_AHEAD + 1) * DSA_HEAD_GROUP
ZERO_BAND = 1 << 20
GROUPS = 256
COUNT_UNKNOWN = 1 << 30
FINISH_STEPS = 2


def _rowsum8(x):
    v = x.reshape(x.shape[0] // SUBLANES, SUBLANES, x.shape[1])
    while v.shape[0] > 1:
        half = v.shape[0] // 2
        v = v[:half] + v[half:]
    return v[0]


def _t5_bias_tiles(posk, posq, tab_ref, heads):
    return _t5_bias_of_rel(posk - posq, tab_ref, heads)


def _t5_bias_of_rel(rel, tab_ref, heads):
    half = NUM_BUCKETS // 2
    max_exact = half // 2
    n = jnp.abs(rel)
    nf = jnp.maximum(n, 1).astype(F32)
    large = max_exact + (jnp.log(nf / max_exact) / math.log(MAX_DISTANCE / max_exact)
                         * (half - max_exact)).astype(I32)
    large = jnp.minimum(large, half - 1)
    bucket = jnp.where(rel > 0, half, 0) + jnp.where(n < max_exact, n, large)
    tk, tq = bucket.shape
    out = []
    for h in range(heads):
        tab = jnp.broadcast_to(tab_ref[h:h + 1, :], (tk, LANES))
        cols = [jnp.take_along_axis(tab, bucket[:, c * LANES:(c + 1) * LANES], axis=1,
                                    mode="promise_in_bounds")
                for c in range(tq // LANES)]
        out.append(jnp.concatenate(cols, axis=1))
    return out


def _dsa_kernel(pmin_ref, pmax_ref, prun_ref,
                qa_ref, qi_ref, wi_ref, sm_ref, vat_ref, posq_ref, posk_ref, tab_ref, far_ref,
                o_ref,
                keys_ref, gmax_ref, m_ref, acc_ref, s_ref, *, tq, topk, nq):
    b = pl.program_id(0)
    i = pl.program_id(1)
    tk = tq
    n_tiles = i + 1

    w_t = wi_ref[...].T[0:IDX_HEADS, :]
    key_l = lax.broadcasted_iota(I32, (tk, tq), 0)
    q_l = lax.broadcasted_iota(I32, (tk, tq), 1)
    diag_ok = key_l < (q_l // CHUNK + 1) * CHUNK
    not_key_l = ~key_l

    def score_tile(j, diag):
        r0 = pl.multiple_of(j * tk, tk)
        ki_lo = sm_ref[pl.ds(r0, tk), 256:384]
        ki_hi = sm_ref[pl.ds(r0, tk), 384:512]
        score = None
        for p in range(IDX_HEADS // 2):
            qp = qi_ref[:, p * LANES:(p + 1) * LANES]
            for h, kk in ((2 * p, ki_lo), (2 * p + 1, ki_hi)):
                t = w_t[h:h + 1, :] * jnp.maximum(_dot_nt(kk, qp), 0.0)
                score = t if score is None else score + t
        score = score + 0.0
        bits = lax.bitcast_convert_type(score, I32)
        key = jnp.where(bits < 0, (bits ^ 0x7FFFFFFF) - ZERO_BAND, bits)
        key = jnp.where(score == 0.0, not_key_l - j * tk, key)
        if diag:
            key = jnp.where(diag_ok, key, INT_MIN)
        keys_ref[pl.ds(r0, tk), :] = key
        gmax_ref[...] = jnp.maximum(gmax_ref[...],
                                    jnp.max(key.reshape(tk // GROUPS, GROUPS, tq), axis=0))

    gmax_ref[...] = jnp.full(gmax_ref.shape, INT_MIN, I32)
    lax.fori_loop(0, i, lambda j, c: (score_tile(j, False), c)[1], 0)
    score_tile(i, True)

    def count_tiles(pred):
        def body(j, acc):
            r0 = pl.multiple_of(j * tk, tk)
            return acc + _rowsum8(jnp.where(pred(keys_ref[pl.ds(r0, tk), :], j), 1, 0))
        acc = lax.fori_loop(0, n_tiles, body, jnp.zeros((SUBLANES, tq), I32))
        return jnp.sum(acc, axis=0, keepdims=True)

    q_row = lax.broadcasted_iota(I32, (1, tq), 1)
    n_adm = i * tq + (q_row // CHUNK + 1) * CHUNK

    gmax = gmax_ref[...]
    top = jnp.max(gmax, axis=0, keepdims=True)
    low = jnp.min(gmax, axis=0, keepdims=True)
    cnt_pos = count_tiles(lambda k_, j: k_ >= 1)
    cnt_zero = count_tiles(lambda k_, j: k_ >= -ZERO_BAND)
    few = n_adm <= topk
    in_pos = cnt_pos >= topk
    in_zero = jnp.logical_not(in_pos) & (cnt_zero >= topk)
    base_lo = jnp.where(in_pos, 1, jnp.where(in_zero, -ZERO_BAND, INT_MIN + 1))
    base_cnt = jnp.where(in_pos, cnt_pos, jnp.where(in_zero, cnt_zero, n_adm))
    hi0 = jnp.where(in_pos, top + 1, jnp.where(in_zero, 1, -ZERO_BAND))
    tighter = low > base_lo
    lo0 = jnp.where(few, INT_MIN + 1, jnp.where(tighter, low, base_lo))
    cnt0 = jnp.where(few, topk, jnp.where(tighter, COUNT_UNKNOWN, base_cnt))
    hi0 = jnp.where(few, INT_MIN + 2, hi0)

    def any_row(pred):
        return jnp.max(jnp.where(pred, 1, 0))

    def bisect(lo, hi, cnt_lo, slack):
        def pending(lo, hi, cnt_lo):
            return any_row((cnt_lo - topk > slack) & (hi - lo > 1))

        def step(st):
            it, lo, hi, cnt_lo, _ = st
            mid = lo + ((hi - lo) >> 1)
            cnt = count_tiles(lambda k_, j: k_ >= mid)
            take = cnt >= topk
            lo = jnp.where(take, mid, lo)
            hi = jnp.where(take, hi, mid)
            cnt_lo = jnp.where(take, cnt, cnt_lo)
            return it + 1, lo, hi, cnt_lo, pending(lo, hi, cnt_lo)

        return lax.while_loop(lambda st: (st[0] < 34) & (st[4] > 0), step,
                              (jnp.int32(0), lo, hi, cnt_lo, pending(lo, hi, cnt_lo)))

    _, lo1, hi1, cnt1, _ = bisect(lo0, hi0, cnt0, FINISH_STEPS)

    def surplus(lo, cnt_lo):
        return (cnt_lo > topk) & (cnt_lo - topk <= FINISH_STEPS) & (hi1 - lo > 1)

    def drop_step(st):
        it, lo, cnt_lo, _ = st
        def body(j, acc):
            r0 = pl.multiple_of(j * tk, tk)
            k_ = keys_ref[pl.ds(r0, tk), :]
            cand = jnp.where(k_ >= lo, k_, INT_MAX)
            v = cand.reshape(tk // SUBLANES, SUBLANES, tq)
            while v.shape[0] > 1:
                half = v.shape[0] // 2
                v = jnp.minimum(v[:half], v[half:])
            return jnp.minimum(acc, v[0])
        acc = lax.fori_loop(0, n_tiles, body, jnp.full((SUBLANES, tq), INT_MAX, I32))
        smallest = jnp.min(acc, axis=0, keepdims=True)
        go = surplus(lo, cnt_lo)
        lo = jnp.where(go, smallest + 1, lo)
        cnt_lo = jnp.where(go, cnt_lo - 1, cnt_lo)
        return it + 1, lo, cnt_lo, any_row(surplus(lo, cnt_lo))

    _, lo2, _, _ = lax.while_loop(lambda st: (st[0] < FINISH_STEPS) & (st[3] > 0), drop_step,
                                  (jnp.int32(0), lo1, cnt1, any_row(surplus(lo1, cnt1))))

    cnt2 = count_tiles(lambda k_, j: k_ >= lo2)
    redo = (cnt2 != topk) & jnp.logical_not(few)
    its, tau, _, _, _ = bisect(jnp.where(redo, lo1, lo2), jnp.where(redo, hi1, lo2 + 1),
                               jnp.where(redo, cnt1, topk), 0)
    cnt_ge = lax.cond(its > 0, lambda: count_tiles(lambda k_, j: k_ >= tau), lambda: cnt2)
    over = (cnt_ge > topk) & jnp.logical_not(few)
    tau_sel = tau

    @pl.when(jnp.max(jnp.where(over, 1, 0)) > 0)
    def _ties():
        cnt_gt = count_tiles(lambda k_, j: k_ > tau)
        room = topk - cnt_gt

        def idx_step(p, jmax):
            cand = jmax | jnp.left_shift(jnp.int32(1), 14 - p)
            cnt = count_tiles(lambda k_, j: (k_ == tau) & (key_l + j * tk < cand))
            return jnp.where(cnt <= room, cand, jmax)

        jmax = lax.fori_loop(0, 15, idx_step, jnp.zeros((1, tq), I32))

        def demote(j, carry):
            r0 = pl.multiple_of(j * tk, tk)
            k_ = keys_ref[pl.ds(r0, tk), :]
            drop = (k_ == tau) & (key_l + j * tk >= jmax) & over
            keys_ref[pl.ds(r0, tk), :] = jnp.where(drop, k_ - 1, k_)
            return carry

        lax.fori_loop(0, n_tiles, demote, 0)

    m_ref[...] = jnp.full(m_ref.shape, NEG_BIG, F32)
    acc_ref[...] = jnp.zeros(acc_ref.shape, F32)
    posq = posq_ref[0]

    def attn_tile(j, mode):
        near = mode != "far"
        r0 = pl.multiple_of(j * tk, tk)
        ka_lo = sm_ref[pl.ds(r0, tk), 0:128]
        ka_hi = sm_ref[pl.ds(r0, tk), 128:256]
        va_t = vat_ref[0:PV_ROWS, pl.ds(r0, tk)]
        maskb = jnp.where(keys_ref[pl.ds(r0, tk), :] >= tau_sel, 0.0, NEG_BIG)
        if mode == "any":
            posk_row = lax.bitcast_convert_type(posk_ref[0, :, pl.ds(r0, tk)], F32)
            posk_col = lax.bitcast_convert_type(
                jnp.broadcast_to(posk_row, (LANES, tk)).T, I32)
            bias = _t5_bias_tiles(jnp.tile(posk_col, (1, tq // LANES)), posq, tab_ref, A_HEADS)
        elif mode == "run":
            nb = tq // LANES
            shift_max = (nb - 1) * LANES
            d0 = pmin_ref[b * nq + j] - pmin_ref[b * nq + i]
            x = lax.broadcasted_iota(I32, (tk + shift_max, LANES), 0)
            l = lax.broadcasted_iota(I32, (tk + shift_max, LANES), 1)
            panel = _t5_bias_of_rel(x - l + (d0 - shift_max), tab_ref, A_HEADS)
            bias = [jnp.concatenate([p[shift_max - cb * LANES:shift_max - cb * LANES + tk]
                                     for cb in range(nb)], axis=1) for p in panel]
        m_prev = m_ref[...]
        m_next = [None] * A_HEADS
        slot = lambda h: h % DSA_SLOTS

        def logits(g):
            for h in range(g * DSA_HEAD_GROUP, (g + 1) * DSA_HEAD_GROUP):
                s = _dot_nt(ka_hi if h % 2 else ka_lo,
                            qa_ref[:, (h // 2) * LANES:(h // 2 + 1) * LANES])
                s = s + ((bias[h] + maskb) if near else maskb)
                s_ref[slot(h)] = s
                s_max = jnp.max(s, axis=0, keepdims=True)
                m_next[h] = jnp.maximum(m_prev[h:h + 1, :],
                                        s_max if near else s_max + far_ref[h])

        def weighted_values(g):
            for h in range(g * DSA_HEAD_GROUP, (g + 1) * DSA_HEAD_GROUP):
                alpha = jnp.exp2(m_prev[h:h + 1, :] - m_next[h])
                shift = m_next[h] if near else m_next[h] - far_ref[h]
                pr = jnp.exp2(s_ref[slot(h)] - shift).astype(BF16)
                acc_ref[h, 0:PV_ROWS, :] = alpha * acc_ref[h, 0:PV_ROWS, :] + _dot(va_t, pr)

        _two_phase(A_HEADS // DSA_HEAD_GROUP, logits, weighted_values, ahead=DSA_AHEAD)
        m_ref[...] = jnp.concatenate(m_next, axis=0)

    def attn_body(j, carry):
        is_far = pmax_ref[b * nq + j] - pmin_ref[b * nq + i] <= -MAX_DISTANCE
        is_run = (prun_ref[b * nq + j] == 1) & (prun_ref[b * nq + i] == 1)
        lax.cond(is_far, lambda: attn_tile(j, "far"),
                 lambda: lax.cond(is_run, lambda: attn_tile(j, "run"),
                                  lambda: attn_tile(j, "any")))
        return carry

    lax.fori_loop(0, n_tiles, attn_body, 0)

    for h in range(A_HEADS):
        a = acc_ref[h]
        o_ref[:, h * LANES:(h + 1) * LANES] = (a / a[V_DIM:V_DIM + 1, :]).T.astype(BF16)


def _dsa(qa, qi, wi, sm, vat, posq, posk, tab, far, pmin, pmax, prun, batch, seq, tq, topk):
    n = qa.shape[0]
    nq = seq // tq
    row = lambda w: pl.BlockSpec((tq, w), lambda b, i, *_: (b * nq + i, 0))
    grid_spec = pltpu.PrefetchScalarGridSpec(
        num_scalar_prefetch=3,
        grid=(batch, nq),
        in_specs=[row(512), row(512), row(LANES),
                  pl.BlockSpec((seq, 512), lambda b, i, *_: (b, 0),
                               pipeline_mode=pl.Buffered(1)),
                  pl.BlockSpec((LANES, seq), lambda b, i, *_: (0, b)),
                  pl.BlockSpec((1, 1, tq), lambda b, i, *_: (b * nq + i, 0, 0)),
                  pl.BlockSpec((1, 1, seq), lambda b, i, *_: (b, 0, 0)),
                  pl.BlockSpec((A_HEADS, LANES), lambda b, i, *_: (0, 0)),
                  pl.BlockSpec(memory_space=pltpu.SMEM)],
        out_specs=row(A_HEADS * LANES),
        scratch_shapes=[pltpu.VMEM((seq, tq), I32),
                        pltpu.VMEM((GROUPS, tq), I32),
                        pltpu.VMEM((A_HEADS, tq), F32),
                        pltpu.VMEM((A_HEADS, LANES, tq), F32),
                        pltpu.VMEM((DSA_SLOTS, tq, tq), F32)],
    )
    return pl.pallas_call(
        functools.partial(_dsa_kernel, tq=tq, topk=topk, nq=nq),
        grid_spec=grid_spec,
        out_shape=jax.ShapeDtypeStruct((n, A_HEADS * LANES), BF16),
        compiler_params=pltpu.CompilerParams(vmem_limit_bytes=VMEM_LIMIT),
        name="dsa",
    )(pmin, pmax, prun, qa, qi, wi, sm, vat, posq, posk, tab, far)


MLA_HEAD_GROUP = 8
MLA_SUBGROUP = 8
MLA_AHEAD = 0


def _mla_kernel(q_ref, k_ref, vt_ref, o_ref, acc_ref, s_ref, *, tq):
    i = pl.program_id(2)
    tk = tq
    hg = MLA_HEAD_GROUP
    acc_ref[...] = jnp.zeros(acc_ref.shape, F32)

    def tile(j, ms, diag):
        r0 = pl.multiple_of(j * tk, tk)
        if diag:
            key_l = lax.broadcasted_iota(I32, (tk, tq), 0)
            q_l = lax.broadcasted_iota(I32, (tk, tq), 1)
            ok = key_l < (q_l // CHUNK + 1) * CHUNK
        head = lambda h: slice(h * LANES, (h + 1) * LANES)
        out = [None] * hg
        sub = MLA_SUBGROUP

        def logits(g):
            for h in range(g * sub, (g + 1) * sub):
                s = _dot_nt(k_ref[pl.ds(r0, tk), head(h)], q_ref[:, head(h)])
                if diag:
                    s = jnp.where(ok, s, NEG_BIG)
                s_ref[h] = s
                out[h] = jnp.maximum(ms[h], jnp.max(s, axis=0, keepdims=True))

        def weighted_values(g):
            for h in range(g * sub, (g + 1) * sub):
                alpha = jnp.exp2(ms[h] - out[h])
                pr = jnp.exp2(s_ref[h] - out[h]).astype(BF16)
                v_t = vt_ref[h * LANES:h * LANES + PV_ROWS, pl.ds(r0, tk)]
                acc_ref[h, 0:PV_ROWS, :] = alpha * acc_ref[h, 0:PV_ROWS, :] + _dot(v_t, pr)

        _two_phase(hg // sub, logits, weighted_values, ahead=MLA_AHEAD)
        return tuple(out)

    m0 = tuple(jnp.full((1, tq), NEG_BIG, F32) for _ in range(hg))
    ms = lax.fori_loop(0, i, lambda j, m_: tile(j, m_, False), m0)
    tile(i, ms, True)
    for h in range(hg):
        a = acc_ref[h]
        o_ref[:, h * LANES:(h + 1) * LANES] = (a / a[V_DIM:V_DIM + 1, :]).T.astype(BF16)


def _mla(qb, kb, vbt, batch, seq, tq):
    n = qb.shape[0]
    nq = seq // tq
    hg = MLA_HEAD_GROUP
    ng = B_HEADS // hg
    w = hg * LANES
    return pl.pallas_call(
        functools.partial(_mla_kernel, tq=tq),
        grid=(batch, ng, nq),
        in_specs=[pl.BlockSpec((tq, w), lambda b, g, i: (b * nq + i, g)),
                  pl.BlockSpec((seq, w), lambda b, g, i: (b, g), pipeline_mode=pl.Buffered(1)),
                  pl.BlockSpec((w, seq), lambda b, g, i: (b * ng + g, 0),
                               pipeline_mode=pl.Buffered(1))],
        out_specs=pl.BlockSpec((tq, w), lambda b, g, i: (b * nq + i, g)),
        out_shape=jax.ShapeDtypeStruct((n, B_HEADS * LANES), BF16),
        scratch_shapes=[pltpu.VMEM((hg, LANES, tq), F32), pltpu.VMEM((hg, tq, tq), F32)],
        compiler_params=pltpu.CompilerParams(vmem_limit_bytes=VMEM_LIMIT),
        name="mla",
    )(qb, kb, vbt)


def _tail_kernel(x_ref, oa_ref, ob_ref, gt_ref, mod_ref, woa_ref, wob_ref, wout_ref,
                 g2_ref, win_ref, wdn_ref, gf_ref, o_ref, *, d, d_ff, chunk):
    x = x_ref[...]
    gt2 = mod_ref[0, 5:6, :]
    sh3, sc3, gt3 = mod_ref[0, 6:7, :], mod_ref[0, 7:8, :], mod_ref[0, 8:9, :]
    ya = _dot(oa_ref[...], woa_ref[...])
    yb = _dot(ob_ref[...], wob_ref[...])
    y = gt_ref[:, 0:d].astype(F32) * ya + gt_ref[:, d:2 * d].astype(F32) * yb
    x = x + gt2 * _dot(y.astype(BF16), wout_ref[...])
    h = _rms(x, g2_ref[...]) * (1.0 + sc3) + sh3
    f = _swiglu_tile(h.astype(BF16), win_ref, wdn_ref, d_ff, chunk)
    x = x + (0.5 * gt3) * f
    o_ref[...] = _rms(x, gf_ref[...])


def _tail(x1, oa, ob, gates, mod3, woa, wob, wout, g2, w_in, w_down, gf, seq, tm):
    n, d = x1.shape
    d_ff = w_down.shape[0]
    tiles_per_batch = seq // tm
    row = lambda w: pl.BlockSpec((tm, w), lambda i: (i, 0))
    kern = functools.partial(_tail_kernel, d=d, d_ff=d_ff, chunk=_ffn_chunk(d_ff))
    return pl.pallas_call(
        kern,
        grid=(n // tm,),
        in_specs=[row(d), row(oa.shape[1]), row(ob.shape[1]), row(2 * d),
                  pl.BlockSpec((1, N_ADA, d), lambda i: (i // tiles_per_batch, 0, 0)),
                  _const_spec(woa.shape), _const_spec(wob.shape), _const_spec(wout.shape),
                  _const_spec((1, d)), _const_spec(w_in.shape), _const_spec(w_down.shape),
                  _const_spec((1, d))],
        out_specs=row(d),
        out_shape=jax.ShapeDtypeStruct((n, d), F32),
        compiler_params=pltpu.CompilerParams(vmem_limit_bytes=VMEM_LIMIT),
        name="tail",
    )(x1, oa, ob, gates, mod3, woa, wob, wout, g2, w_in, w_down, gf)


def _pad_heads(w, heads, width):
    k = w.shape[0]
    w3 = w.reshape(k, heads, width)
    return jnp.pad(w3, ((0, 0), (0, 0), (0, LANES - width))).reshape(k, heads * LANES)


def _pad_head_rows(w, heads, width):
    dd = w.shape[1]
    w3 = w.reshape(heads, width, dd)
    return jnp.pad(w3, ((0, 0), (0, LANES - width), (0, 0))).reshape(heads * LANES, dd)


def _layout_w_in(w_in, d):
    z = lambda c: jnp.zeros((d, c), w_in.dtype)
    o = 0
    qa = w_in[:, o:o + 512] * (A_HEAD_DIM ** -0.5 * LOG2E); o += 512
    ka = w_in[:, o:o + 64]; o += 64
    va = w_in[:, o:o + 64]; o += 64
    qi = w_in[:, o:o + 512]; o += 512
    ki = w_in[:, o:o + 64]; o += 64
    wi = w_in[:, o:o + IDX_HEADS]; o += IDX_HEADS
    cq = w_in[:, o:o + Q_LORA]; o += Q_LORA
    ckv = w_in[:, o:o + KV_LORA]; o += KV_LORA
    kr = w_in[:, o:o + QK_ROPE]; o += QK_ROPE
    gates = w_in[:, o:o + 2 * d]
    hr = QK_ROPE // 2
    kt = jnp.concatenate([-kr[:, hr:], kr[:, :hr]], axis=1)
    cols = [qa, qi,
            ka, z(64), z(64), ka, va, z(64), ki, z(64), z(64), ki,
            wi, z(LANES - IDX_HEADS),
            z(64), kr, z(32), z(64), kt, z(32),
            cq, ckv, gates]
    return jnp.concatenate(cols, axis=1).astype(BF16)


def _layout_w_uq(w_uq):
    scale = (QK_NOPE + QK_ROPE) ** -0.5 * LOG2E
    dh = QK_NOPE + QK_ROPE
    hr = QK_ROPE // 2
    src_plain = np.zeros(B_HEADS * LANES, np.int32)
    mul_plain = np.zeros(B_HEADS * LANES, np.float32)
    src_rot = np.zeros(B_HEADS * LANES, np.int32)
    mul_rot = np.zeros(B_HEADS * LANES, np.float32)
    for h in range(B_HEADS):
        for c in range(dh):
            src_plain[h * LANES + c] = h * dh + c
            mul_plain[h * LANES + c] = scale
        for c in range(QK_ROPE):
            first = c < hr
            src_rot[h * LANES + QK_NOPE + c] = h * dh + QK_NOPE + (c + hr if first else c - hr)
            mul_rot[h * LANES + QK_NOPE + c] = -scale if first else scale
    plain = jnp.take(w_uq, src_plain, axis=1) * mul_plain
    rotw = jnp.take(w_uq, src_rot, axis=1) * mul_rot
    return plain.astype(BF16), rotw.astype(BF16)


def _rope_dense(pos):
    half = QK_ROPE // 2
    assert SUBLANES * half == LANES
    freqs = ROPE_THETA ** (-2.0 * jnp.arange(half, dtype=F32) / QK_ROPE)
    pos_dense = jnp.repeat(pos.reshape(-1, SUBLANES, 1)[:, ::-1, :], half,
                           axis=2).reshape(-1, LANES)
    return pos_dense, jnp.tile(freqs, SUBLANES).reshape(1, LANES)


def _tiles(seq):
    tm = 512 if seq % 512 == 0 else 256
    tq_dsa = 512 if seq % 512 == 0 else 256
    tq_mla = 512 if seq % 512 == 0 else 256
    return tm, tq_dsa, tq_mla


def kernel(x, c, positions, w_ada, b_ada, g_ffn1, w_ffn1_in, w_ffn1_down, g_mix, w_in,
           g_cq, w_uq, g_ckv, w_uk, w_uv, rel_bias, w_o_a, w_o_b, w_out,
           g_ffn2, w_ffn2_in, w_ffn2_down, g_final):
    batch, seq, d = x.shape
    depth = w_ada.shape[0]
    n = batch * seq
    topk = min(TOPK_MAX, seq // 4)
    tm, tq_dsa, tq_mla = _tiles(seq)
    assert seq % tq_dsa == 0 and seq % tm == 0 and d % LANES == 0 and seq <= ZERO_BAND
    assert topk <= GROUPS and tq_dsa % GROUPS == 0
    assert depth == 1, "the tail kernel fuses the final norm into the single layer"

    xf = x.reshape(n, d)
    c_pad = jnp.pad(c, ((0, 8 - batch % 8 if batch % 8 else 0), (0, 0)))
    pos = positions.astype(I32)
    posq = pos.reshape(n // tq_dsa, 1, tq_dsa)
    posk = pos.reshape(batch, 1, seq)
    ptile = pos.reshape(n // tq_dsa, tq_dsa)
    pmin, pmax = ptile.min(axis=1), ptile.max(axis=1)
    prun = jnp.all(ptile[:, 1:] - ptile[:, :-1] == 1, axis=1).astype(I32)
    tab = jnp.pad(rel_bias.T.astype(F32) * LOG2E, ((0, 0), (0, LANES - NUM_BUCKETS)))
    far = rel_bias[NUM_BUCKETS // 2 - 1].astype(F32) * LOG2E

    l = 0
    mod = _ada(c_pad, w_ada[l], b_ada[l])[:batch]
    mod3 = mod.reshape(batch, N_ADA, d)
    x1 = _ffn1(xf, mod3, g_ffn1[l].reshape(1, d), w_ffn1_in[l].astype(BF16),
               w_ffn1_down[l].astype(BF16), seq, tm)
    wq, wqr = _layout_w_uq(w_uq[l])
    qa, qi, sm, va, wi, qb, kb, vb, gates = _proj(
        x1, mod3, g_mix[l].reshape(1, d), _layout_w_in(w_in[l], d),
        g_cq[l].reshape(1, Q_LORA), wq, wqr, g_ckv[l].reshape(1, KV_LORA),
        _pad_heads(w_uk[l], B_HEADS, QK_NOPE).astype(BF16),
        _pad_heads(w_uv[l], B_HEADS, V_DIM).astype(BF16),
        *_rope_dense(pos), seq, tm)
    vat = va.T
    vbt = vb.reshape(batch, seq, B_HEADS * LANES).transpose(0, 2, 1).reshape(-1, seq)
    oa = _dsa(qa, qi, wi, sm, vat, posq, posk, tab, far, pmin, pmax, prun, batch, seq, tq_dsa,
              topk)
    ob = _mla(qb, kb, vbt, batch, seq, tq_mla)
    out = _tail(x1, oa, ob, gates, mod3,
                _pad_head_rows(w_o_a[l], A_HEADS, A_HEAD_DIM).astype(BF16),
                _pad_head_rows(w_o_b[l], B_HEADS, V_DIM).astype(BF16),
                w_out[l].astype(BF16), g_ffn2[l].reshape(1, d),
                w_ffn2_in[l].astype(BF16), w_ffn2_down[l].astype(BF16),
                g_final.reshape(1, d), seq, tm)
    return out.reshape(batch, seq, d)
```

```python
import functools
import math

import jax
import jax.numpy as jnp
import numpy as np
from jax import lax
from jax.experimental import pallas as pl
from jax.experimental.pallas import tpu as pltpu

F32 = jnp.float32
BF16 = jnp.bfloat16
I32 = jnp.int32

CHUNK = 64
EPS = 1e-6
A_HEADS = 8
A_HEAD_DIM = 64
IDX_HEADS = 8
IDX_DIM = 64
TOPK_MAX = 256
B_HEADS = 8
QK_NOPE = 64
QK_ROPE = 32
V_DIM = 64
Q_LORA = 384
KV_LORA = 256
ROPE_THETA = 10000.0
NUM_BUCKETS = 32
MAX_DISTANCE = 128
N_ADA = 9

LANES = 128
SUBLANES = 8
VMEM_LIMIT = 56 * 1024 * 1024
INT_MIN = -2 ** 31
INT_MAX = 2 ** 31 - 1
NEG_BIG = -1e30
LOG2E = math.log2(math.e)
F8 = jnp.float8_e4m3fn
P_SHIFT = 8.0
F8_MAX = 448.0
F8_FIT = 224.0
F8_ROWS = 32
PV_ROWS_F8 = -(-(V_DIM + 1) // F8_ROWS) * F8_ROWS

A_Q_W = A_HEADS * A_HEAD_DIM
I_Q_W = IDX_HEADS * IDX_DIM
PAIR_W = 2 * LANES
SM_W = 5 * LANES
MISC_W = 3 * LANES
C_QA = 0
C_QI = C_QA + A_Q_W
C_SM = C_QI + I_Q_W
C_WI = C_SM + SM_W
C_CQ = C_WI + MISC_W
C_CKV = C_CQ + Q_LORA
C_GT = C_CKV + KV_LORA


def _dot(a, b):
    return jnp.dot(a, b, preferred_element_type=F32)


def _dot_nt(a, b):
    return lax.dot_general(a, b, (((1,), (1,)), ((), ())), preferred_element_type=F32)


def _two_phase(n_groups, phase1, phase2, ahead=1):
    for g in range(min(ahead, n_groups)):
        phase1(g)
    for g in range(n_groups):
        if g + ahead < n_groups:
            phase1(g + ahead)
        phase2(g)


def _fit_queries(q, scale):
    return jnp.clip(q.astype(F32) * scale, -F8_MAX, F8_MAX).astype(F8)


def _store_heads(acc_ref, o_ref, heads, unscale):
    def normalised(h):
        a = acc_ref[h]
        return (a * (unscale / a[V_DIM:V_DIM + 1, :])).T[:, 0:V_DIM]

    for h in range(0, heads, 2):
        pair = jnp.concatenate([normalised(h), normalised(h + 1)], axis=1)
        o_ref[:, h * V_DIM:(h + 2) * V_DIM] = pair.astype(BF16)


def _rms(x, g):
    return x * lax.rsqrt(jnp.mean(x * x, axis=-1, keepdims=True) + EPS) * g


def _const_spec(shape):
    nd = len(shape)
    return pl.BlockSpec(shape, lambda *_: (0,) * nd, pipeline_mode=pl.Buffered(1))


def _ada_kernel(c_ref, w_ref, b_ref, o_ref):
    c = c_ref[...]
    s = c * jax.nn.sigmoid(c)
    o_ref[...] = _dot(s.astype(BF16), w_ref[...].astype(BF16)) + b_ref[...]


def _ada(c_pad, w_ada, b_ada):
    d = c_pad.shape[1]
    n_out = w_ada.shape[1]
    return pl.pallas_call(
        _ada_kernel,
        grid=(n_out // d,),
        in_specs=[pl.BlockSpec(c_pad.shape, lambda j: (0, 0)),
                  pl.BlockSpec((d, d), lambda j: (0, j)),
                  pl.BlockSpec((1, d), lambda j: (0, j))],
        out_specs=pl.BlockSpec(c_pad.shape, lambda j: (0, j)),
        out_shape=jax.ShapeDtypeStruct((c_pad.shape[0], n_out), F32),
        compiler_params=pltpu.CompilerParams(vmem_limit_bytes=VMEM_LIMIT),
        name="ada",
    )(c_pad, w_ada, b_ada.reshape(1, n_out))


def _swiglu_tile(hb, win_ref, wdn_ref, d_ff, chunk):
    acc = None
    for j in range(d_ff // chunk):
        g = _dot(hb, win_ref[:, j * chunk:(j + 1) * chunk])
        u = _dot(hb, win_ref[:, d_ff + j * chunk:d_ff + (j + 1) * chunk])
        a = (g * jax.nn.sigmoid(g) * u).astype(BF16)
        part = _dot(a, wdn_ref[j * chunk:(j + 1) * chunk, :])
        acc = part if acc is None else acc + part
    return acc


def _ffn_chunk(d_ff):
    for c in (512, 256, 128):
        if d_ff % c == 0:
            return c
    raise ValueError("d_ff must be a multiple of 128")


def _ffn1_kernel(x_ref, mod_ref, g_ref, win_ref, wdn_ref, o_ref, *, d_ff, chunk):
    x = x_ref[...]
    sh, sc, gt = mod_ref[0, 0:1, :], mod_ref[0, 1:2, :], mod_ref[0, 2:3, :]
    h = _rms(x, g_ref[...]) * (1.0 + sc) + sh
    y = _swiglu_tile(h.astype(BF16), win_ref, wdn_ref, d_ff, chunk)
    o_ref[...] = x + (0.5 * gt) * y


def _ffn1(x2, mod3, g, w_in, w_down, seq, tm):
    n, d = x2.shape
    d_ff = w_down.shape[0]
    tiles_per_batch = seq // tm
    kern = functools.partial(_ffn1_kernel, d_ff=d_ff, chunk=_ffn_chunk(d_ff))
    return pl.pallas_call(
        kern,
        grid=(n // tm,),
        in_specs=[pl.BlockSpec((tm, d), lambda i: (i, 0)),
                  pl.BlockSpec((1, N_ADA, d), lambda i: (i // tiles_per_batch, 0, 0)),
                  _const_spec((1, d)),
                  _const_spec(w_in.shape),
                  _const_spec(w_down.shape)],
        out_specs=pl.BlockSpec((tm, d), lambda i: (i, 0)),
        out_shape=jax.ShapeDtypeStruct((n, d), F32),
        compiler_params=pltpu.CompilerParams(vmem_limit_bytes=VMEM_LIMIT),
        name="ffn1",
    )(x2, mod3, g, w_in, w_down)


def _proj_kernel(x_ref, mod_ref, g_ref, w_ref, gcq_ref, wq_ref, wqr_ref, gckv_ref,
                 wk_ref, wv_ref, pos_ref, freq_ref,
                 qa_ref, qi_ref, ska_ref, ski_ref, va_ref, wi_ref, qb_ref, kb_ref, vb_ref, gt_ref,
                 *, d):
    x = x_ref[...]
    sh, sc = mod_ref[0, 3:4, :], mod_ref[0, 4:5, :]
    hb = (_rms(x, g_ref[...]) * (1.0 + sc) + sh).astype(BF16)
    lane = lax.broadcasted_iota(I32, (1, LANES), 1)
    tm = x.shape[0]
    half = QK_ROPE // 2
    ang = pos_ref[...].astype(F32) * freq_ref[...]

    def spread(dense):
        rows = jnp.broadcast_to(dense[:, None, :], (tm // SUBLANES, SUBLANES, LANES))
        top = pltpu.roll(rows.reshape(tm, LANES), 0, 1, stride=half, stride_axis=0)
        first = pltpu.roll(top, QK_NOPE + half, 1)
        return first, pltpu.roll(first, half, 1)

    in_first = (lane >= QK_NOPE) & (lane < QK_NOPE + half)
    in_second = (lane >= QK_NOPE + half) & (lane < QK_NOPE + QK_ROPE)
    c1, c2 = spread(jnp.cos(ang))
    s1, s2 = spread(jnp.sin(ang))
    cs = jnp.where(lane < QK_NOPE, 1.0, jnp.where(in_first, c1, jnp.where(in_second, c2, 0.0)))
    sn = jnp.where(in_first, s1, jnp.where(in_second, s2, 0.0))

    qa_ref[...] = _dot(hb, w_ref[:, C_QA:C_QA + A_Q_W]).astype(BF16)
    qi_ref[...] = _dot(hb, w_ref[:, C_QI:C_QI + I_Q_W]).astype(BF16)

    sm = _dot(hb, w_ref[:, C_SM:C_SM + SM_W])
    ska_ref[...] = sm[:, 0:PAIR_W].astype(BF16)
    va_ref[...] = sm[:, PAIR_W:PAIR_W + LANES].astype(BF16)
    ski_ref[...] = sm[:, PAIR_W + LANES:SM_W].astype(BF16)

    misc = _dot(hb, w_ref[:, C_WI:C_WI + MISC_W])
    wi_ref[...] = misc[:, 0:LANES]
    kpe = misc[:, LANES:2 * LANES] * cs + misc[:, 2 * LANES:3 * LANES] * sn

    lat = _dot(hb, w_ref[:, C_CQ:C_GT])
    cq = _rms(lat[:, 0:Q_LORA], gcq_ref[...]).astype(BF16)
    ckv = _rms(lat[:, Q_LORA:Q_LORA + KV_LORA], gckv_ref[...]).astype(BF16)
    qn = _dot(cq, wq_ref[...])
    qr = _dot(cq, wqr_ref[...])
    kn = _dot(ckv, wk_ref[...])
    vv = _dot(ckv, wv_ref[...])
    for h in range(B_HEADS):
        sl = slice(h * LANES, (h + 1) * LANES)
        qb_ref[:, sl] = (qn[:, sl] * cs + qr[:, sl] * sn).astype(BF16)
        kb_ref[:, sl] = (kn[:, sl] + kpe).astype(BF16)
    vb_ref[...] = vv.astype(BF16)

    gl = _dot(hb, w_ref[:, C_GT:C_GT + 2 * d])
    gt_ref[...] = jax.nn.sigmoid(gl).astype(BF16)


def _proj(x1, mod3, g, w_all, gcq, wq, wqr, gckv, wk, wv, pos_col, freq_row, seq, tm):
    n, d = x1.shape
    tiles_per_batch = seq // tm
    row = lambda w: pl.BlockSpec((tm, w), lambda i: (i, 0))
    slots = B_HEADS * LANES
    out_widths = (A_Q_W, I_Q_W, PAIR_W, PAIR_W, LANES, LANES, slots, slots, slots, 2 * d)
    out_dtypes = (BF16, BF16, BF16, BF16, BF16, F32, BF16, BF16, BF16, BF16)
    return pl.pallas_call(
        functools.partial(_proj_kernel, d=d),
        grid=(n // tm,),
        in_specs=[row(d),
                  pl.BlockSpec((1, N_ADA, d), lambda i: (i // tiles_per_batch, 0, 0)),
                  _const_spec((1, d)), _const_spec(w_all.shape),
                  _const_spec(gcq.shape), _const_spec(wq.shape), _const_spec(wqr.shape),
                  _const_spec(gckv.shape), _const_spec(wk.shape), _const_spec(wv.shape),
                  pl.BlockSpec((tm // SUBLANES, LANES), lambda i: (i, 0)),
                  _const_spec((1, LANES))],
        out_specs=[row(w) for w in out_widths],
        out_shape=[jax.ShapeDtypeStruct((n, w), dt) for w, dt in zip(out_widths, out_dtypes)],
        compiler_params=pltpu.CompilerParams(vmem_limit_bytes=VMEM_LIMIT),
        name="proj",
    )(x1, mod3, g, w_all, gcq, wq, wqr, gckv, wk, wv, pos_col, freq_row)


DSA_HEAD_GROUP = 2
DSA_AHEAD = 1
DSA_SLOTS = (DSA_AHEAD + 1) * DSA_HEAD_GROUP
ZERO_BAND = 1 << 20
GROUPS = 256
COUNT_UNKNOWN = 1 << 30
FINISH_STEPS = 2


def _rowsum8(x):
    v = x.reshape(x.shape[0] // SUBLANES, SUBLANES, x.shape[1])
    while v.shape[0] > 1:
        half = v.shape[0] // 2
        v = v[:half] + v[half:]
    return v[0]


def _t5_bias_tiles(posk, posq, tab_ref, heads):
    return _t5_bias_of_rel(posk - posq, tab_ref, heads)


def _t5_bias_of_rel(rel, tab_ref, heads):
    half = NUM_BUCKETS // 2
    max_exact = half // 2
    n = jnp.abs(rel)
    nf = jnp.maximum(n, 1).astype(F32)
    large = max_exact + (jnp.log(nf / max_exact) / math.log(MAX_DISTANCE / max_exact)
                         * (half - max_exact)).astype(I32)
    large = jnp.minimum(large, half - 1)
    bucket = jnp.where(rel > 0, half, 0) + jnp.where(n < max_exact, n, large)
    tk, tq = bucket.shape
    out = []
    for h in range(heads):
        tab = jnp.broadcast_to(tab_ref[h:h + 1, :], (tk, LANES))
        cols = [jnp.take_along_axis(tab, bucket[:, c * LANES:(c + 1) * LANES], axis=1,
                                    mode="promise_in_bounds")
                for c in range(tq // LANES)]
        out.append(jnp.concatenate(cols, axis=1))
    return out


def _dsa_kernel(pmin_ref, pmax_ref, prun_ref,
                qa_ref, qi_ref, wi_ref, ska_ref, ski_ref, vat_ref, posq_ref, posk_ref, tab_ref,
                far_ref, fit_ref,
                o_ref,
                keys_ref, gmax_ref, m_ref, acc_ref, s_ref, q8_ref, *, tq, topk, nq):
    b = pl.program_id(0)
    i = pl.program_id(1)
    tk = tq
    n_tiles = i + 1

    w_t = wi_ref[...].T[0:IDX_HEADS, :]
    key_l = lax.broadcasted_iota(I32, (tk, tq), 0)
    q_l = lax.broadcasted_iota(I32, (tk, tq), 1)
    diag_ok = key_l < (q_l // CHUNK + 1) * CHUNK
    not_key_l = ~key_l

    def score_tile(j, diag):
        r0 = pl.multiple_of(j * tk, tk)
        ki_lo = ski_ref[pl.ds(r0, tk), 0:128]
        ki_hi = ski_ref[pl.ds(r0, tk), 128:256]
        score = None
        for p in range(IDX_HEADS // 2):
            qp = qi_ref[:, p * LANES:(p + 1) * LANES]
            for h, kk in ((2 * p, ki_lo), (2 * p + 1, ki_hi)):
                t = w_t[h:h + 1, :] * jnp.maximum(_dot_nt(kk, qp), 0.0)
                score = t if score is None else score + t
        score = score + 0.0
        bits = lax.bitcast_convert_type(score, I32)
        key = jnp.where(bits < 0, (bits ^ 0x7FFFFFFF) - ZERO_BAND, bits)
        key = jnp.where(score == 0.0, not_key_l - j * tk, key)
        if diag:
            key = jnp.where(diag_ok, key, INT_MIN)
        keys_ref[pl.ds(r0, tk), :] = key
        gmax_ref[...] = jnp.maximum(gmax_ref[...],
                                    jnp.max(key.reshape(tk // GROUPS, GROUPS, tq), axis=0))

    gmax_ref[...] = jnp.full(gmax_ref.shape, INT_MIN, I32)
    lax.fori_loop(0, i, lambda j, c: (score_tile(j, False), c)[1], 0)
    score_tile(i, True)

    def count_tiles(pred):
        def body(j, acc):
            r0 = pl.multiple_of(j * tk, tk)
            return acc + _rowsum8(jnp.where(pred(keys_ref[pl.ds(r0, tk), :], j), 1, 0))
        acc = lax.fori_loop(0, n_tiles, body, jnp.zeros((SUBLANES, tq), I32))
        return jnp.sum(acc, axis=0, keepdims=True)

    q_row = lax.broadcasted_iota(I32, (1, tq), 1)
    n_adm = i * tq + (q_row // CHUNK + 1) * CHUNK

    gmax = gmax_ref[...]
    top = jnp.max(gmax, axis=0, keepdims=True)
    low = jnp.min(gmax, axis=0, keepdims=True)
    cnt_pos = count_tiles(lambda k_, j: k_ >= 1)
    cnt_zero = count_tiles(lambda k_, j: k_ >= -ZERO_BAND)
    few = n_adm <= topk
    in_pos = cnt_pos >= topk
    in_zero = jnp.logical_not(in_pos) & (cnt_zero >= topk)
    base_lo = jnp.where(in_pos, 1, jnp.where(in_zero, -ZERO_BAND, INT_MIN + 1))
    base_cnt = jnp.where(in_pos, cnt_pos, jnp.where(in_zero, cnt_zero, n_adm))
    hi0 = jnp.where(in_pos, top + 1, jnp.where(in_zero, 1, -ZERO_BAND))
    tighter = low > base_lo
    lo0 = jnp.where(few, INT_MIN + 1, jnp.where(tighter, low, base_lo))
    cnt0 = jnp.where(few, topk, jnp.where(tighter, COUNT_UNKNOWN, base_cnt))
    hi0 = jnp.where(few, INT_MIN + 2, hi0)

    def any_row(pred):
        return jnp.max(jnp.where(pred, 1, 0))

    def bisect(lo, hi, cnt_lo, slack):
        def pending(lo, hi, cnt_lo):
            return any_row((cnt_lo - topk > slack) & (hi - lo > 1))

        def step(st):
            it, lo, hi, cnt_lo, _ = st
            mid = lo + ((hi - lo) >> 1)
            cnt = count_tiles(lambda k_, j: k_ >= mid)
            take = cnt >= topk
            lo = jnp.where(take, mid, lo)
            hi = jnp.where(take, hi, mid)
            cnt_lo = jnp.where(take, cnt, cnt_lo)
            return it + 1, lo, hi, cnt_lo, pending(lo, hi, cnt_lo)

        return lax.while_loop(lambda st: (st[0] < 34) & (st[4] > 0), step,
                              (jnp.int32(0), lo, hi, cnt_lo, pending(lo, hi, cnt_lo)))

    _, lo1, hi1, cnt1, _ = bisect(lo0, hi0, cnt0, FINISH_STEPS)

    def surplus(lo, cnt_lo):
        return (cnt_lo > topk) & (cnt_lo - topk <= FINISH_STEPS) & (hi1 - lo > 1)

    def drop_step(st):
        it, lo, cnt_lo, _ = st
        def body(j, acc):
            r0 = pl.multiple_of(j * tk, tk)
            k_ = keys_ref[pl.ds(r0, tk), :]
            cand = jnp.where(k_ >= lo, k_, INT_MAX)
            v = cand.reshape(tk // SUBLANES, SUBLANES, tq)
            while v.shape[0] > 1:
                half = v.shape[0] // 2
                v = jnp.minimum(v[:half], v[half:])
            return jnp.minimum(acc, v[0])
        acc = lax.fori_loop(0, n_tiles, body, jnp.full((SUBLANES, tq), INT_MAX, I32))
        smallest = jnp.min(acc, axis=0, keepdims=True)
        go = surplus(lo, cnt_lo)
        lo = jnp.where(go, smallest + 1, lo)
        cnt_lo = jnp.where(go, cnt_lo - 1, cnt_lo)
        return it + 1, lo, cnt_lo, any_row(surplus(lo, cnt_lo))

    _, lo2, _, _ = lax.while_loop(lambda st: (st[0] < FINISH_STEPS) & (st[3] > 0), drop_step,
                                  (jnp.int32(0), lo1, cnt1, any_row(surplus(lo1, cnt1))))

    cnt2 = count_tiles(lambda k_, j: k_ >= lo2)
    redo = (cnt2 != topk) & jnp.logical_not(few)
    its, tau, _, _, _ = bisect(jnp.where(redo, lo1, lo2), jnp.where(redo, hi1, lo2 + 1),
                               jnp.where(redo, cnt1, topk), 0)
    cnt_ge = lax.cond(its > 0, lambda: count_tiles(lambda k_, j: k_ >= tau), lambda: cnt2)
    over = (cnt_ge > topk) & jnp.logical_not(few)
    tau_sel = tau

    @pl.when(jnp.max(jnp.where(over, 1, 0)) > 0)
    def _ties():
        cnt_gt = count_tiles(lambda k_, j: k_ > tau)
        room = topk - cnt_gt

        def idx_step(p, jmax):
            cand = jmax | jnp.left_shift(jnp.int32(1), 14 - p)
            cnt = count_tiles(lambda k_, j: (k_ == tau) & (key_l + j * tk < cand))
            return jnp.where(cnt <= room, cand, jmax)

        jmax = lax.fori_loop(0, 15, idx_step, jnp.zeros((1, tq), I32))

        def demote(j, carry):
            r0 = pl.multiple_of(j * tk, tk)
            k_ = keys_ref[pl.ds(r0, tk), :]
            drop = (k_ == tau) & (key_l + j * tk >= jmax) & over
            keys_ref[pl.ds(r0, tk), :] = jnp.where(drop, k_ - 1, k_)
            return carry

        lax.fori_loop(0, n_tiles, demote, 0)

    m_ref[...] = jnp.full(m_ref.shape, NEG_BIG, F32)
    acc_ref[...] = jnp.zeros(acc_ref.shape, F32)
    posq = posq_ref[0]
    q8_ref[...] = _fit_queries(qa_ref[...], fit_ref[0, b])

    def attn_tile(j, mode):
        near = mode != "far"
        r0 = pl.multiple_of(j * tk, tk)
        ka_lo = ska_ref[pl.ds(r0, tk), 0:128]
        ka_hi = ska_ref[pl.ds(r0, tk), 128:256]
        va_t = vat_ref[0:PV_ROWS_F8, pl.ds(r0, tk)]
        maskb = jnp.where(keys_ref[pl.ds(r0, tk), :] >= tau_sel, 0.0, NEG_BIG)
        if mode == "any":
            posk_row = lax.bitcast_convert_type(posk_ref[0, :, pl.ds(r0, tk)], F32)
            posk_col = lax.bitcast_convert_type(
                jnp.broadcast_to(posk_row, (LANES, tk)).T, I32)
            bias = _t5_bias_tiles(jnp.tile(posk_col, (1, tq // LANES)), posq, tab_ref, A_HEADS)
        elif mode == "run":
            nb = tq // LANES
            shift_max = (nb - 1) * LANES
            d0 = pmin_ref[b * nq + j] - pmin_ref[b * nq + i]
            x = lax.broadcasted_iota(I32, (tk + shift_max, LANES), 0)
            l = lax.broadcasted_iota(I32, (tk + shift_max, LANES), 1)
            panel = _t5_bias_of_rel(x - l + (d0 - shift_max), tab_ref, A_HEADS)
            bias = [jnp.concatenate([p[shift_max - cb * LANES:shift_max - cb * LANES + tk]
                                     for cb in range(nb)], axis=1) for p in panel]
        m_prev = m_ref[...]
        m_next = [None] * A_HEADS
        slot = lambda h: h % DSA_SLOTS

        def logits(g):
            for h in range(g * DSA_HEAD_GROUP, (g + 1) * DSA_HEAD_GROUP):
                s = _dot_nt(ka_hi if h % 2 else ka_lo,
                            q8_ref[:, (h // 2) * LANES:(h // 2 + 1) * LANES])
                s = s + ((bias[h] + maskb) if near else maskb)
                s_ref[slot(h)] = s
                s_max = jnp.max(s, axis=0, keepdims=True)
                m_next[h] = jnp.maximum(m_prev[h:h + 1, :],
                                        s_max if near else s_max + far_ref[h])

        def weighted_values(g):
            for h in range(g * DSA_HEAD_GROUP, (g + 1) * DSA_HEAD_GROUP):
                alpha = jnp.exp2(m_prev[h:h + 1, :] - m_next[h])
                shift = m_next[h] - (P_SHIFT if near else P_SHIFT + far_ref[h])
                pr = jnp.exp2(s_ref[slot(h)] - shift).astype(F8)
                acc_ref[h, 0:PV_ROWS_F8, :] = (alpha * acc_ref[h, 0:PV_ROWS_F8, :]
                                               + _dot(va_t, pr))

        _two_phase(A_HEADS // DSA_HEAD_GROUP, logits, weighted_values, ahead=DSA_AHEAD)
        m_ref[...] = jnp.concatenate(m_next, axis=0)

    def attn_body(j, carry):
        is_far = pmax_ref[b * nq + j] - pmin_ref[b * nq + i] <= -MAX_DISTANCE
        is_run = (prun_ref[b * nq + j] == 1) & (prun_ref[b * nq + i] == 1)
        lax.cond(is_far, lambda: attn_tile(j, "far"),
                 lambda: lax.cond(is_run, lambda: attn_tile(j, "run"),
                                  lambda: attn_tile(j, "any")))
        return carry

    lax.fori_loop(0, n_tiles, attn_body, 0)
    _store_heads(acc_ref, o_ref, A_HEADS, fit_ref[1, b])


def _dsa(qa, qi, wi, ska, ski, vat, posq, posk, tab, far, fit, pmin, pmax, prun, batch, seq,
         tq, topk):
    n = qa.shape[0]
    nq = seq // tq
    row = lambda w: pl.BlockSpec((tq, w), lambda b, i, *_: (b * nq + i, 0))
    grid_spec = pltpu.PrefetchScalarGridSpec(
        num_scalar_prefetch=3,
        grid=(batch, nq),
        in_specs=[row(A_Q_W), row(I_Q_W), row(LANES),
                  pl.BlockSpec((seq, PAIR_W), lambda b, i, *_: (b, 0),
                               pipeline_mode=pl.Buffered(1)),
                  pl.BlockSpec((seq, PAIR_W), lambda b, i, *_: (b, 0),
                               pipeline_mode=pl.Buffered(1)),
                  pl.BlockSpec((LANES, seq), lambda b, i, *_: (0, b)),
                  pl.BlockSpec((1, 1, tq), lambda b, i, *_: (b * nq + i, 0, 0)),
                  pl.BlockSpec((1, 1, seq), lambda b, i, *_: (b, 0, 0)),
                  pl.BlockSpec((A_HEADS, LANES), lambda b, i, *_: (0, 0)),
                  pl.BlockSpec(memory_space=pltpu.SMEM),
                  pl.BlockSpec(memory_space=pltpu.SMEM)],
        out_specs=row(A_HEADS * V_DIM),
        scratch_shapes=[pltpu.VMEM((seq, tq), I32),
                        pltpu.VMEM((GROUPS, tq), I32),
                        pltpu.VMEM((A_HEADS, tq), F32),
                        pltpu.VMEM((A_HEADS, LANES, tq), F32),
                        pltpu.VMEM((DSA_SLOTS, tq, tq), F32),
                        pltpu.VMEM((tq, A_Q_W), F8)],
    )
    return pl.pallas_call(
        functools.partial(_dsa_kernel, tq=tq, topk=topk, nq=nq),
        grid_spec=grid_spec,
        out_shape=jax.ShapeDtypeStruct((n, A_HEADS * V_DIM), BF16),
        compiler_params=pltpu.CompilerParams(vmem_limit_bytes=VMEM_LIMIT),
        name="dsa",
    )(pmin, pmax, prun, qa, qi, wi, ska, ski, vat, posq, posk, tab, far, fit)


MLA_HEAD_GROUP = 8
MLA_AHEAD = 2


def _mla_kernel(q_ref, k_ref, vt_ref, fit_ref, o_ref, acc_ref, s_ref, q8_ref, *, tq):
    b = pl.program_id(0)
    i = pl.program_id(2)
    tk = tq
    hg = MLA_HEAD_GROUP
    acc_ref[...] = jnp.zeros(acc_ref.shape, F32)
    q8_ref[...] = _fit_queries(q_ref[...], fit_ref[0, b])

    ahead = MLA_AHEAD
    key_l = lax.broadcasted_iota(I32, (tk, tq), 0)
    q_l = lax.broadcasted_iota(I32, (tk, tq), 1)
    ok = key_l < (q_l // CHUNK + 1) * CHUNK
    head = lambda h: slice(h * LANES, (h + 1) * LANES)

    def logits(j, h, masked):
        r0 = j * tk if isinstance(j, int) else pl.multiple_of(j * tk, tk)
        s = _dot_nt(k_ref[pl.ds(r0, tk), head(h)], q8_ref[:, head(h)])
        if masked is True:
            s = jnp.where(ok, s, NEG_BIG)
        elif masked is not False:
            s = jnp.where(ok | jnp.logical_not(masked), s, NEG_BIG)
        s_ref[h] = s
        return jnp.max(s, axis=0, keepdims=True)

    def weighted_values(j, h, m_prev, s_max):
        r0 = pl.multiple_of(j * tk, tk)
        m_next = jnp.maximum(m_prev, s_max)
        alpha = jnp.exp2(m_prev - m_next)
        pr = jnp.exp2(s_ref[h] - (m_next - P_SHIFT)).astype(F8)
        v_t = vt_ref[h * LANES:h * LANES + PV_ROWS_F8, pl.ds(r0, tk)]
        acc_ref[h, 0:PV_ROWS_F8, :] = alpha * acc_ref[h, 0:PV_ROWS_F8, :] + _dot(v_t, pr)
        return m_next

    def tile(j, ms, lead, last, next_last=False):
        s_max = list(lead) + [None] * (hg - ahead)
        nxt = []
        out = []
        for h in range(hg):
            if h + ahead < hg:
                s_max[h + ahead] = logits(j, h + ahead, last)
            elif not last:
                nxt.append(logits(j + 1, h + ahead - hg, next_last))
            out.append(weighted_values(j, h, ms[h], s_max[h]))
        return tuple(out), tuple(nxt)

    m0 = tuple(jnp.full((1, tq), NEG_BIG, F32) for _ in range(hg))
    lead0 = tuple(logits(0, h, i == 0) for h in range(ahead))
    st = lax.fori_loop(0, jnp.maximum(i - 1, 0),
                       lambda j, st: tile(j, st[0], st[1], False), (m0, lead0))
    ms, lead = lax.cond(i >= 1, lambda: tile(i - 1, st[0], st[1], False, True), lambda: st)
    tile(i, ms, lead, True)
    _store_heads(acc_ref, o_ref, hg, fit_ref[1, b])


def _mla(qb, kb, vbt, fit, batch, seq, tq):
    n = qb.shape[0]
    nq = seq // tq
    hg = MLA_HEAD_GROUP
    ng = B_HEADS // hg
    w = hg * LANES
    return pl.pallas_call(
        functools.partial(_mla_kernel, tq=tq),
        grid=(batch, ng, nq),
        in_specs=[pl.BlockSpec((tq, w), lambda b, g, i: (b * nq + i, g)),
                  pl.BlockSpec((seq, w), lambda b, g, i: (b, g), pipeline_mode=pl.Buffered(1)),
                  pl.BlockSpec((w, seq), lambda b, g, i: (b * ng + g, 0),
                               pipeline_mode=pl.Buffered(1)),
                  pl.BlockSpec(memory_space=pltpu.SMEM)],
        out_specs=pl.BlockSpec((tq, hg * V_DIM), lambda b, g, i: (b * nq + i, g)),
        out_shape=jax.ShapeDtypeStruct((n, B_HEADS * V_DIM), BF16),
        scratch_shapes=[pltpu.VMEM((hg, LANES, tq), F32), pltpu.VMEM((hg, tq, tq), F32),
                        pltpu.VMEM((tq, w), F8)],
        compiler_params=pltpu.CompilerParams(vmem_limit_bytes=VMEM_LIMIT),
        name="mla",
    )(qb, kb, vbt, fit)


def _tail_kernel(x_ref, oa_ref, ob_ref, gt_ref, mod_ref, woa_ref, wob_ref, wout_ref,
                 g2_ref, win_ref, wdn_ref, gf_ref, o_ref, *, d, d_ff, chunk):
    x = x_ref[...]
    gt2 = mod_ref[0, 5:6, :]
    sh3, sc3, gt3 = mod_ref[0, 6:7, :], mod_ref[0, 7:8, :], mod_ref[0, 8:9, :]
    ya = _dot(oa_ref[...], woa_ref[...])
    yb = _dot(ob_ref[...], wob_ref[...])
    y = gt_ref[:, 0:d].astype(F32) * ya + gt_ref[:, d:2 * d].astype(F32) * yb
    x = x + gt2 * _dot(y.astype(BF16), wout_ref[...])
    h = _rms(x, g2_ref[...]) * (1.0 + sc3) + sh3
    f = _swiglu_tile(h.astype(BF16), win_ref, wdn_ref, d_ff, chunk)
    x = x + (0.5 * gt3) * f
    o_ref[...] = _rms(x, gf_ref[...])


def _tail(x1, oa, ob, gates, mod3, woa, wob, wout, g2, w_in, w_down, gf, seq, tm):
    n, d = x1.shape
    d_ff = w_down.shape[0]
    tiles_per_batch = seq // tm
    row = lambda w: pl.BlockSpec((tm, w), lambda i: (i, 0))
    kern = functools.partial(_tail_kernel, d=d, d_ff=d_ff, chunk=_ffn_chunk(d_ff))
    return pl.pallas_call(
        kern,
        grid=(n // tm,),
        in_specs=[row(d), row(oa.shape[1]), row(ob.shape[1]), row(2 * d),
                  pl.BlockSpec((1, N_ADA, d), lambda i: (i // tiles_per_batch, 0, 0)),
                  _const_spec(woa.shape), _const_spec(wob.shape), _const_spec(wout.shape),
                  _const_spec((1, d)), _const_spec(w_in.shape), _const_spec(w_down.shape),
                  _const_spec((1, d))],
        out_specs=row(d),
        out_shape=jax.ShapeDtypeStruct((n, d), F32),
        compiler_params=pltpu.CompilerParams(vmem_limit_bytes=VMEM_LIMIT),
        name="tail",
    )(x1, oa, ob, gates, mod3, woa, wob, wout, g2, w_in, w_down, gf)


def _pad_heads(w, heads, width):
    k = w.shape[0]
    w3 = w.reshape(k, heads, width)
    return jnp.pad(w3, ((0, 0), (0, 0), (0, LANES - width))).reshape(k, heads * LANES)


def _layout_w_in(w_in, d):
    z = lambda c: jnp.zeros((d, c), w_in.dtype)
    sizes = (A_Q_W, A_HEAD_DIM, A_HEAD_DIM, I_Q_W, IDX_DIM, IDX_HEADS, Q_LORA, KV_LORA,
             QK_ROPE, 2 * d)
    starts = np.cumsum((0,) + sizes)
    qa, ka, va, qi, ki, wi, cq, ckv, kr, gates = (
        w_in[:, a:a + s] for a, s in zip(starts, sizes))
    qa = qa * (A_HEAD_DIM ** -0.5 * LOG2E)
    hr = QK_ROPE // 2
    kt = jnp.concatenate([-kr[:, hr:], kr[:, :hr]], axis=1)
    half = LANES // 2
    tail = LANES - QK_NOPE - QK_ROPE
    cols = [qa, qi,
            ka, z(half), z(half), ka, va, z(half), ki, z(half), z(half), ki,
            wi, z(LANES - IDX_HEADS),
            z(QK_NOPE), kr, z(tail), z(QK_NOPE), kt, z(tail),
            cq, ckv, gates]
    return jnp.concatenate(cols, axis=1).astype(BF16)


def _layout_w_uq(w_uq):
    scale = (QK_NOPE + QK_ROPE) ** -0.5 * LOG2E
    dh = QK_NOPE + QK_ROPE
    hr = QK_ROPE // 2
    src_plain = np.zeros(B_HEADS * LANES, np.int32)
    mul_plain = np.zeros(B_HEADS * LANES, np.float32)
    src_rot = np.zeros(B_HEADS * LANES, np.int32)
    mul_rot = np.zeros(B_HEADS * LANES, np.float32)
    for h in range(B_HEADS):
        for c in range(dh):
            src_plain[h * LANES + c] = h * dh + c
            mul_plain[h * LANES + c] = scale
        for c in range(QK_ROPE):
            first = c < hr
            src_rot[h * LANES + QK_NOPE + c] = h * dh + QK_NOPE + (c + hr if first else c - hr)
            mul_rot[h * LANES + QK_NOPE + c] = -scale if first else scale
    plain = jnp.take(w_uq, src_plain, axis=1) * mul_plain
    rotw = jnp.take(w_uq, src_rot, axis=1) * mul_rot
    return plain.astype(BF16), rotw.astype(BF16)


def _rope_dense(pos):
    half = QK_ROPE // 2
    assert SUBLANES * half == LANES
    freqs = ROPE_THETA ** (-2.0 * jnp.arange(half, dtype=F32) / QK_ROPE)
    pos_dense = jnp.repeat(pos.reshape(-1, SUBLANES, 1)[:, ::-1, :], half,
                           axis=2).reshape(-1, LANES)
    return pos_dense, jnp.tile(freqs, SUBLANES).reshape(1, LANES)


def _fit_keys_values(q, k, v, batch):
    n = k.shape[0]
    tiny = jnp.float32(1e-30)

    def amax(x):
        return jnp.maximum(jnp.max(jnp.abs(x.astype(F32)).reshape(batch, -1), axis=1), tiny)

    c = jnp.sqrt(amax(q) / amax(k))
    v_max = amax(v)
    per_row = lambda s: jnp.repeat(s, n // batch)[:, None]
    k8 = jnp.clip(k.astype(F32) * per_row(c), -F8_MAX, F8_MAX).astype(F8)
    ones = (jnp.arange(v.shape[1]) % LANES == V_DIM).astype(F32)
    v8 = (v.astype(F32) * per_row(F8_FIT / v_max) + ones).astype(F8)
    return k8, v8, jnp.stack([1.0 / c, v_max / F8_FIT])


def _tiles(seq):
    tm = 512 if seq % 512 == 0 else 256
    tq_dsa = 512 if seq % 512 == 0 else 256
    tq_mla = 512 if seq % 512 == 0 else 256
    return tm, tq_dsa, tq_mla


def kernel(x, c, positions, w_ada, b_ada, g_ffn1, w_ffn1_in, w_ffn1_down, g_mix, w_in,
           g_cq, w_uq, g_ckv, w_uk, w_uv, rel_bias, w_o_a, w_o_b, w_out,
           g_ffn2, w_ffn2_in, w_ffn2_down, g_final):
    batch, seq, d = x.shape
    depth = w_ada.shape[0]
    n = batch * seq
    topk = min(TOPK_MAX, seq // 4)
    tm, tq_dsa, tq_mla = _tiles(seq)
    assert seq % tq_dsa == 0 and seq % tm == 0 and d % LANES == 0 and seq <= ZERO_BAND
    assert topk <= GROUPS and tq_dsa % GROUPS == 0
    assert A_HEAD_DIM == IDX_DIM == V_DIM == LANES // 2
    assert depth == 1, "the tail kernel fuses the final norm into the single layer"

    xf = x.reshape(n, d)
    c_pad = jnp.pad(c, ((0, 8 - batch % 8 if batch % 8 else 0), (0, 0)))
    pos = positions.astype(I32)
    posq = pos.reshape(n // tq_dsa, 1, tq_dsa)
    posk = pos.reshape(batch, 1, seq)
    ptile = pos.reshape(n // tq_dsa, tq_dsa)
    pmin, pmax = ptile.min(axis=1), ptile.max(axis=1)
    prun = jnp.all(ptile[:, 1:] - ptile[:, :-1] == 1, axis=1).astype(I32)
    tab = jnp.pad(rel_bias.T.astype(F32) * LOG2E, ((0, 0), (0, LANES - NUM_BUCKETS)))
    far = rel_bias[NUM_BUCKETS // 2 - 1].astype(F32) * LOG2E

    l = 0
    mod = _ada(c_pad, w_ada[l], b_ada[l])[:batch]
    mod3 = mod.reshape(batch, N_ADA, d)
    x1 = _ffn1(xf, mod3, g_ffn1[l].reshape(1, d), w_ffn1_in[l].astype(BF16),
               w_ffn1_down[l].astype(BF16), seq, tm)
    wq, wqr = _layout_w_uq(w_uq[l])
    qa, qi, ska, ski, va, wi, qb, kb, vb, gates = _proj(
        x1, mod3, g_mix[l].reshape(1, d), _layout_w_in(w_in[l], d),
        g_cq[l].reshape(1, Q_LORA), wq, wqr, g_ckv[l].reshape(1, KV_LORA),
        _pad_heads(w_uk[l], B_HEADS, QK_NOPE).astype(BF16),
        _pad_heads(w_uv[l], B_HEADS, V_DIM).astype(BF16),
        *_rope_dense(pos), seq, tm)
    ska8, va8, fit_a = _fit_keys_values(qa, ska, va, batch)
    kb8, vb8, fit_b = _fit_keys_values(qb, kb, vb, batch)
    vat = va8.T
    vbt = vb8.reshape(batch, seq, B_HEADS * LANES).transpose(0, 2, 1).reshape(-1, seq)
    oa = _dsa(qa, qi, wi, ska8, ski, vat, posq, posk, tab, far, fit_a, pmin, pmax, prun, batch,
              seq, tq_dsa, topk)
    ob = _mla(qb, kb8, vbt, fit_b, batch, seq, tq_mla)
    out = _tail(x1, oa, ob, gates, mod3,
                w_o_a[l].astype(BF16), w_o_b[l].astype(BF16),
                w_out[l].astype(BF16), g_ffn2[l].reshape(1, d),
                w_ffn2_in[l].astype(BF16), w_ffn2_down[l].astype(BF16),
                g_final.reshape(1, d), seq, tm)
    return out.reshape(batch, seq, d)
```

```python
import functools
import math

import jax
import jax.numpy as jnp
import numpy as np
from jax import lax
from jax.experimental import pallas as pl
from jax.experimental.pallas import tpu as pltpu

F32 = jnp.float32
BF16 = jnp.bfloat16
I32 = jnp.int32

CHUNK = 64
EPS = 1e-6
A_HEADS = 8
A_HEAD_DIM = 64
IDX_HEADS = 8
IDX_DIM = 64
TOPK_MAX = 256
B_HEADS = 8
QK_NOPE = 64
QK_ROPE = 32
V_DIM = 64
Q_LORA = 384
KV_LORA = 256
ROPE_THETA = 10000.0
NUM_BUCKETS = 32
MAX_DISTANCE = 128
N_ADA = 9

LANES = 128
SUBLANES = 8
VMEM_LIMIT = 56 * 1024 * 1024
INT_MIN = -2 ** 31
INT_MAX = 2 ** 31 - 1
NEG_BIG = -1e30
LOG2E = math.log2(math.e)
F8 = jnp.float8_e4m3fn
P_SHIFT = 8.0
F8_MAX = 448.0
F8_FIT = 224.0
F8_ROWS = 32
PV_ROWS_F8 = -(-(V_DIM + 1) // F8_ROWS) * F8_ROWS

A_Q_W = A_HEADS * A_HEAD_DIM
I_Q_W = IDX_HEADS * IDX_DIM
PAIR_W = 2 * LANES
SM_W = 5 * LANES
MISC_W = 3 * LANES
C_QA = 0
C_QI = C_QA + A_Q_W
C_SM = C_QI + I_Q_W
C_WI = C_SM + SM_W
C_CQ = C_WI + MISC_W
C_CKV = C_CQ + Q_LORA
C_GT = C_CKV + KV_LORA


def _dot(a, b):
    return jnp.dot(a, b, preferred_element_type=F32)


def _dot_nt(a, b):
    return lax.dot_general(a, b, (((1,), (1,)), ((), ())), preferred_element_type=F32)


def _two_phase(n_groups, phase1, phase2, ahead=1):
    for g in range(min(ahead, n_groups)):
        phase1(g)
    for g in range(n_groups):
        if g + ahead < n_groups:
            phase1(g + ahead)
        phase2(g)


def _fit_queries(q, scale):
    return jnp.clip(q.astype(F32) * scale, -F8_MAX, F8_MAX).astype(F8)


def _store_heads(acc_ref, o_ref, heads, unscale):
    def normalised(h):
        a = acc_ref[h]
        return (a * (unscale / a[V_DIM:V_DIM + 1, :])).T[:, 0:V_DIM]

    for h in range(0, heads, 2):
        pair = jnp.concatenate([normalised(h), normalised(h + 1)], axis=1)
        o_ref[:, h * V_DIM:(h + 2) * V_DIM] = pair.astype(BF16)


def _rms(x, g):
    return x * lax.rsqrt(jnp.mean(x * x, axis=-1, keepdims=True) + EPS) * g


def _const_spec(shape):
    nd = len(shape)
    return pl.BlockSpec(shape, lambda *_: (0,) * nd, pipeline_mode=pl.Buffered(1))


def _ada_kernel(c_ref, w_ref, b_ref, o_ref):
    c = c_ref[...]
    s = c * jax.nn.sigmoid(c)
    o_ref[...] = _dot(s.astype(BF16), w_ref[...].astype(BF16)) + b_ref[...]


def _ada(c_pad, w_ada, b_ada):
    d = c_pad.shape[1]
    n_out = w_ada.shape[1]
    return pl.pallas_call(
        _ada_kernel,
        grid=(n_out // d,),
        in_specs=[pl.BlockSpec(c_pad.shape, lambda j: (0, 0)),
                  pl.BlockSpec((d, d), lambda j: (0, j)),
                  pl.BlockSpec((1, d), lambda j: (0, j))],
        out_specs=pl.BlockSpec(c_pad.shape, lambda j: (0, j)),
        out_shape=jax.ShapeDtypeStruct((c_pad.shape[0], n_out), F32),
        compiler_params=pltpu.CompilerParams(vmem_limit_bytes=VMEM_LIMIT),
        name="ada",
    )(c_pad, w_ada, b_ada.reshape(1, n_out))


def _swiglu_tile(hb, win_ref, wdn_ref, d_ff, chunk):
    acc = None
    for j in range(d_ff // chunk):
        g = _dot(hb, win_ref[:, j * chunk:(j + 1) * chunk])
        u = _dot(hb, win_ref[:, d_ff + j * chunk:d_ff + (j + 1) * chunk])
        a = (g * jax.nn.sigmoid(g) * u).astype(BF16)
        part = _dot(a, wdn_ref[j * chunk:(j + 1) * chunk, :])
        acc = part if acc is None else acc + part
    return acc


def _ffn_chunk(d_ff):
    for c in (512, 256, 128):
        if d_ff % c == 0:
            return c
    raise ValueError("d_ff must be a multiple of 128")


def _ffn1_kernel(x_ref, mod_ref, g_ref, win_ref, wdn_ref, o_ref, *, d_ff, chunk):
    x = x_ref[...]
    sh, sc, gt = mod_ref[0, 0:1, :], mod_ref[0, 1:2, :], mod_ref[0, 2:3, :]
    h = _rms(x, g_ref[...]) * (1.0 + sc) + sh
    y = _swiglu_tile(h.astype(BF16), win_ref, wdn_ref, d_ff, chunk)
    o_ref[...] = x + (0.5 * gt) * y


def _ffn1(x2, mod3, g, w_in, w_down, seq, tm):
    n, d = x2.shape
    d_ff = w_down.shape[0]
    tiles_per_batch = seq // tm
    kern = functools.partial(_ffn1_kernel, d_ff=d_ff, chunk=_ffn_chunk(d_ff))
    return pl.pallas_call(
        kern,
        grid=(n // tm,),
        in_specs=[pl.BlockSpec((tm, d), lambda i: (i, 0)),
                  pl.BlockSpec((1, N_ADA, d), lambda i: (i // tiles_per_batch, 0, 0)),
                  _const_spec((1, d)),
                  _const_spec(w_in.shape),
                  _const_spec(w_down.shape)],
        out_specs=pl.BlockSpec((tm, d), lambda i: (i, 0)),
        out_shape=jax.ShapeDtypeStruct((n, d), F32),
        compiler_params=pltpu.CompilerParams(vmem_limit_bytes=VMEM_LIMIT),
        name="ffn1",
    )(x2, mod3, g, w_in, w_down)


def _proj_kernel(x_ref, mod_ref, g_ref, w_ref, gcq_ref, wq_ref, wqr_ref, gckv_ref,
                 wk_ref, wv_ref, pos_ref, freq_ref,
                 qa_ref, qi_ref, ska_ref, ski_ref, vat_ref, wi_ref, qb_ref, kb_ref, vbt_ref, gt_ref,
                 amax_ref, *, d):
    x = x_ref[...]
    sh, sc = mod_ref[0, 3:4, :], mod_ref[0, 4:5, :]
    hb = (_rms(x, g_ref[...]) * (1.0 + sc) + sh).astype(BF16)
    lane = lax.broadcasted_iota(I32, (1, LANES), 1)
    tm = x.shape[0]
    half = QK_ROPE // 2
    ang = pos_ref[...].astype(F32) * freq_ref[...]

    def spread(dense):
        rows = jnp.broadcast_to(dense[:, None, :], (tm // SUBLANES, SUBLANES, LANES))
        top = pltpu.roll(rows.reshape(tm, LANES), 0, 1, stride=half, stride_axis=0)
        first = pltpu.roll(top, QK_NOPE + half, 1)
        return first, pltpu.roll(first, half, 1)

    in_first = (lane >= QK_NOPE) & (lane < QK_NOPE + half)
    in_second = (lane >= QK_NOPE + half) & (lane < QK_NOPE + QK_ROPE)
    c1, c2 = spread(jnp.cos(ang))
    s1, s2 = spread(jnp.sin(ang))
    cs = jnp.where(lane < QK_NOPE, 1.0, jnp.where(in_first, c1, jnp.where(in_second, c2, 0.0)))
    sn = jnp.where(in_first, s1, jnp.where(in_second, s2, 0.0))

    def tile_max(a):
        col = jnp.max(jnp.abs(a), axis=0, keepdims=True)
        return jnp.broadcast_to(jnp.max(col, axis=1, keepdims=True), (1, LANES))

    qa = _dot(hb, w_ref[:, C_QA:C_QA + A_Q_W])
    qa_ref[...] = qa.astype(BF16)
    qi_ref[...] = _dot(hb, w_ref[:, C_QI:C_QI + I_Q_W]).astype(BF16)

    sm = _dot(hb, w_ref[:, C_SM:C_SM + SM_W])
    ska_ref[...] = sm[:, 0:PAIR_W].astype(BF16)
    va = sm[:, PAIR_W:PAIR_W + LANES]
    vat_ref[...] = va.T.astype(BF16)
    ski_ref[...] = sm[:, PAIR_W + LANES:SM_W].astype(BF16)
    maxima = [tile_max(qa), tile_max(sm[:, 0:LANES]), tile_max(va)]

    misc = _dot(hb, w_ref[:, C_WI:C_WI + MISC_W])
    wi_ref[...] = misc[:, 0:LANES]
    kpe = misc[:, LANES:2 * LANES] * cs + misc[:, 2 * LANES:3 * LANES] * sn

    lat = _dot(hb, w_ref[:, C_CQ:C_GT])
    cq = _rms(lat[:, 0:Q_LORA], gcq_ref[...]).astype(BF16)
    ckv = _rms(lat[:, Q_LORA:Q_LORA + KV_LORA], gckv_ref[...]).astype(BF16)
    qn = _dot(cq, wq_ref[...])
    qr = _dot(cq, wqr_ref[...])
    kn = _dot(ckv, wk_ref[...])
    vv = _dot(ckv, wv_ref[...])
    q_abs = k_abs = None
    for h in range(B_HEADS):
        sl = slice(h * LANES, (h + 1) * LANES)
        qh = qn[:, sl] * cs + qr[:, sl] * sn
        kh = kn[:, sl] + kpe
        qb_ref[:, sl] = qh.astype(BF16)
        kb_ref[:, sl] = kh.astype(BF16)
        q_abs = jnp.abs(qh) if q_abs is None else jnp.maximum(q_abs, jnp.abs(qh))
        k_abs = jnp.abs(kh) if k_abs is None else jnp.maximum(k_abs, jnp.abs(kh))
    vbt_ref[...] = vv.T.astype(BF16)
    maxima += [tile_max(q_abs), tile_max(k_abs), tile_max(vv)]
    amax_ref[...] = jnp.concatenate(
        maxima + [jnp.zeros((SUBLANES - len(maxima), LANES), F32)], axis=0)

    gl = _dot(hb, w_ref[:, C_GT:C_GT + 2 * d])
    gt_ref[...] = jax.nn.sigmoid(gl).astype(BF16)


def _proj(x1, mod3, g, w_all, gcq, wq, wqr, gckv, wk, wv, pos_col, freq_row, seq, tm):
    n, d = x1.shape
    tiles_per_batch = seq // tm
    row = lambda w: pl.BlockSpec((tm, w), lambda i: (i, 0))
    slots = B_HEADS * LANES
    n_tiles = n // tm
    outs = [(row(A_Q_W), (n, A_Q_W), BF16),
            (row(I_Q_W), (n, I_Q_W), BF16),
            (row(PAIR_W), (n, PAIR_W), BF16),
            (row(PAIR_W), (n, PAIR_W), BF16),
            (pl.BlockSpec((LANES, tm), lambda i: (0, i)), (LANES, n), BF16),
            (row(LANES), (n, LANES), F32),
            (row(slots), (n, slots), BF16),
            (row(slots), (n, slots), BF16),
            (pl.BlockSpec((slots, tm), lambda i: (i // tiles_per_batch, i % tiles_per_batch)),
             (n // seq * slots, seq), BF16),
            (row(2 * d), (n, 2 * d), BF16),
            (pl.BlockSpec((SUBLANES, LANES), lambda i: (i, 0)),
             (n_tiles * SUBLANES, LANES), F32)]
    return pl.pallas_call(
        functools.partial(_proj_kernel, d=d),
        grid=(n // tm,),
        in_specs=[row(d),
                  pl.BlockSpec((1, N_ADA, d), lambda i: (i // tiles_per_batch, 0, 0)),
                  _const_spec((1, d)), _const_spec(w_all.shape),
                  _const_spec(gcq.shape), _const_spec(wq.shape), _const_spec(wqr.shape),
                  _const_spec(gckv.shape), _const_spec(wk.shape), _const_spec(wv.shape),
                  pl.BlockSpec((tm // SUBLANES, LANES), lambda i: (i, 0)),
                  _const_spec((1, LANES))],
        out_specs=[o[0] for o in outs],
        out_shape=[jax.ShapeDtypeStruct(o[1], o[2]) for o in outs],
        compiler_params=pltpu.CompilerParams(vmem_limit_bytes=VMEM_LIMIT),
        name="proj",
    )(x1, mod3, g, w_all, gcq, wq, wqr, gckv, wk, wv, pos_col, freq_row)


DSA_HEAD_GROUP = 2
DSA_AHEAD = 1
DSA_SLOTS = (DSA_AHEAD + 1) * DSA_HEAD_GROUP
ZERO_BAND = 1 << 20
GROUPS = 256
COUNT_UNKNOWN = 1 << 30
FINISH_STEPS = 2


def _rowsum8(x):
    v = x.reshape(x.shape[0] // SUBLANES, SUBLANES, x.shape[1])
    while v.shape[0] > 1:
        half = v.shape[0] // 2
        v = v[:half] + v[half:]
    return v[0]


def _t5_bias_tiles(posk, posq, tab_ref, heads):
    return _t5_bias_of_rel(posk - posq, tab_ref, heads)


def _t5_bias_of_rel(rel, tab_ref, heads):
    half = NUM_BUCKETS // 2
    max_exact = half // 2
    n = jnp.abs(rel)
    nf = jnp.maximum(n, 1).astype(F32)
    large = max_exact + (jnp.log(nf / max_exact) / math.log(MAX_DISTANCE / max_exact)
                         * (half - max_exact)).astype(I32)
    large = jnp.minimum(large, half - 1)
    bucket = jnp.where(rel > 0, half, 0) + jnp.where(n < max_exact, n, large)
    tk, tq = bucket.shape
    out = []
    for h in range(heads):
        tab = jnp.broadcast_to(tab_ref[h:h + 1, :], (tk, LANES))
        cols = [jnp.take_along_axis(tab, bucket[:, c * LANES:(c + 1) * LANES], axis=1,
                                    mode="promise_in_bounds")
                for c in range(tq // LANES)]
        out.append(jnp.concatenate(cols, axis=1))
    return out


def _dsa_kernel(pmin_ref, pmax_ref, prun_ref,
                qa_ref, qi_ref, wi_ref, ska_ref, ski_ref, vat_ref, posq_ref, posk_ref, tab_ref,
                far_ref, fit_ref,
                o_ref,
                keys_ref, gmax_ref, m_ref, acc_ref, s_ref, q8_ref, *, tq, topk, nq):
    b = pl.program_id(0)
    i = pl.program_id(1)
    tk = tq
    n_tiles = i + 1

    w_t = wi_ref[...].T[0:IDX_HEADS, :]
    key_l = lax.broadcasted_iota(I32, (tk, tq), 0)
    q_l = lax.broadcasted_iota(I32, (tk, tq), 1)
    diag_ok = key_l < (q_l // CHUNK + 1) * CHUNK
    not_key_l = ~key_l

    def score_tile(j, diag):
        r0 = pl.multiple_of(j * tk, tk)
        ki_lo = ski_ref[pl.ds(r0, tk), 0:128]
        ki_hi = ski_ref[pl.ds(r0, tk), 128:256]
        score = None
        for p in range(IDX_HEADS // 2):
            qp = qi_ref[:, p * LANES:(p + 1) * LANES]
            for h, kk in ((2 * p, ki_lo), (2 * p + 1, ki_hi)):
                t = w_t[h:h + 1, :] * jnp.maximum(_dot_nt(kk, qp), 0.0)
                score = t if score is None else score + t
        score = score + 0.0
        bits = lax.bitcast_convert_type(score, I32)
        key = jnp.where(bits < 0, (bits ^ 0x7FFFFFFF) - ZERO_BAND, bits)
        key = jnp.where(score == 0.0, not_key_l - j * tk, key)
        if diag:
            key = jnp.where(diag_ok, key, INT_MIN)
        keys_ref[pl.ds(r0, tk), :] = key
        gmax_ref[...] = jnp.maximum(gmax_ref[...],
                                    jnp.max(key.reshape(tk // GROUPS, GROUPS, tq), axis=0))

    gmax_ref[...] = jnp.full(gmax_ref.shape, INT_MIN, I32)
    lax.fori_loop(0, i, lambda j, c: (score_tile(j, False), c)[1], 0)
    score_tile(i, True)

    def count_tiles(pred):
        def body(j, acc):
            r0 = pl.multiple_of(j * tk, tk)
            return acc + _rowsum8(jnp.where(pred(keys_ref[pl.ds(r0, tk), :], j), 1, 0))
        acc = lax.fori_loop(0, n_tiles, body, jnp.zeros((SUBLANES, tq), I32))
        return jnp.sum(acc, axis=0, keepdims=True)

    q_row = lax.broadcasted_iota(I32, (1, tq), 1)
    n_adm = i * tq + (q_row // CHUNK + 1) * CHUNK

    gmax = gmax_ref[...]
    top = jnp.max(gmax, axis=0, keepdims=True)
    low = jnp.min(gmax, axis=0, keepdims=True)
    cnt_pos = count_tiles(lambda k_, j: k_ >= 1)
    cnt_zero = count_tiles(lambda k_, j: k_ >= -ZERO_BAND)
    few = n_adm <= topk
    in_pos = cnt_pos >= topk
    in_zero = jnp.logical_not(in_pos) & (cnt_zero >= topk)
    base_lo = jnp.where(in_pos, 1, jnp.where(in_zero, -ZERO_BAND, INT_MIN + 1))
    base_cnt = jnp.where(in_pos, cnt_pos, jnp.where(in_zero, cnt_zero, n_adm))
    hi0 = jnp.where(in_pos, top + 1, jnp.where(in_zero, 1, -ZERO_BAND))
    tighter = low > base_lo
    lo0 = jnp.where(few, INT_MIN + 1, jnp.where(tighter, low, base_lo))
    cnt0 = jnp.where(few, topk, jnp.where(tighter, COUNT_UNKNOWN, base_cnt))
    hi0 = jnp.where(few, INT_MIN + 2, hi0)

    def any_row(pred):
        return jnp.max(jnp.where(pred, 1, 0))

    def bisect(lo, hi, cnt_lo, slack):
        def pending(lo, hi, cnt_lo):
            return any_row((cnt_lo - topk > slack) & (hi - lo > 1))

        def step(st):
            it, lo, hi, cnt_lo, _ = st
            mid = lo + ((hi - lo) >> 1)
            cnt = count_tiles(lambda k_, j: k_ >= mid)
            take = cnt >= topk
            lo = jnp.where(take, mid, lo)
            hi = jnp.where(take, hi, mid)
            cnt_lo = jnp.where(take, cnt, cnt_lo)
            return it + 1, lo, hi, cnt_lo, pending(lo, hi, cnt_lo)

        return lax.while_loop(lambda st: (st[0] < 34) & (st[4] > 0), step,
                              (jnp.int32(0), lo, hi, cnt_lo, pending(lo, hi, cnt_lo)))

    _, lo1, hi1, cnt1, _ = bisect(lo0, hi0, cnt0, FINISH_STEPS)

    def surplus(lo, cnt_lo):
        return (cnt_lo > topk) & (cnt_lo - topk <= FINISH_STEPS) & (hi1 - lo > 1)

    def drop_step(st):
        it, lo, cnt_lo, _ = st
        def body(j, acc):
            r0 = pl.multiple_of(j * tk, tk)
            k_ = keys_ref[pl.ds(r0, tk), :]
            cand = jnp.where(k_ >= lo, k_, INT_MAX)
            v = cand.reshape(tk // SUBLANES, SUBLANES, tq)
            while v.shape[0] > 1:
                half = v.shape[0] // 2
                v = jnp.minimum(v[:half], v[half:])
            return jnp.minimum(acc, v[0])
        acc = lax.fori_loop(0, n_tiles, body, jnp.full((SUBLANES, tq), INT_MAX, I32))
        smallest = jnp.min(acc, axis=0, keepdims=True)
        go = surplus(lo, cnt_lo)
        lo = jnp.where(go, smallest + 1, lo)
        cnt_lo = jnp.where(go, cnt_lo - 1, cnt_lo)
        return it + 1, lo, cnt_lo, any_row(surplus(lo, cnt_lo))

    _, lo2, _, _ = lax.while_loop(lambda st: (st[0] < FINISH_STEPS) & (st[3] > 0), drop_step,
                                  (jnp.int32(0), lo1, cnt1, any_row(surplus(lo1, cnt1))))

    cnt2 = count_tiles(lambda k_, j: k_ >= lo2)
    redo = (cnt2 != topk) & jnp.logical_not(few)
    its, tau, _, _, _ = bisect(jnp.where(redo, lo1, lo2), jnp.where(redo, hi1, lo2 + 1),
                               jnp.where(redo, cnt1, topk), 0)
    cnt_ge = lax.cond(its > 0, lambda: count_tiles(lambda k_, j: k_ >= tau), lambda: cnt2)
    over = (cnt_ge > topk) & jnp.logical_not(few)
    tau_sel = tau

    @pl.when(jnp.max(jnp.where(over, 1, 0)) > 0)
    def _ties():
        cnt_gt = count_tiles(lambda k_, j: k_ > tau)
        room = topk - cnt_gt

        def idx_step(p, jmax):
            cand = jmax | jnp.left_shift(jnp.int32(1), 14 - p)
            cnt = count_tiles(lambda k_, j: (k_ == tau) & (key_l + j * tk < cand))
            return jnp.where(cnt <= room, cand, jmax)

        jmax = lax.fori_loop(0, 15, idx_step, jnp.zeros((1, tq), I32))

        def demote(j, carry):
            r0 = pl.multiple_of(j * tk, tk)
            k_ = keys_ref[pl.ds(r0, tk), :]
            drop = (k_ == tau) & (key_l + j * tk >= jmax) & over
            keys_ref[pl.ds(r0, tk), :] = jnp.where(drop, k_ - 1, k_)
            return carry

        lax.fori_loop(0, n_tiles, demote, 0)

    m_ref[...] = jnp.full(m_ref.shape, NEG_BIG, F32)
    acc_ref[...] = jnp.zeros(acc_ref.shape, F32)
    posq = posq_ref[0]
    q8_ref[...] = _fit_queries(qa_ref[...], fit_ref[0, b])

    def attn_tile(j, mode):
        near = mode != "far"
        r0 = pl.multiple_of(j * tk, tk)
        ka_lo = ska_ref[pl.ds(r0, tk), 0:128]
        ka_hi = ska_ref[pl.ds(r0, tk), 128:256]
        va_t = vat_ref[0:PV_ROWS_F8, pl.ds(r0, tk)]
        maskb = jnp.where(keys_ref[pl.ds(r0, tk), :] >= tau_sel, 0.0, NEG_BIG)
        if mode == "any":
            posk_row = lax.bitcast_convert_type(posk_ref[0, :, pl.ds(r0, tk)], F32)
            posk_col = lax.bitcast_convert_type(
                jnp.broadcast_to(posk_row, (LANES, tk)).T, I32)
            bias = _t5_bias_tiles(jnp.tile(posk_col, (1, tq // LANES)), posq, tab_ref, A_HEADS)
        elif mode == "run":
            nb = tq // LANES
            shift_max = (nb - 1) * LANES
            d0 = pmin_ref[b * nq + j] - pmin_ref[b * nq + i]
            x = lax.broadcasted_iota(I32, (tk + shift_max, LANES), 0)
            l = lax.broadcasted_iota(I32, (tk + shift_max, LANES), 1)
            panel = _t5_bias_of_rel(x - l + (d0 - shift_max), tab_ref, A_HEADS)
            bias = [jnp.concatenate([p[shift_max - cb * LANES:shift_max - cb * LANES + tk]
                                     for cb in range(nb)], axis=1) for p in panel]
        m_prev = m_ref[...]
        m_next = [None] * A_HEADS
        slot = lambda h: h % DSA_SLOTS

        def logits(g):
            for h in range(g * DSA_HEAD_GROUP, (g + 1) * DSA_HEAD_GROUP):
                s = _dot_nt(ka_hi if h % 2 else ka_lo,
                            q8_ref[:, (h // 2) * LANES:(h // 2 + 1) * LANES])
                s = s + ((bias[h] + maskb) if near else maskb)
                s_ref[slot(h)] = s
                s_max = jnp.max(s, axis=0, keepdims=True)
                m_next[h] = jnp.maximum(m_prev[h:h + 1, :],
                                        s_max if near else s_max + far_ref[h])

        def weighted_values(g):
            for h in range(g * DSA_HEAD_GROUP, (g + 1) * DSA_HEAD_GROUP):
                alpha = jnp.exp2(m_prev[h:h + 1, :] - m_next[h])
                shift = m_next[h] - (P_SHIFT if near else P_SHIFT + far_ref[h])
                pr = jnp.exp2(s_ref[slot(h)] - shift).astype(F8)
                acc_ref[h, 0:PV_ROWS_F8, :] = (alpha * acc_ref[h, 0:PV_ROWS_F8, :]
                                               + _dot(va_t, pr))

        _two_phase(A_HEADS // DSA_HEAD_GROUP, logits, weighted_values, ahead=DSA_AHEAD)
        m_ref[...] = jnp.concatenate(m_next, axis=0)

    def attn_body(j, carry):
        is_far = pmax_ref[b * nq + j] - pmin_ref[b * nq + i] <= -MAX_DISTANCE
        is_run = (prun_ref[b * nq + j] == 1) & (prun_ref[b * nq + i] == 1)
        lax.cond(is_far, lambda: attn_tile(j, "far"),
                 lambda: lax.cond(is_run, lambda: attn_tile(j, "run"),
                                  lambda: attn_tile(j, "any")))
        return carry

    lax.fori_loop(0, n_tiles, attn_body, 0)
    _store_heads(acc_ref, o_ref, A_HEADS, fit_ref[1, b])


def _dsa(qa, qi, wi, ska, ski, vat, posq, posk, tab, far, fit, pmin, pmax, prun, batch, seq,
         tq, topk):
    n = qa.shape[0]
    nq = seq // tq
    row = lambda w: pl.BlockSpec((tq, w), lambda b, i, *_: (b * nq + i, 0))
    grid_spec = pltpu.PrefetchScalarGridSpec(
        num_scalar_prefetch=3,
        grid=(batch, nq),
        in_specs=[row(A_Q_W), row(I_Q_W), row(LANES),
                  pl.BlockSpec((seq, PAIR_W), lambda b, i, *_: (b, 0),
                               pipeline_mode=pl.Buffered(1)),
                  pl.BlockSpec((seq, PAIR_W), lambda b, i, *_: (b, 0),
                               pipeline_mode=pl.Buffered(1)),
                  pl.BlockSpec((LANES, seq), lambda b, i, *_: (0, b)),
                  pl.BlockSpec((1, 1, tq), lambda b, i, *_: (b * nq + i, 0, 0)),
                  pl.BlockSpec((1, 1, seq), lambda b, i, *_: (b, 0, 0)),
                  pl.BlockSpec((A_HEADS, LANES), lambda b, i, *_: (0, 0)),
                  pl.BlockSpec(memory_space=pltpu.SMEM),
                  pl.BlockSpec(memory_space=pltpu.SMEM)],
        out_specs=row(A_HEADS * V_DIM),
        scratch_shapes=[pltpu.VMEM((seq, tq), I32),
                        pltpu.VMEM((GROUPS, tq), I32),
                        pltpu.VMEM((A_HEADS, tq), F32),
                        pltpu.VMEM((A_HEADS, LANES, tq), F32),
                        pltpu.VMEM((DSA_SLOTS, tq, tq), F32),
                        pltpu.VMEM((tq, A_Q_W), F8)],
    )
    return pl.pallas_call(
        functools.partial(_dsa_kernel, tq=tq, topk=topk, nq=nq),
        grid_spec=grid_spec,
        out_shape=jax.ShapeDtypeStruct((n, A_HEADS * V_DIM), BF16),
        compiler_params=pltpu.CompilerParams(vmem_limit_bytes=VMEM_LIMIT),
        name="dsa",
    )(pmin, pmax, prun, qa, qi, wi, ska, ski, vat, posq, posk, tab, far, fit)


MLA_HEAD_GROUP = 8
MLA_AHEAD = 2


def _mla_kernel(q_ref, k_ref, vt_ref, fit_ref, o_ref, acc_ref, s_ref, q8_ref, *, tq):
    b = pl.program_id(0)
    i = pl.program_id(2)
    tk = tq
    hg = MLA_HEAD_GROUP
    acc_ref[...] = jnp.zeros(acc_ref.shape, F32)
    q8_ref[...] = _fit_queries(q_ref[...], fit_ref[0, b])

    ahead = MLA_AHEAD
    key_l = lax.broadcasted_iota(I32, (tk, tq), 0)
    q_l = lax.broadcasted_iota(I32, (tk, tq), 1)
    ok = key_l < (q_l // CHUNK + 1) * CHUNK
    head = lambda h: slice(h * LANES, (h + 1) * LANES)

    def logits(j, h, masked):
        r0 = j * tk if isinstance(j, int) else pl.multiple_of(j * tk, tk)
        s = _dot_nt(k_ref[pl.ds(r0, tk), head(h)], q8_ref[:, head(h)])
        if masked is True:
            s = jnp.where(ok, s, NEG_BIG)
        elif masked is not False:
            s = jnp.where(ok | jnp.logical_not(masked), s, NEG_BIG)
        s_ref[h] = s
        return jnp.max(s, axis=0, keepdims=True)

    def weighted_values(j, h, m_prev, s_max):
        r0 = pl.multiple_of(j * tk, tk)
        m_next = jnp.maximum(m_prev, s_max)
        alpha = jnp.exp2(m_prev - m_next)
        pr = jnp.exp2(s_ref[h] - (m_next - P_SHIFT)).astype(F8)
        v_t = vt_ref[h * LANES:h * LANES + PV_ROWS_F8, pl.ds(r0, tk)]
        acc_ref[h, 0:PV_ROWS_F8, :] = alpha * acc_ref[h, 0:PV_ROWS_F8, :] + _dot(v_t, pr)
        return m_next

    def tile(j, ms, lead, last, next_last=False):
        s_max = list(lead) + [None] * (hg - ahead)
        nxt = []
        out = []
        for h in range(hg):
            if h + ahead < hg:
                s_max[h + ahead] = logits(j, h + ahead, last)
            elif not last:
                nxt.append(logits(j + 1, h + ahead - hg, next_last))
            out.append(weighted_values(j, h, ms[h], s_max[h]))
        return tuple(out), tuple(nxt)

    m0 = tuple(jnp.full((1, tq), NEG_BIG, F32) for _ in range(hg))
    lead0 = tuple(logits(0, h, i == 0) for h in range(ahead))
    st = lax.fori_loop(0, jnp.maximum(i - 1, 0),
                       lambda j, st: tile(j, st[0], st[1], False), (m0, lead0))
    ms, lead = lax.cond(i >= 1, lambda: tile(i - 1, st[0], st[1], False, True), lambda: st)
    tile(i, ms, lead, True)
    _store_heads(acc_ref, o_ref, hg, fit_ref[1, b])


def _mla(qb, kb, vbt, fit, batch, seq, tq):
    n = qb.shape[0]
    nq = seq // tq
    hg = MLA_HEAD_GROUP
    ng = B_HEADS // hg
    w = hg * LANES
    return pl.pallas_call(
        functools.partial(_mla_kernel, tq=tq),
        grid=(batch, ng, nq),
        in_specs=[pl.BlockSpec((tq, w), lambda b, g, i: (b * nq + i, g)),
                  pl.BlockSpec((seq, w), lambda b, g, i: (b, g), pipeline_mode=pl.Buffered(1)),
                  pl.BlockSpec((w, seq), lambda b, g, i: (b * ng + g, 0),
                               pipeline_mode=pl.Buffered(1)),
                  pl.BlockSpec(memory_space=pltpu.SMEM)],
        out_specs=pl.BlockSpec((tq, hg * V_DIM), lambda b, g, i: (b * nq + i, g)),
        out_shape=jax.ShapeDtypeStruct((n, B_HEADS * V_DIM), BF16),
        scratch_shapes=[pltpu.VMEM((hg, LANES, tq), F32), pltpu.VMEM((hg, tq, tq), F32),
                        pltpu.VMEM((tq, w), F8)],
        compiler_params=pltpu.CompilerParams(vmem_limit_bytes=VMEM_LIMIT),
        name="mla",
    )(qb, kb, vbt, fit)


def _tail_kernel(x_ref, oa_ref, ob_ref, gt_ref, mod_ref, woa_ref, wob_ref, wout_ref,
                 g2_ref, win_ref, wdn_ref, gf_ref, o_ref, *, d, d_ff, chunk):
    x = x_ref[...]
    gt2 = mod_ref[0, 5:6, :]
    sh3, sc3, gt3 = mod_ref[0, 6:7, :], mod_ref[0, 7:8, :], mod_ref[0, 8:9, :]
    ya = _dot(oa_ref[...], woa_ref[...])
    yb = _dot(ob_ref[...], wob_ref[...])
    y = gt_ref[:, 0:d].astype(F32) * ya + gt_ref[:, d:2 * d].astype(F32) * yb
    x = x + gt2 * _dot(y.astype(BF16), wout_ref[...])
    h = _rms(x, g2_ref[...]) * (1.0 + sc3) + sh3
    f = _swiglu_tile(h.astype(BF16), win_ref, wdn_ref, d_ff, chunk)
    x = x + (0.5 * gt3) * f
    o_ref[...] = _rms(x, gf_ref[...])


def _tail(x1, oa, ob, gates, mod3, woa, wob, wout, g2, w_in, w_down, gf, seq, tm):
    n, d = x1.shape
    d_ff = w_down.shape[0]
    tiles_per_batch = seq // tm
    row = lambda w: pl.BlockSpec((tm, w), lambda i: (i, 0))
    kern = functools.partial(_tail_kernel, d=d, d_ff=d_ff, chunk=_ffn_chunk(d_ff))
    return pl.pallas_call(
        kern,
        grid=(n // tm,),
        in_specs=[row(d), row(oa.shape[1]), row(ob.shape[1]), row(2 * d),
                  pl.BlockSpec((1, N_ADA, d), lambda i: (i // tiles_per_batch, 0, 0)),
                  _const_spec(woa.shape), _const_spec(wob.shape), _const_spec(wout.shape),
                  _const_spec((1, d)), _const_spec(w_in.shape), _const_spec(w_down.shape),
                  _const_spec((1, d))],
        out_specs=row(d),
        out_shape=jax.ShapeDtypeStruct((n, d), F32),
        compiler_params=pltpu.CompilerParams(vmem_limit_bytes=VMEM_LIMIT),
        name="tail",
    )(x1, oa, ob, gates, mod3, woa, wob, wout, g2, w_in, w_down, gf)


def _pad_heads(w, heads, width):
    k = w.shape[0]
    w3 = w.reshape(k, heads, width)
    return jnp.pad(w3, ((0, 0), (0, 0), (0, LANES - width))).reshape(k, heads * LANES)


def _layout_w_in(w_in, d):
    z = lambda c: jnp.zeros((d, c), w_in.dtype)
    sizes = (A_Q_W, A_HEAD_DIM, A_HEAD_DIM, I_Q_W, IDX_DIM, IDX_HEADS, Q_LORA, KV_LORA,
             QK_ROPE, 2 * d)
    starts = np.cumsum((0,) + sizes)
    qa, ka, va, qi, ki, wi, cq, ckv, kr, gates = (
        w_in[:, a:a + s] for a, s in zip(starts, sizes))
    qa = qa * (A_HEAD_DIM ** -0.5 * LOG2E)
    hr = QK_ROPE // 2
    kt = jnp.concatenate([-kr[:, hr:], kr[:, :hr]], axis=1)
    half = LANES // 2
    tail = LANES - QK_NOPE - QK_ROPE
    cols = [qa, qi,
            ka, z(half), z(half), ka, va, z(half), ki, z(half), z(half), ki,
            wi, z(LANES - IDX_HEADS),
            z(QK_NOPE), kr, z(tail), z(QK_NOPE), kt, z(tail),
            cq, ckv, gates]
    return jnp.concatenate(cols, axis=1).astype(BF16)


def _layout_w_uq(w_uq):
    scale = (QK_NOPE + QK_ROPE) ** -0.5 * LOG2E
    dh = QK_NOPE + QK_ROPE
    hr = QK_ROPE // 2
    src_plain = np.zeros(B_HEADS * LANES, np.int32)
    mul_plain = np.zeros(B_HEADS * LANES, np.float32)
    src_rot = np.zeros(B_HEADS * LANES, np.int32)
    mul_rot = np.zeros(B_HEADS * LANES, np.float32)
    for h in range(B_HEADS):
        for c in range(dh):
            src_plain[h * LANES + c] = h * dh + c
            mul_plain[h * LANES + c] = scale
        for c in range(QK_ROPE):
            first = c < hr
            src_rot[h * LANES + QK_NOPE + c] = h * dh + QK_NOPE + (c + hr if first else c - hr)
            mul_rot[h * LANES + QK_NOPE + c] = -scale if first else scale
    plain = jnp.take(w_uq, src_plain, axis=1) * mul_plain
    rotw = jnp.take(w_uq, src_rot, axis=1) * mul_rot
    return plain.astype(BF16), rotw.astype(BF16)


def _rope_dense(pos):
    half = QK_ROPE // 2
    assert SUBLANES * half == LANES
    freqs = ROPE_THETA ** (-2.0 * jnp.arange(half, dtype=F32) / QK_ROPE)
    pos_dense = jnp.repeat(pos.reshape(-1, SUBLANES, 1)[:, ::-1, :], half,
                           axis=2).reshape(-1, LANES)
    return pos_dense, jnp.tile(freqs, SUBLANES).reshape(1, LANES)


def _fit_keys_values(q_max, k_max, v_max, k, v_t, batch, per_batch_rows):
    n = k.shape[0]
    tiny = jnp.float32(1e-30)
    q_max, k_max, v_max = (jnp.maximum(m, tiny) for m in (q_max, k_max, v_max))
    c = jnp.sqrt(q_max / k_max)
    k8 = jnp.clip(k.astype(F32) * jnp.repeat(c, n // batch)[:, None], -F8_MAX, F8_MAX).astype(F8)
    v_scale = F8_FIT / v_max
    if per_batch_rows:
        v_scale = jnp.repeat(v_scale, v_t.shape[0] // batch)[:, None]
    else:
        v_scale = jnp.repeat(v_scale, v_t.shape[1] // batch)[None, :]
    ones = (jnp.arange(v_t.shape[0]) % LANES == V_DIM).astype(F32)[:, None]
    v8_t = (v_t.astype(F32) * v_scale + ones).astype(F8)
    return k8, v8_t, jnp.stack([1.0 / c, v_max / F8_FIT])


def _tiles(seq):
    tm = 512 if seq % 512 == 0 else 256
    tq_dsa = 512 if seq % 512 == 0 else 256
    tq_mla = 512 if seq % 512 == 0 else 256
    return tm, tq_dsa, tq_mla


def kernel(x, c, positions, w_ada, b_ada, g_ffn1, w_ffn1_in, w_ffn1_down, g_mix, w_in,
           g_cq, w_uq, g_ckv, w_uk, w_uv, rel_bias, w_o_a, w_o_b, w_out,
           g_ffn2, w_ffn2_in, w_ffn2_down, g_final):
    batch, seq, d = x.shape
    depth = w_ada.shape[0]
    n = batch * seq
    topk = min(TOPK_MAX, seq // 4)
    tm, tq_dsa, tq_mla = _tiles(seq)
    assert seq % tq_dsa == 0 and seq % tm == 0 and d % LANES == 0 and seq <= ZERO_BAND
    assert topk <= GROUPS and tq_dsa % GROUPS == 0
    assert A_HEAD_DIM == IDX_DIM == V_DIM == LANES // 2
    assert depth == 1, "the tail kernel fuses the final norm into the single layer"

    xf = x.reshape(n, d)
    c_pad = jnp.pad(c, ((0, 8 - batch % 8 if batch % 8 else 0), (0, 0)))
    pos = positions.astype(I32)
    posq = pos.reshape(n // tq_dsa, 1, tq_dsa)
    posk = pos.reshape(batch, 1, seq)
    ptile = pos.reshape(n // tq_dsa, tq_dsa)
    pmin, pmax = ptile.min(axis=1), ptile.max(axis=1)
    prun = jnp.all(ptile[:, 1:] - ptile[:, :-1] == 1, axis=1).astype(I32)
    tab = jnp.pad(rel_bias.T.astype(F32) * LOG2E, ((0, 0), (0, LANES - NUM_BUCKETS)))
    far = rel_bias[NUM_BUCKETS // 2 - 1].astype(F32) * LOG2E

    l = 0
    mod = _ada(c_pad, w_ada[l], b_ada[l])[:batch]
    mod3 = mod.reshape(batch, N_ADA, d)
    x1 = _ffn1(xf, mod3, g_ffn1[l].reshape(1, d), w_ffn1_in[l].astype(BF16),
               w_ffn1_down[l].astype(BF16), seq, tm)
    wq, wqr = _layout_w_uq(w_uq[l])
    qa, qi, ska, ski, vat, wi, qb, kb, vbt, gates, amax = _proj(
        x1, mod3, g_mix[l].reshape(1, d), _layout_w_in(w_in[l], d),
        g_cq[l].reshape(1, Q_LORA), wq, wqr, g_ckv[l].reshape(1, KV_LORA),
        _pad_heads(w_uk[l], B_HEADS, QK_NOPE).astype(BF16),
        _pad_heads(w_uv[l], B_HEADS, V_DIM).astype(BF16),
        *_rope_dense(pos), seq, tm)
    bmax = amax.reshape(batch, seq // tm, SUBLANES, LANES)[:, :, :, 0].max(axis=1)
    ska8, vat8, fit_a = _fit_keys_values(bmax[:, 0], bmax[:, 1], bmax[:, 2], ska, vat, batch,
                                         per_batch_rows=False)
    kb8, vbt8, fit_b = _fit_keys_values(bmax[:, 3], bmax[:, 4], bmax[:, 5], kb, vbt, batch,
                                        per_batch_rows=True)
    oa = _dsa(qa, qi, wi, ska8, ski, vat8, posq, posk, tab, far, fit_a, pmin, pmax, prun, batch,
              seq, tq_dsa, topk)
    ob = _mla(qb, kb8, vbt8, fit_b, batch, seq, tq_mla)
    out = _tail(x1, oa, ob, gates, mod3,
                w_o_a[l].astype(BF16), w_o_b[l].astype(BF16),
                w_out[l].astype(BF16), g_ffn2[l].reshape(1, d),
                w_ffn2_in[l].astype(BF16), w_ffn2_down[l].astype(BF16),
                g_final.reshape(1, d), seq, tm)
    return out.reshape(batch, seq, d)
```

```python
import functools
import math

import jax
import jax.numpy as jnp
import numpy as np
from jax import lax
from jax.experimental import pallas as pl
from jax.experimental.pallas import tpu as pltpu

F32 = jnp.float32
BF16 = jnp.bfloat16
I32 = jnp.int32
I16 = jnp.int16

CHUNK = 64
EPS = 1e-6
A_HEADS = 8
A_HEAD_DIM = 64
IDX_HEADS = 8
IDX_DIM = 64
TOPK_MAX = 256
B_HEADS = 8
QK_NOPE = 64
QK_ROPE = 32
V_DIM = 64
Q_LORA = 384
KV_LORA = 256
ROPE_THETA = 10000.0
NUM_BUCKETS = 32
MAX_DISTANCE = 128
N_ADA = 9

LANES = 128
SUBLANES = 8
VMEM_LIMIT = 56 * 1024 * 1024
INT_MIN = -2 ** 31
INT_MAX = 2 ** 31 - 1
NEG_BIG = -1e30
LOG2E = math.log2(math.e)
F8 = jnp.float8_e4m3fn
P_SHIFT = 8.0
F8_MAX = 448.0
F8_FIT = 224.0
F8_ROWS = 32
I16_ROWS = 16
PV_ROWS_F8 = -(-(V_DIM + 1) // F8_ROWS) * F8_ROWS

A_Q_W = A_HEADS * A_HEAD_DIM
I_Q_W = IDX_HEADS * IDX_DIM
PAIR_W = 2 * LANES
SM_W = 5 * LANES
MISC_W = 3 * LANES
C_QA = 0
C_QI = C_QA + A_Q_W
C_SM = C_QI + I_Q_W
C_WI = C_SM + SM_W
C_CQ = C_WI + MISC_W
C_CKV = C_CQ + Q_LORA
C_GT = C_CKV + KV_LORA


def _dot(a, b):
    return jnp.dot(a, b, preferred_element_type=F32)


def _dot_nt(a, b):
    return lax.dot_general(a, b, (((1,), (1,)), ((), ())), preferred_element_type=F32)


def _two_phase(n_groups, phase1, phase2, ahead=1):
    for g in range(min(ahead, n_groups)):
        phase1(g)
    for g in range(n_groups):
        if g + ahead < n_groups:
            phase1(g + ahead)
        phase2(g)


def _fit_queries(q, scale):
    return jnp.clip(q.astype(F32) * scale, -F8_MAX, F8_MAX).astype(F8)


def _store_heads(acc_ref, o_ref, heads, unscale):
    def normalised(h):
        a = acc_ref[h]
        return (a * (unscale / a[V_DIM:V_DIM + 1, :])).T[:, 0:V_DIM]

    for h in range(0, heads, 2):
        pair = jnp.concatenate([normalised(h), normalised(h + 1)], axis=1)
        o_ref[:, h * V_DIM:(h + 2) * V_DIM] = pair.astype(BF16)


def _rms(x, g):
    return x * lax.rsqrt(jnp.mean(x * x, axis=-1, keepdims=True) + EPS) * g


def _const_spec(shape):
    nd = len(shape)
    return pl.BlockSpec(shape, lambda *_: (0,) * nd, pipeline_mode=pl.Buffered(1))


def _ada_kernel(c_ref, w_ref, b_ref, o_ref):
    c = c_ref[...]
    s = c * jax.nn.sigmoid(c)
    o_ref[...] = _dot(s.astype(BF16), w_ref[...].astype(BF16)) + b_ref[...]


def _ada(c_pad, w_ada, b_ada):
    d = c_pad.shape[1]
    n_out = w_ada.shape[1]
    return pl.pallas_call(
        _ada_kernel,
        grid=(n_out // d,),
        in_specs=[pl.BlockSpec(c_pad.shape, lambda j: (0, 0)),
                  pl.BlockSpec((d, d), lambda j: (0, j)),
                  pl.BlockSpec((1, d), lambda j: (0, j))],
        out_specs=pl.BlockSpec(c_pad.shape, lambda j: (0, j)),
        out_shape=jax.ShapeDtypeStruct((c_pad.shape[0], n_out), F32),
        compiler_params=pltpu.CompilerParams(vmem_limit_bytes=VMEM_LIMIT),
        name="ada",
    )(c_pad, w_ada, b_ada.reshape(1, n_out))


def _swiglu_tile(hb, win_ref, wdn_ref, d_ff, chunk):
    acc = None
    for j in range(d_ff // chunk):
        g = _dot(hb, win_ref[:, j * chunk:(j + 1) * chunk])
        u = _dot(hb, win_ref[:, d_ff + j * chunk:d_ff + (j + 1) * chunk])
        a = (g * jax.nn.sigmoid(g) * u).astype(BF16)
        part = _dot(a, wdn_ref[j * chunk:(j + 1) * chunk, :])
        acc = part if acc is None else acc + part
    return acc


def _ffn_chunk(d_ff):
    for c in (512, 256, 128):
        if d_ff % c == 0:
            return c
    raise ValueError("d_ff must be a multiple of 128")


def _ffn1_kernel(x_ref, mod_ref, g_ref, win_ref, wdn_ref, o_ref, *, d_ff, chunk):
    x = x_ref[...]
    sh, sc, gt = mod_ref[0, 0:1, :], mod_ref[0, 1:2, :], mod_ref[0, 2:3, :]
    h = _rms(x, g_ref[...]) * (1.0 + sc) + sh
    y = _swiglu_tile(h.astype(BF16), win_ref, wdn_ref, d_ff, chunk)
    o_ref[...] = x + (0.5 * gt) * y


def _ffn1(x2, mod3, g, w_in, w_down, seq, tm):
    n, d = x2.shape
    d_ff = w_down.shape[0]
    tiles_per_batch = seq // tm
    kern = functools.partial(_ffn1_kernel, d_ff=d_ff, chunk=_ffn_chunk(d_ff))
    return pl.pallas_call(
        kern,
        grid=(n // tm,),
        in_specs=[pl.BlockSpec((tm, d), lambda i: (i, 0)),
                  pl.BlockSpec((1, N_ADA, d), lambda i: (i // tiles_per_batch, 0, 0)),
                  _const_spec((1, d)),
                  _const_spec(w_in.shape),
                  _const_spec(w_down.shape)],
        out_specs=pl.BlockSpec((tm, d), lambda i: (i, 0)),
        out_shape=jax.ShapeDtypeStruct((n, d), F32),
        compiler_params=pltpu.CompilerParams(vmem_limit_bytes=VMEM_LIMIT),
        name="ffn1",
    )(x2, mod3, g, w_in, w_down)


def _proj_kernel(x_ref, mod_ref, g_ref, w_ref, gcq_ref, wq_ref, wqr_ref, gckv_ref,
                 wk_ref, wv_ref, pos_ref, freq_ref,
                 qa_ref, qi_ref, ska_ref, ski_ref, vat_ref, wi_ref, qb_ref, kb_ref, vbt_ref, gt_ref,
                 amax_ref, *, d):
    x = x_ref[...]
    sh, sc = mod_ref[0, 3:4, :], mod_ref[0, 4:5, :]
    hb = (_rms(x, g_ref[...]) * (1.0 + sc) + sh).astype(BF16)
    lane = lax.broadcasted_iota(I32, (1, LANES), 1)
    tm = x.shape[0]
    half = QK_ROPE // 2
    ang = pos_ref[...].astype(F32) * freq_ref[...]

    def spread(dense):
        rows = jnp.broadcast_to(dense[:, None, :], (tm // SUBLANES, SUBLANES, LANES))
        top = pltpu.roll(rows.reshape(tm, LANES), 0, 1, stride=half, stride_axis=0)
        first = pltpu.roll(top, QK_NOPE + half, 1)
        return first, pltpu.roll(first, half, 1)

    in_first = (lane >= QK_NOPE) & (lane < QK_NOPE + half)
    in_second = (lane >= QK_NOPE + half) & (lane < QK_NOPE + QK_ROPE)
    c1, c2 = spread(jnp.cos(ang))
    s1, s2 = spread(jnp.sin(ang))
    cs = jnp.where(lane < QK_NOPE, 1.0, jnp.where(in_first, c1, jnp.where(in_second, c2, 0.0)))
    sn = jnp.where(in_first, s1, jnp.where(in_second, s2, 0.0))

    def tile_max(a):
        col = jnp.max(jnp.abs(a), axis=0, keepdims=True)
        return jnp.broadcast_to(jnp.max(col, axis=1, keepdims=True), (1, LANES))

    qa = _dot(hb, w_ref[:, C_QA:C_QA + A_Q_W])
    qa_ref[...] = qa.astype(BF16)
    qi_ref[...] = _dot(hb, w_ref[:, C_QI:C_QI + I_Q_W]).astype(BF16)

    sm = _dot(hb, w_ref[:, C_SM:C_SM + SM_W])
    ska_ref[...] = sm[:, 0:PAIR_W].astype(BF16)
    va = sm[:, PAIR_W:PAIR_W + LANES]
    vat_ref[...] = va.T.astype(BF16)
    ski_ref[...] = sm[:, PAIR_W + LANES:SM_W].astype(BF16)
    maxima = [tile_max(qa), tile_max(sm[:, 0:LANES]), tile_max(va)]

    misc = _dot(hb, w_ref[:, C_WI:C_WI + MISC_W])
    wi_ref[...] = misc[:, 0:LANES]
    kpe = misc[:, LANES:2 * LANES] * cs + misc[:, 2 * LANES:3 * LANES] * sn

    lat = _dot(hb, w_ref[:, C_CQ:C_GT])
    cq = _rms(lat[:, 0:Q_LORA], gcq_ref[...]).astype(BF16)
    ckv = _rms(lat[:, Q_LORA:Q_LORA + KV_LORA], gckv_ref[...]).astype(BF16)
    qn = _dot(cq, wq_ref[...])
    qr = _dot(cq, wqr_ref[...])
    kn = _dot(ckv, wk_ref[...])
    vv = _dot(ckv, wv_ref[...])
    q_abs = k_abs = None
    for h in range(B_HEADS):
        sl = slice(h * LANES, (h + 1) * LANES)
        qh = qn[:, sl] * cs + qr[:, sl] * sn
        kh = kn[:, sl] + kpe
        qb_ref[:, sl] = qh.astype(BF16)
        kb_ref[:, sl] = kh.astype(BF16)
        q_abs = jnp.abs(qh) if q_abs is None else jnp.maximum(q_abs, jnp.abs(qh))
        k_abs = jnp.abs(kh) if k_abs is None else jnp.maximum(k_abs, jnp.abs(kh))
    vbt_ref[...] = vv.T.astype(BF16)
    maxima += [tile_max(q_abs), tile_max(k_abs), tile_max(vv)]
    amax_ref[...] = jnp.concatenate(
        maxima + [jnp.zeros((SUBLANES - len(maxima), LANES), F32)], axis=0)

    gl = _dot(hb, w_ref[:, C_GT:C_GT + 2 * d])
    gt_ref[...] = jax.nn.sigmoid(gl).astype(BF16)


def _proj(x1, mod3, g, w_all, gcq, wq, wqr, gckv, wk, wv, pos_col, freq_row, seq, tm):
    n, d = x1.shape
    tiles_per_batch = seq // tm
    row = lambda w: pl.BlockSpec((tm, w), lambda i: (i, 0))
    slots = B_HEADS * LANES
    n_tiles = n // tm
    outs = [(row(A_Q_W), (n, A_Q_W), BF16),
            (row(I_Q_W), (n, I_Q_W), BF16),
            (row(PAIR_W), (n, PAIR_W), BF16),
            (row(PAIR_W), (n, PAIR_W), BF16),
            (pl.BlockSpec((LANES, tm), lambda i: (0, i)), (LANES, n), BF16),
            (row(LANES), (n, LANES), F32),
            (row(slots), (n, slots), BF16),
            (row(slots), (n, slots), BF16),
            (pl.BlockSpec((slots, tm), lambda i: (i // tiles_per_batch, i % tiles_per_batch)),
             (n // seq * slots, seq), BF16),
            (row(2 * d), (n, 2 * d), BF16),
            (pl.BlockSpec((SUBLANES, LANES), lambda i: (i, 0)),
             (n_tiles * SUBLANES, LANES), F32)]
    return pl.pallas_call(
        functools.partial(_proj_kernel, d=d),
        grid=(n // tm,),
        in_specs=[row(d),
                  pl.BlockSpec((1, N_ADA, d), lambda i: (i // tiles_per_batch, 0, 0)),
                  _const_spec((1, d)), _const_spec(w_all.shape),
                  _const_spec(gcq.shape), _const_spec(wq.shape), _const_spec(wqr.shape),
                  _const_spec(gckv.shape), _const_spec(wk.shape), _const_spec(wv.shape),
                  pl.BlockSpec((tm // SUBLANES, LANES), lambda i: (i, 0)),
                  _const_spec((1, LANES))],
        out_specs=[o[0] for o in outs],
        out_shape=[jax.ShapeDtypeStruct(o[1], o[2]) for o in outs],
        compiler_params=pltpu.CompilerParams(vmem_limit_bytes=VMEM_LIMIT),
        name="proj",
    )(x1, mod3, g, w_all, gcq, wq, wqr, gckv, wk, wv, pos_col, freq_row)


DSA_HEAD_GROUP = 2
DSA_AHEAD = 1
DSA_SLOTS = (DSA_AHEAD + 1) * DSA_HEAD_GROUP
ZERO_BAND = 1 << 20
GROUPS = 256
COUNT_UNKNOWN = 1 << 30
FINISH_STEPS = 2


def _rowsum8(x):
    v = x.reshape(x.shape[0] // SUBLANES, SUBLANES, x.shape[1])
    while v.shape[0] > 1:
        half = v.shape[0] // 2
        v = v[:half] + v[half:]
    return v[0]


def _t5_bias_tiles(posk, posq, tab_ref, heads):
    return _t5_bias_of_rel(posk - posq, tab_ref, heads)


def _t5_bias_of_rel(rel, tab_ref, heads):
    half = NUM_BUCKETS // 2
    max_exact = half // 2
    n = jnp.abs(rel)
    nf = jnp.maximum(n, 1).astype(F32)
    large = max_exact + (jnp.log(nf / max_exact) / math.log(MAX_DISTANCE / max_exact)
                         * (half - max_exact)).astype(I32)
    large = jnp.minimum(large, half - 1)
    bucket = jnp.where(rel > 0, half, 0) + jnp.where(n < max_exact, n, large)
    tk, tq = bucket.shape
    out = []
    for h in range(heads):
        tab = jnp.broadcast_to(tab_ref[h:h + 1, :], (tk, LANES))
        cols = [jnp.take_along_axis(tab, bucket[:, c * LANES:(c + 1) * LANES], axis=1,
                                    mode="promise_in_bounds")
                for c in range(tq // LANES)]
        out.append(jnp.concatenate(cols, axis=1))
    return out


def _dsa_kernel(pmin_ref, pmax_ref, prun_ref,
                qa_ref, qi_ref, wi_ref, ska_ref, ski_ref, vat_ref, posq_ref, posk_ref, tab_ref,
                far_ref, fit_ref,
                o_ref,
                keys_ref, keys16_ref, gmax_ref, m_ref, acc_ref, s_ref, q8_ref, *, tq, topk, nq):
    b = pl.program_id(0)
    i = pl.program_id(1)
    tk = tq
    n_tiles = i + 1

    w_t = wi_ref[...].T[0:IDX_HEADS, :]
    key_l = lax.broadcasted_iota(I32, (tk, tq), 0)
    q_l = lax.broadcasted_iota(I32, (tk, tq), 1)
    diag_ok = key_l < (q_l // CHUNK + 1) * CHUNK
    not_key_l = ~key_l

    def score_tile(j, diag):
        r0 = pl.multiple_of(j * tk, tk)
        ki_lo = ski_ref[pl.ds(r0, tk), 0:128]
        ki_hi = ski_ref[pl.ds(r0, tk), 128:256]
        score = None
        for p in range(IDX_HEADS // 2):
            qp = qi_ref[:, p * LANES:(p + 1) * LANES]
            for h, kk in ((2 * p, ki_lo), (2 * p + 1, ki_hi)):
                t = w_t[h:h + 1, :] * jnp.maximum(_dot_nt(kk, qp), 0.0)
                score = t if score is None else score + t
        score = score + 0.0
        bits = lax.bitcast_convert_type(score, I32)
        key = jnp.where(bits < 0, (bits ^ 0x7FFFFFFF) - ZERO_BAND, bits)
        key = jnp.where(score == 0.0, not_key_l - j * tk, key)
        if diag:
            key = jnp.where(diag_ok, key, INT_MIN)
        keys_ref[pl.ds(r0, tk), :] = key
        keys16_ref[pl.ds(r0, tk), :] = (key >> 16).astype(I16)
        gmax_ref[...] = jnp.maximum(gmax_ref[...],
                                    jnp.max(key.reshape(tk // GROUPS, GROUPS, tq), axis=0))

    gmax_ref[...] = jnp.full(gmax_ref.shape, INT_MIN, I32)
    lax.fori_loop(0, i, lambda j, c: (score_tile(j, False), c)[1], 0)
    score_tile(i, True)

    def count_tiles(pred):
        def body(j, acc):
            r0 = pl.multiple_of(j * tk, tk)
            return acc + _rowsum8(jnp.where(pred(keys_ref[pl.ds(r0, tk), :], j), 1, 0))
        acc = lax.fori_loop(0, n_tiles, body, jnp.zeros((SUBLANES, tq), I32))
        return jnp.sum(acc, axis=0, keepdims=True)

    q_row = lax.broadcasted_iota(I32, (1, tq), 1)
    n_adm = i * tq + (q_row // CHUNK + 1) * CHUNK

    gmax = gmax_ref[...]
    top = jnp.max(gmax, axis=0, keepdims=True)
    low = jnp.min(gmax, axis=0, keepdims=True)
    cnt_pos = count_tiles(lambda k_, j: k_ >= 1)
    cnt_zero = count_tiles(lambda k_, j: k_ >= -ZERO_BAND)
    few = n_adm <= topk
    in_pos = cnt_pos >= topk
    in_zero = jnp.logical_not(in_pos) & (cnt_zero >= topk)
    base_lo = jnp.where(in_pos, 1, jnp.where(in_zero, -ZERO_BAND, INT_MIN + 1))
    base_cnt = jnp.where(in_pos, cnt_pos, jnp.where(in_zero, cnt_zero, n_adm))
    hi0 = jnp.where(in_pos, top + 1, jnp.where(in_zero, 1, -ZERO_BAND))
    tighter = low > base_lo
    lo0 = jnp.where(few, INT_MIN + 1, jnp.where(tighter, low, base_lo))
    cnt0 = jnp.where(few, topk, jnp.where(tighter, COUNT_UNKNOWN, base_cnt))
    hi0 = jnp.where(few, INT_MIN + 2, hi0)

    def any_row(pred):
        return jnp.max(jnp.where(pred, 1, 0))

    def count_high(c):
        c16 = c.astype(I16)

        def body(j, acc):
            r0 = pl.multiple_of(j * tk, tk)
            ind = jnp.where(keys16_ref[pl.ds(r0, tk), :] >= c16,
                            jnp.ones((), I16), jnp.zeros((), I16))
            v = ind.reshape(tk // I16_ROWS, I16_ROWS, tq)
            while v.shape[0] > 1:
                half = v.shape[0] // 2
                v = v[:half] + v[half:]
            return acc + v[0]

        acc = lax.fori_loop(0, n_tiles, body, jnp.zeros((I16_ROWS, tq), I16))
        return jnp.sum(acc.astype(I32), axis=0, keepdims=True)

    def coarse_pending(lo_c, hi_c, cnt_c):
        return any_row((hi_c - lo_c > 1) & (cnt_c != topk))

    def coarse_step(st):
        it, lo_c, hi_c, cnt_c, _ = st
        mid = lo_c + ((hi_c - lo_c) >> 1)
        cnt = count_high(mid)
        take = cnt >= topk
        lo_c = jnp.where(take, mid, lo_c)
        hi_c = jnp.where(take, hi_c, mid)
        cnt_c = jnp.where(take, cnt, cnt_c)
        return it + 1, lo_c, hi_c, cnt_c, coarse_pending(lo_c, hi_c, cnt_c)

    lo_c0, hi_c0 = lo0 >> 16, (hi0 >> 16) + 1
    cnt_c0 = jnp.full((1, tq), COUNT_UNKNOWN, I32)
    _, lo_c, hi_c, cnt_c, _ = lax.while_loop(
        lambda st: (st[0] < 18) & (st[4] > 0), coarse_step,
        (jnp.int32(0), lo_c0, hi_c0, cnt_c0, coarse_pending(lo_c0, hi_c0, cnt_c0)))
    lo_w, hi_w = lo_c << 16, hi_c << 16
    counted = (lo_w >= lo0) & (cnt_c != COUNT_UNKNOWN)
    cnt0 = jnp.where(counted, cnt_c, cnt0)
    lo0 = jnp.maximum(lo0, lo_w)
    hi0 = jnp.minimum(hi0, hi_w)

    def bisect(lo, hi, cnt_lo, slack):
        def pending(lo, hi, cnt_lo):
            return any_row((cnt_lo - topk > slack) & (hi - lo > 1))

        def step(st):
            it, lo, hi, cnt_lo, _ = st
            mid = lo + ((hi - lo) >> 1)
            cnt = count_tiles(lambda k_, j: k_ >= mid)
            take = cnt >= topk
            lo = jnp.where(take, mid, lo)
            hi = jnp.where(take, hi, mid)
            cnt_lo = jnp.where(take, cnt, cnt_lo)
            return it + 1, lo, hi, cnt_lo, pending(lo, hi, cnt_lo)

        return lax.while_loop(lambda st: (st[0] < 34) & (st[4] > 0), step,
                              (jnp.int32(0), lo, hi, cnt_lo, pending(lo, hi, cnt_lo)))

    _, lo1, hi1, cnt1, _ = bisect(lo0, hi0, cnt0, FINISH_STEPS)

    def surplus(lo, cnt_lo):
        return (cnt_lo > topk) & (cnt_lo - topk <= FINISH_STEPS) & (hi1 - lo > 1)

    def drop_step(st):
        it, lo, cnt_lo, _ = st
        def body(j, acc):
            r0 = pl.multiple_of(j * tk, tk)
            k_ = keys_ref[pl.ds(r0, tk), :]
            cand = jnp.where(k_ >= lo, k_, INT_MAX)
            v = cand.reshape(tk // SUBLANES, SUBLANES, tq)
            while v.shape[0] > 1:
                half = v.shape[0] // 2
                v = jnp.minimum(v[:half], v[half:])
            return jnp.minimum(acc, v[0])
        acc = lax.fori_loop(0, n_tiles, body, jnp.full((SUBLANES, tq), INT_MAX, I32))
        smallest = jnp.min(acc, axis=0, keepdims=True)
        go = surplus(lo, cnt_lo)
        lo = jnp.where(go, smallest + 1, lo)
        cnt_lo = jnp.where(go, cnt_lo - 1, cnt_lo)
        return it + 1, lo, cnt_lo, any_row(surplus(lo, cnt_lo))

    _, lo2, _, _ = lax.while_loop(lambda st: (st[0] < FINISH_STEPS) & (st[3] > 0), drop_step,
                                  (jnp.int32(0), lo1, cnt1, any_row(surplus(lo1, cnt1))))

    cnt2 = count_tiles(lambda k_, j: k_ >= lo2)
    redo = (cnt2 != topk) & jnp.logical_not(few)
    its, tau, _, _, _ = bisect(jnp.where(redo, lo1, lo2), jnp.where(redo, hi1, lo2 + 1),
                               jnp.where(redo, cnt1, topk), 0)
    cnt_ge = lax.cond(its > 0, lambda: count_tiles(lambda k_, j: k_ >= tau), lambda: cnt2)
    over = (cnt_ge > topk) & jnp.logical_not(few)
    tau_sel = tau

    @pl.when(jnp.max(jnp.where(over, 1, 0)) > 0)
    def _ties():
        cnt_gt = count_tiles(lambda k_, j: k_ > tau)
        room = topk - cnt_gt

        def idx_step(p, jmax):
            cand = jmax | jnp.left_shift(jnp.int32(1), 14 - p)
            cnt = count_tiles(lambda k_, j: (k_ == tau) & (key_l + j * tk < cand))
            return jnp.where(cnt <= room, cand, jmax)

        jmax = lax.fori_loop(0, 15, idx_step, jnp.zeros((1, tq), I32))

        def demote(j, carry):
            r0 = pl.multiple_of(j * tk, tk)
            k_ = keys_ref[pl.ds(r0, tk), :]
            drop = (k_ == tau) & (key_l + j * tk >= jmax) & over
            keys_ref[pl.ds(r0, tk), :] = jnp.where(drop, k_ - 1, k_)
            return carry

        lax.fori_loop(0, n_tiles, demote, 0)

    m_ref[...] = jnp.full(m_ref.shape, NEG_BIG, F32)
    acc_ref[...] = jnp.zeros(acc_ref.shape, F32)
    posq = posq_ref[0]
    q8_ref[...] = _fit_queries(qa_ref[...], fit_ref[0, b])

    def attn_tile(j, mode):
        near = mode != "far"
        r0 = pl.multiple_of(j * tk, tk)
        ka_lo = ska_ref[pl.ds(r0, tk), 0:128]
        ka_hi = ska_ref[pl.ds(r0, tk), 128:256]
        va_t = vat_ref[0:PV_ROWS_F8, pl.ds(r0, tk)]
        maskb = jnp.where(keys_ref[pl.ds(r0, tk), :] >= tau_sel, 0.0, NEG_BIG)
        if mode == "any":
            posk_row = lax.bitcast_convert_type(posk_ref[0, :, pl.ds(r0, tk)], F32)
            posk_col = lax.bitcast_convert_type(
                jnp.broadcast_to(posk_row, (LANES, tk)).T, I32)
            bias = _t5_bias_tiles(jnp.tile(posk_col, (1, tq // LANES)), posq, tab_ref, A_HEADS)
        elif mode == "run":
            nb = tq // LANES
            shift_max = (nb - 1) * LANES
            d0 = pmin_ref[b * nq + j] - pmin_ref[b * nq + i]
            x = lax.broadcasted_iota(I32, (tk + shift_max, LANES), 0)
            l = lax.broadcasted_iota(I32, (tk + shift_max, LANES), 1)
            panel = _t5_bias_of_rel(x - l + (d0 - shift_max), tab_ref, A_HEADS)
            bias = [jnp.concatenate([p[shift_max - cb * LANES:shift_max - cb * LANES + tk]
                                     for cb in range(nb)], axis=1) for p in panel]
        m_prev = m_ref[...]
        m_next = [None] * A_HEADS
        slot = lambda h: h % DSA_SLOTS

        def logits(g):
            for h in range(g * DSA_HEAD_GROUP, (g + 1) * DSA_HEAD_GROUP):
                s = _dot_nt(ka_hi if h % 2 else ka_lo,
                            q8_ref[:, (h // 2) * LANES:(h // 2 + 1) * LANES])
                s = s + ((bias[h] + maskb) if near else maskb)
                s_ref[slot(h)] = s
                s_max = jnp.max(s, axis=0, keepdims=True)
                m_next[h] = jnp.maximum(m_prev[h:h + 1, :],
                                        s_max if near else s_max + far_ref[h])

        def weighted_values(g):
            for h in range(g * DSA_HEAD_GROUP, (g + 1) * DSA_HEAD_GROUP):
                alpha = jnp.exp2(m_prev[h:h + 1, :] - m_next[h])
                shift = m_next[h] - (P_SHIFT if near else P_SHIFT + far_ref[h])
                pr = jnp.exp2(s_ref[slot(h)] - shift).astype(F8)
                acc_ref[h, 0:PV_ROWS_F8, :] = (alpha * acc_ref[h, 0:PV_ROWS_F8, :]
                                               + _dot(va_t, pr))

        _two_phase(A_HEADS // DSA_HEAD_GROUP, logits, weighted_values, ahead=DSA_AHEAD)
        m_ref[...] = jnp.concatenate(m_next, axis=0)

    def attn_body(j, carry):
        is_far = pmax_ref[b * nq + j] - pmin_ref[b * nq + i] <= -MAX_DISTANCE
        is_run = (prun_ref[b * nq + j] == 1) & (prun_ref[b * nq + i] == 1)
        lax.cond(is_far, lambda: attn_tile(j, "far"),
                 lambda: lax.cond(is_run, lambda: attn_tile(j, "run"),
                                  lambda: attn_tile(j, "any")))
        return carry

    lax.fori_loop(0, n_tiles, attn_body, 0)
    _store_heads(acc_ref, o_ref, A_HEADS, fit_ref[1, b])


def _dsa(qa, qi, wi, ska, ski, vat, posq, posk, tab, far, fit, pmin, pmax, prun, batch, seq,
         tq, topk):
    n = qa.shape[0]
    nq = seq // tq
    row = lambda w: pl.BlockSpec((tq, w), lambda b, i, *_: (b * nq + i, 0))
    grid_spec = pltpu.PrefetchScalarGridSpec(
        num_scalar_prefetch=3,
        grid=(batch, nq),
        in_specs=[row(A_Q_W), row(I_Q_W), row(LANES),
                  pl.BlockSpec((seq, PAIR_W), lambda b, i, *_: (b, 0),
                               pipeline_mode=pl.Buffered(1)),
                  pl.BlockSpec((seq, PAIR_W), lambda b, i, *_: (b, 0),
                               pipeline_mode=pl.Buffered(1)),
                  pl.BlockSpec((LANES, seq), lambda b, i, *_: (0, b)),
                  pl.BlockSpec((1, 1, tq), lambda b, i, *_: (b * nq + i, 0, 0)),
                  pl.BlockSpec((1, 1, seq), lambda b, i, *_: (b, 0, 0)),
                  pl.BlockSpec((A_HEADS, LANES), lambda b, i, *_: (0, 0)),
                  pl.BlockSpec(memory_space=pltpu.SMEM),
                  pl.BlockSpec(memory_space=pltpu.SMEM)],
        out_specs=row(A_HEADS * V_DIM),
        scratch_shapes=[pltpu.VMEM((seq, tq), I32),
                        pltpu.VMEM((seq, tq), I16),
                        pltpu.VMEM((GROUPS, tq), I32),
                        pltpu.VMEM((A_HEADS, tq), F32),
                        pltpu.VMEM((A_HEADS, LANES, tq), F32),
                        pltpu.VMEM((DSA_SLOTS, tq, tq), F32),
                        pltpu.VMEM((tq, A_Q_W), F8)],
    )
    return pl.pallas_call(
        functools.partial(_dsa_kernel, tq=tq, topk=topk, nq=nq),
        grid_spec=grid_spec,
        out_shape=jax.ShapeDtypeStruct((n, A_HEADS * V_DIM), BF16),
        compiler_params=pltpu.CompilerParams(vmem_limit_bytes=VMEM_LIMIT),
        name="dsa",
    )(pmin, pmax, prun, qa, qi, wi, ska, ski, vat, posq, posk, tab, far, fit)


MLA_HEAD_GROUP = 8
MLA_AHEAD = 2


def _mla_kernel(q_ref, k_ref, vt_ref, fit_ref, o_ref, acc_ref, s_ref, q8_ref, *, tq):
    b = pl.program_id(0)
    i = pl.program_id(2)
    tk = tq
    hg = MLA_HEAD_GROUP
    acc_ref[...] = jnp.zeros(acc_ref.shape, F32)
    q8_ref[...] = _fit_queries(q_ref[...], fit_ref[0, b])

    ahead = MLA_AHEAD
    key_l = lax.broadcasted_iota(I32, (tk, tq), 0)
    q_l = lax.broadcasted_iota(I32, (tk, tq), 1)
    ok = key_l < (q_l // CHUNK + 1) * CHUNK
    head = lambda h: slice(h * LANES, (h + 1) * LANES)

    def logits(j, h, masked):
        r0 = j * tk if isinstance(j, int) else pl.multiple_of(j * tk, tk)
        s = _dot_nt(k_ref[pl.ds(r0, tk), head(h)], q8_ref[:, head(h)])
        if masked is True:
            s = jnp.where(ok, s, NEG_BIG)
        elif masked is not False:
            s = jnp.where(ok | jnp.logical_not(masked), s, NEG_BIG)
        s_ref[h] = s
        return jnp.max(s, axis=0, keepdims=True)

    def weighted_values(j, h, m_prev, s_max):
        r0 = pl.multiple_of(j * tk, tk)
        m_next = jnp.maximum(m_prev, s_max)
        alpha = jnp.exp2(m_prev - m_next)
        pr = jnp.exp2(s_ref[h] - (m_next - P_SHIFT)).astype(F8)
        v_t = vt_ref[h * LANES:h * LANES + PV_ROWS_F8, pl.ds(r0, tk)]
        acc_ref[h, 0:PV_ROWS_F8, :] = alpha * acc_ref[h, 0:PV_ROWS_F8, :] + _dot(v_t, pr)
        return m_next

    def tile(j, ms, lead, last, next_last=False):
        s_max = list(lead) + [None] * (hg - ahead)
        nxt = []
        out = []
        for h in range(hg):
            if h + ahead < hg:
                s_max[h + ahead] = logits(j, h + ahead, last)
            elif not last:
                nxt.append(logits(j + 1, h + ahead - hg, next_last))
            out.append(weighted_values(j, h, ms[h], s_max[h]))
        return tuple(out), tuple(nxt)

    m0 = tuple(jnp.full((1, tq), NEG_BIG, F32) for _ in range(hg))
    lead0 = tuple(logits(0, h, i == 0) for h in range(ahead))
    st = lax.fori_loop(0, jnp.maximum(i - 1, 0),
                       lambda j, st: tile(j, st[0], st[1], False), (m0, lead0))
    ms, lead = lax.cond(i >= 1, lambda: tile(i - 1, st[0], st[1], False, True), lambda: st)
    tile(i, ms, lead, True)
    _store_heads(acc_ref, o_ref, hg, fit_ref[1, b])


def _mla(qb, kb, vbt, fit, batch, seq, tq):
    n = qb.shape[0]
    nq = seq // tq
    hg = MLA_HEAD_GROUP
    ng = B_HEADS // hg
    w = hg * LANES
    return pl.pallas_call(
        functools.partial(_mla_kernel, tq=tq),
        grid=(batch, ng, nq),
        in_specs=[pl.BlockSpec((tq, w), lambda b, g, i: (b * nq + i, g)),
                  pl.BlockSpec((seq, w), lambda b, g, i: (b, g), pipeline_mode=pl.Buffered(1)),
                  pl.BlockSpec((w, seq), lambda b, g, i: (b * ng + g, 0),
                               pipeline_mode=pl.Buffered(1)),
                  pl.BlockSpec(memory_space=pltpu.SMEM)],
        out_specs=pl.BlockSpec((tq, hg * V_DIM), lambda b, g, i: (b * nq + i, g)),
        out_shape=jax.ShapeDtypeStruct((n, B_HEADS * V_DIM), BF16),
        scratch_shapes=[pltpu.VMEM((hg, LANES, tq), F32), pltpu.VMEM((hg, tq, tq), F32),
                        pltpu.VMEM((tq, w), F8)],
        compiler_params=pltpu.CompilerParams(vmem_limit_bytes=VMEM_LIMIT),
        name="mla",
    )(qb, kb, vbt, fit)


def _tail_kernel(x_ref, oa_ref, ob_ref, gt_ref, mod_ref, woa_ref, wob_ref, wout_ref,
                 g2_ref, win_ref, wdn_ref, gf_ref, o_ref, *, d, d_ff, chunk):
    x = x_ref[...]
    gt2 = mod_ref[0, 5:6, :]
    sh3, sc3, gt3 = mod_ref[0, 6:7, :], mod_ref[0, 7:8, :], mod_ref[0, 8:9, :]
    ya = _dot(oa_ref[...], woa_ref[...])
    yb = _dot(ob_ref[...], wob_ref[...])
    y = gt_ref[:, 0:d].astype(F32) * ya + gt_ref[:, d:2 * d].astype(F32) * yb
    x = x + gt2 * _dot(y.astype(BF16), wout_ref[...])
    h = _rms(x, g2_ref[...]) * (1.0 + sc3) + sh3
    f = _swiglu_tile(h.astype(BF16), win_ref, wdn_ref, d_ff, chunk)
    x = x + (0.5 * gt3) * f
    o_ref[...] = _rms(x, gf_ref[...])


def _tail(x1, oa, ob, gates, mod3, woa, wob, wout, g2, w_in, w_down, gf, seq, tm):
    n, d = x1.shape
    d_ff = w_down.shape[0]
    tiles_per_batch = seq // tm
    row = lambda w: pl.BlockSpec((tm, w), lambda i: (i, 0))
    kern = functools.partial(_tail_kernel, d=d, d_ff=d_ff, chunk=_ffn_chunk(d_ff))
    return pl.pallas_call(
        kern,
        grid=(n // tm,),
        in_specs=[row(d), row(oa.shape[1]), row(ob.shape[1]), row(2 * d),
                  pl.BlockSpec((1, N_ADA, d), lambda i: (i // tiles_per_batch, 0, 0)),
                  _const_spec(woa.shape), _const_spec(wob.shape), _const_spec(wout.shape),
                  _const_spec((1, d)), _const_spec(w_in.shape), _const_spec(w_down.shape),
                  _const_spec((1, d))],
        out_specs=row(d),
        out_shape=jax.ShapeDtypeStruct((n, d), F32),
        compiler_params=pltpu.CompilerParams(vmem_limit_bytes=VMEM_LIMIT),
        name="tail",
    )(x1, oa, ob, gates, mod3, woa, wob, wout, g2, w_in, w_down, gf)


def _pad_heads(w, heads, width):
    k = w.shape[0]
    w3 = w.reshape(k, heads, width)
    return jnp.pad(w3, ((0, 0), (0, 0), (0, LANES - width))).reshape(k, heads * LANES)


def _layout_w_in(w_in, d):
    z = lambda c: jnp.zeros((d, c), w_in.dtype)
    sizes = (A_Q_W, A_HEAD_DIM, A_HEAD_DIM, I_Q_W, IDX_DIM, IDX_HEADS, Q_LORA, KV_LORA,
             QK_ROPE, 2 * d)
    starts = np.cumsum((0,) + sizes)
    qa, ka, va, qi, ki, wi, cq, ckv, kr, gates = (
        w_in[:, a:a + s] for a, s in zip(starts, sizes))
    qa = qa * (A_HEAD_DIM ** -0.5 * LOG2E)
    hr = QK_ROPE // 2
    kt = jnp.concatenate([-kr[:, hr:], kr[:, :hr]], axis=1)
    half = LANES // 2
    tail = LANES - QK_NOPE - QK_ROPE
    cols = [qa, qi,
            ka, z(half), z(half), ka, va, z(half), ki, z(half), z(half), ki,
            wi, z(LANES - IDX_HEADS),
            z(QK_NOPE), kr, z(tail), z(QK_NOPE), kt, z(tail),
            cq, ckv, gates]
    return jnp.concatenate(cols, axis=1).astype(BF16)


def _layout_w_uq(w_uq):
    scale = (QK_NOPE + QK_ROPE) ** -0.5 * LOG2E
    dh = QK_NOPE + QK_ROPE
    hr = QK_ROPE // 2
    src_plain = np.zeros(B_HEADS * LANES, np.int32)
    mul_plain = np.zeros(B_HEADS * LANES, np.float32)
    src_rot = np.zeros(B_HEADS * LANES, np.int32)
    mul_rot = np.zeros(B_HEADS * LANES, np.float32)
    for h in range(B_HEADS):
        for c in range(dh):
            src_plain[h * LANES + c] = h * dh + c
            mul_plain[h * LANES + c] = scale
        for c in range(QK_ROPE):
            first = c < hr
            src_rot[h * LANES + QK_NOPE + c] = h * dh + QK_NOPE + (c + hr if first else c - hr)
            mul_rot[h * LANES + QK_NOPE + c] = -scale if first else scale
    plain = jnp.take(w_uq, src_plain, axis=1) * mul_plain
    rotw = jnp.take(w_uq, src_rot, axis=1) * mul_rot
    return plain.astype(BF16), rotw.astype(BF16)


def _rope_dense(pos):
    half = QK_ROPE // 2
    assert SUBLANES * half == LANES
    freqs = ROPE_THETA ** (-2.0 * jnp.arange(half, dtype=F32) / QK_ROPE)
    pos_dense = jnp.repeat(pos.reshape(-1, SUBLANES, 1)[:, ::-1, :], half,
                           axis=2).reshape(-1, LANES)
    return pos_dense, jnp.tile(freqs, SUBLANES).reshape(1, LANES)


def _fit_keys_values(q_max, k_max, v_max, k, v_t, batch, per_batch_rows):
    n = k.shape[0]
    tiny = jnp.float32(1e-30)
    q_max, k_max, v_max = (jnp.maximum(m, tiny) for m in (q_max, k_max, v_max))
    c = jnp.sqrt(q_max / k_max)
    k8 = jnp.clip(k.astype(F32) * jnp.repeat(c, n // batch)[:, None], -F8_MAX, F8_MAX).astype(F8)
    v_scale = F8_FIT / v_max
    if per_batch_rows:
        v_scale = jnp.repeat(v_scale, v_t.shape[0] // batch)[:, None]
    else:
        v_scale = jnp.repeat(v_scale, v_t.shape[1] // batch)[None, :]
    ones = (jnp.arange(v_t.shape[0]) % LANES == V_DIM).astype(F32)[:, None]
    v8_t = (v_t.astype(F32) * v_scale + ones).astype(F8)
    return k8, v8_t, jnp.stack([1.0 / c, v_max / F8_FIT])


def _tiles(seq):
    tm = 512 if seq % 512 == 0 else 256
    tq_dsa = 512 if seq % 512 == 0 else 256
    tq_mla = 512 if seq % 512 == 0 else 256
    return tm, tq_dsa, tq_mla


def kernel(x, c, positions, w_ada, b_ada, g_ffn1, w_ffn1_in, w_ffn1_down, g_mix, w_in,
           g_cq, w_uq, g_ckv, w_uk, w_uv, rel_bias, w_o_a, w_o_b, w_out,
           g_ffn2, w_ffn2_in, w_ffn2_down, g_final):
    batch, seq, d = x.shape
    depth = w_ada.shape[0]
    n = batch * seq
    topk = min(TOPK_MAX, seq // 4)
    tm, tq_dsa, tq_mla = _tiles(seq)
    assert seq % tq_dsa == 0 and seq % tm == 0 and d % LANES == 0 and seq <= ZERO_BAND
    assert topk <= GROUPS and tq_dsa % GROUPS == 0
    assert A_HEAD_DIM == IDX_DIM == V_DIM == LANES // 2
    assert depth == 1, "the tail kernel fuses the final norm into the single layer"

    xf = x.reshape(n, d)
    c_pad = jnp.pad(c, ((0, 8 - batch % 8 if batch % 8 else 0), (0, 0)))
    pos = positions.astype(I32)
    posq = pos.reshape(n // tq_dsa, 1, tq_dsa)
    posk = pos.reshape(batch, 1, seq)
    ptile = pos.reshape(n // tq_dsa, tq_dsa)
    pmin, pmax = ptile.min(axis=1), ptile.max(axis=1)
    prun = jnp.all(ptile[:, 1:] - ptile[:, :-1] == 1, axis=1).astype(I32)
    tab = jnp.pad(rel_bias.T.astype(F32) * LOG2E, ((0, 0), (0, LANES - NUM_BUCKETS)))
    far = rel_bias[NUM_BUCKETS // 2 - 1].astype(F32) * LOG2E

    l = 0
    mod = _ada(c_pad, w_ada[l], b_ada[l])[:batch]
    mod3 = mod.reshape(batch, N_ADA, d)
    x1 = _ffn1(xf, mod3, g_ffn1[l].reshape(1, d), w_ffn1_in[l].astype(BF16),
               w_ffn1_down[l].astype(BF16), seq, tm)
    wq, wqr = _layout_w_uq(w_uq[l])
    qa, qi, ska, ski, vat, wi, qb, kb, vbt, gates, amax = _proj(
        x1, mod3, g_mix[l].reshape(1, d), _layout_w_in(w_in[l], d),
        g_cq[l].reshape(1, Q_LORA), wq, wqr, g_ckv[l].reshape(1, KV_LORA),
        _pad_heads(w_uk[l], B_HEADS, QK_NOPE).astype(BF16),
        _pad_heads(w_uv[l], B_HEADS, V_DIM).astype(BF16),
        *_rope_dense(pos), seq, tm)
    bmax = amax.reshape(batch, seq // tm, SUBLANES, LANES)[:, :, :, 0].max(axis=1)
    ska8, vat8, fit_a = _fit_keys_values(bmax[:, 0], bmax[:, 1], bmax[:, 2], ska, vat, batch,
                                         per_batch_rows=False)
    kb8, vbt8, fit_b = _fit_keys_values(bmax[:, 3], bmax[:, 4], bmax[:, 5], kb, vbt, batch,
                                        per_batch_rows=True)
    oa = _dsa(qa, qi, wi, ska8, ski, vat8, posq, posk, tab, far, fit_a, pmin, pmax, prun, batch,
              seq, tq_dsa, topk)
    ob = _mla(qb, kb8, vbt8, fit_b, batch, seq, tq_mla)
    out = _tail(x1, oa, ob, gates, mod3,
                w_o_a[l].astype(BF16), w_o_b[l].astype(BF16),
                w_out[l].astype(BF16), g_ffn2[l].reshape(1, d),
                w_ffn2_in[l].astype(BF16), w_ffn2_down[l].astype(BF16),
                g_final.reshape(1, d), seq, tm)
    return out.reshape(batch, seq, d)
```

```python
import functools
import math

import jax
import jax.numpy as jnp
import numpy as np
from jax import lax
from jax.experimental import pallas as pl
from jax.experimental.pallas import tpu as pltpu

F32 = jnp.float32
BF16 = jnp.bfloat16
I32 = jnp.int32

CHUNK = 64
EPS = 1e-6
A_HEADS = 8
A_HEAD_DIM = 64
IDX_HEADS = 8
IDX_DIM = 64
TOPK_MAX = 256
B_HEADS = 8
QK_NOPE = 64
QK_ROPE = 32
V_DIM = 64
Q_LORA = 384
KV_LORA = 256
ROPE_THETA = 10000.0
NUM_BUCKETS = 32
MAX_DISTANCE = 128
N_ADA = 9

LANES = 128
SUBLANES = 8
VMEM_LIMIT = 56 * 1024 * 1024
INT_MIN = -2 ** 31
INT_MAX = 2 ** 31 - 1
NEG_BIG = -1e30
LOG2E = math.log2(math.e)
F8 = jnp.float8_e4m3fn
P_SHIFT = 8.0
F8_MAX = 448.0
F8_FIT = 224.0
F8_ROWS = 32
PV_ROWS_F8 = -(-(V_DIM + 1) // F8_ROWS) * F8_ROWS

A_Q_W = A_HEADS * A_HEAD_DIM
I_Q_W = IDX_HEADS * IDX_DIM
PAIR_W = 2 * LANES
SM_W = 5 * LANES
MISC_W = 3 * LANES
C_QA = 0
C_QI = C_QA + A_Q_W
C_SM = C_QI + I_Q_W
C_WI = C_SM + SM_W
C_CQ = C_WI + MISC_W
C_CKV = C_CQ + Q_LORA
C_GT = C_CKV + KV_LORA


def _dot(a, b):
    return jnp.dot(a, b, preferred_element_type=F32)


def _dot_nt(a, b):
    return lax.dot_general(a, b, (((1,), (1,)), ((), ())), preferred_element_type=F32)


def _two_phase(n_groups, phase1, phase2, ahead=1):
    for g in range(min(ahead, n_groups)):
        phase1(g)
    for g in range(n_groups):
        if g + ahead < n_groups:
            phase1(g + ahead)
        phase2(g)


def _fit_queries(q, scale):
    return jnp.clip(q.astype(F32) * scale, -F8_MAX, F8_MAX).astype(F8)


def _store_heads(acc_ref, o_ref, heads, unscale):
    def normalised(h):
        a = acc_ref[h]
        return (a * (unscale / a[V_DIM:V_DIM + 1, :])).T[:, 0:V_DIM]

    for h in range(0, heads, 2):
        pair = jnp.concatenate([normalised(h), normalised(h + 1)], axis=1)
        o_ref[:, h * V_DIM:(h + 2) * V_DIM] = pair.astype(BF16)


def _rms(x, g):
    return x * lax.rsqrt(jnp.mean(x * x, axis=-1, keepdims=True) + EPS) * g


def _const_spec(shape):
    nd = len(shape)
    return pl.BlockSpec(shape, lambda *_: (0,) * nd, pipeline_mode=pl.Buffered(1))


def _ada_kernel(c_ref, w_ref, b_ref, o_ref):
    c = c_ref[...]
    s = c * jax.nn.sigmoid(c)
    o_ref[...] = _dot(s.astype(BF16), w_ref[...].astype(BF16)) + b_ref[...]


def _ada(c_pad, w_ada, b_ada):
    d = c_pad.shape[1]
    n_out = w_ada.shape[1]
    return pl.pallas_call(
        _ada_kernel,
        grid=(n_out // d,),
        in_specs=[pl.BlockSpec(c_pad.shape, lambda j: (0, 0)),
                  pl.BlockSpec((d, d), lambda j: (0, j)),
                  pl.BlockSpec((1, d), lambda j: (0, j))],
        out_specs=pl.BlockSpec(c_pad.shape, lambda j: (0, j)),
        out_shape=jax.ShapeDtypeStruct((c_pad.shape[0], n_out), F32),
        compiler_params=pltpu.CompilerParams(vmem_limit_bytes=VMEM_LIMIT),
        name="ada",
    )(c_pad, w_ada, b_ada.reshape(1, n_out))


def _swiglu_tile(hb, win_ref, wdn_ref, d_ff, chunk):
    acc = None
    for j in range(d_ff // chunk):
        g = _dot(hb, win_ref[:, j * chunk:(j + 1) * chunk])
        u = _dot(hb, win_ref[:, d_ff + j * chunk:d_ff + (j + 1) * chunk])
        a = (g * jax.nn.sigmoid(g) * u).astype(BF16)
        part = _dot(a, wdn_ref[j * chunk:(j + 1) * chunk, :])
        acc = part if acc is None else acc + part
    return acc


def _ffn_chunk(d_ff):
    for c in (512, 256, 128):
        if d_ff % c == 0:
            return c
    raise ValueError("d_ff must be a multiple of 128")


def _ffn1_kernel(x_ref, mod_ref, g_ref, win_ref, wdn_ref, o_ref, *, d_ff, chunk):
    x = x_ref[...]
    sh, sc, gt = mod_ref[0, 0:1, :], mod_ref[0, 1:2, :], mod_ref[0, 2:3, :]
    h = _rms(x, g_ref[...]) * (1.0 + sc) + sh
    y = _swiglu_tile(h.astype(BF16), win_ref, wdn_ref, d_ff, chunk)
    o_ref[...] = x + (0.5 * gt) * y


def _ffn1(x2, mod3, g, w_in, w_down, seq, tm):
    n, d = x2.shape
    d_ff = w_down.shape[0]
    tiles_per_batch = seq // tm
    kern = functools.partial(_ffn1_kernel, d_ff=d_ff, chunk=_ffn_chunk(d_ff))
    return pl.pallas_call(
        kern,
        grid=(n // tm,),
        in_specs=[pl.BlockSpec((tm, d), lambda i: (i, 0)),
                  pl.BlockSpec((1, N_ADA, d), lambda i: (i // tiles_per_batch, 0, 0)),
                  _const_spec((1, d)),
                  _const_spec(w_in.shape),
                  _const_spec(w_down.shape)],
        out_specs=pl.BlockSpec((tm, d), lambda i: (i, 0)),
        out_shape=jax.ShapeDtypeStruct((n, d), F32),
        compiler_params=pltpu.CompilerParams(vmem_limit_bytes=VMEM_LIMIT),
        name="ffn1",
    )(x2, mod3, g, w_in, w_down)


def _proj_kernel(x_ref, mod_ref, g_ref, w_ref, gcq_ref, wq_ref, wqr_ref, gckv_ref,
                 wk_ref, wv_ref, pos_ref, freq_ref,
                 qa_ref, qi_ref, ska_ref, ski_ref, vat_ref, wi_ref, qb_ref, kb_ref, vbt_ref, gt_ref,
                 amax_ref, *, d):
    x = x_ref[...]
    sh, sc = mod_ref[0, 3:4, :], mod_ref[0, 4:5, :]
    hb = (_rms(x, g_ref[...]) * (1.0 + sc) + sh).astype(BF16)
    lane = lax.broadcasted_iota(I32, (1, LANES), 1)
    tm = x.shape[0]
    half = QK_ROPE // 2
    ang = pos_ref[...].astype(F32) * freq_ref[...]

    def spread(dense):
        rows = jnp.broadcast_to(dense[:, None, :], (tm // SUBLANES, SUBLANES, LANES))
        top = pltpu.roll(rows.reshape(tm, LANES), 0, 1, stride=half, stride_axis=0)
        first = pltpu.roll(top, QK_NOPE + half, 1)
        return first, pltpu.roll(first, half, 1)

    in_first = (lane >= QK_NOPE) & (lane < QK_NOPE + half)
    in_second = (lane >= QK_NOPE + half) & (lane < QK_NOPE + QK_ROPE)
    c1, c2 = spread(jnp.cos(ang))
    s1, s2 = spread(jnp.sin(ang))
    cs = jnp.where(lane < QK_NOPE, 1.0, jnp.where(in_first, c1, jnp.where(in_second, c2, 0.0)))
    sn = jnp.where(in_first, s1, jnp.where(in_second, s2, 0.0))

    def tile_max(a):
        col = jnp.max(jnp.abs(a), axis=0, keepdims=True)
        return jnp.broadcast_to(jnp.max(col, axis=1, keepdims=True), (1, LANES))

    qa = _dot(hb, w_ref[:, C_QA:C_QA + A_Q_W])
    qa_ref[...] = qa.astype(BF16)
    qi_ref[...] = _dot(hb, w_ref[:, C_QI:C_QI + I_Q_W]).astype(BF16)

    sm = _dot(hb, w_ref[:, C_SM:C_SM + SM_W])
    ska_ref[...] = sm[:, 0:PAIR_W].astype(BF16)
    va = sm[:, PAIR_W:PAIR_W + LANES]
    vat_ref[...] = va.T.astype(BF16)
    ski_ref[...] = sm[:, PAIR_W + LANES:SM_W].astype(BF16)
    maxima = [tile_max(qa), tile_max(sm[:, 0:LANES]), tile_max(va)]

    misc = _dot(hb, w_ref[:, C_WI:C_WI + MISC_W])
    wi_ref[...] = misc[:, 0:LANES]
    kpe = misc[:, LANES:2 * LANES] * cs + misc[:, 2 * LANES:3 * LANES] * sn

    lat = _dot(hb, w_ref[:, C_CQ:C_GT])
    cq = _rms(lat[:, 0:Q_LORA], gcq_ref[...]).astype(BF16)
    ckv = _rms(lat[:, Q_LORA:Q_LORA + KV_LORA], gckv_ref[...]).astype(BF16)
    qn = _dot(cq, wq_ref[...])
    qr = _dot(cq, wqr_ref[...])
    kn = _dot(ckv, wk_ref[...])
    vv = _dot(ckv, wv_ref[...])
    q_abs = k_abs = None
    for h in range(B_HEADS):
        sl = slice(h * LANES, (h + 1) * LANES)
        qh = qn[:, sl] * cs + qr[:, sl] * sn
        kh = kn[:, sl] + kpe
        qb_ref[:, sl] = qh.astype(BF16)
        kb_ref[:, sl] = kh.astype(BF16)
        q_abs = jnp.abs(qh) if q_abs is None else jnp.maximum(q_abs, jnp.abs(qh))
        k_abs = jnp.abs(kh) if k_abs is None else jnp.maximum(k_abs, jnp.abs(kh))
    vbt_ref[...] = vv.T.astype(BF16)
    maxima += [tile_max(q_abs), tile_max(k_abs), tile_max(vv)]
    amax_ref[...] = jnp.concatenate(
        maxima + [jnp.zeros((SUBLANES - len(maxima), LANES), F32)], axis=0)

    gl = _dot(hb, w_ref[:, C_GT:C_GT + 2 * d])
    gt_ref[...] = jax.nn.sigmoid(gl).astype(BF16)


def _proj(x1, mod3, g, w_all, gcq, wq, wqr, gckv, wk, wv, pos_col, freq_row, seq, tm):
    n, d = x1.shape
    tiles_per_batch = seq // tm
    row = lambda w: pl.BlockSpec((tm, w), lambda i: (i, 0))
    slots = B_HEADS * LANES
    n_tiles = n // tm
    outs = [(row(A_Q_W), (n, A_Q_W), BF16),
            (row(I_Q_W), (n, I_Q_W), BF16),
            (row(PAIR_W), (n, PAIR_W), BF16),
            (row(PAIR_W), (n, PAIR_W), BF16),
            (pl.BlockSpec((LANES, tm), lambda i: (0, i)), (LANES, n), BF16),
            (row(LANES), (n, LANES), F32),
            (row(slots), (n, slots), BF16),
            (row(slots), (n, slots), BF16),
            (pl.BlockSpec((slots, tm), lambda i: (i // tiles_per_batch, i % tiles_per_batch)),
             (n // seq * slots, seq), BF16),
            (row(2 * d), (n, 2 * d), BF16),
            (pl.BlockSpec((SUBLANES, LANES), lambda i: (i, 0)),
             (n_tiles * SUBLANES, LANES), F32)]
    return pl.pallas_call(
        functools.partial(_proj_kernel, d=d),
        grid=(n // tm,),
        in_specs=[row(d),
                  pl.BlockSpec((1, N_ADA, d), lambda i: (i // tiles_per_batch, 0, 0)),
                  _const_spec((1, d)), _const_spec(w_all.shape),
                  _const_spec(gcq.shape), _const_spec(wq.shape), _const_spec(wqr.shape),
                  _const_spec(gckv.shape), _const_spec(wk.shape), _const_spec(wv.shape),
                  pl.BlockSpec((tm // SUBLANES, LANES), lambda i: (i, 0)),
                  _const_spec((1, LANES))],
        out_specs=[o[0] for o in outs],
        out_shape=[jax.ShapeDtypeStruct(o[1], o[2]) for o in outs],
        compiler_params=pltpu.CompilerParams(vmem_limit_bytes=VMEM_LIMIT),
        name="proj",
    )(x1, mod3, g, w_all, gcq, wq, wqr, gckv, wk, wv, pos_col, freq_row)


DSA_HEAD_GROUP = 2
DSA_AHEAD = 1
DSA_SLOTS = (DSA_AHEAD + 1) * DSA_HEAD_GROUP
ZERO_BAND = 1 << 20
GROUPS = 256
COUNT_UNKNOWN = 1 << 30
FINISH_STEPS = 2


def _rowsum8(x):
    v = x.reshape(x.shape[0] // SUBLANES, SUBLANES, x.shape[1])
    while v.shape[0] > 1:
        half = v.shape[0] // 2
        v = v[:half] + v[half:]
    return v[0]


def _t5_bias_tiles(posk, posq, tab_ref, heads):
    return _t5_bias_of_rel(posk - posq, tab_ref, heads)


def _t5_bias_of_rel(rel, tab_ref, heads):
    half = NUM_BUCKETS // 2
    max_exact = half // 2
    n = jnp.abs(rel)
    nf = jnp.maximum(n, 1).astype(F32)
    large = max_exact + (jnp.log(nf / max_exact) / math.log(MAX_DISTANCE / max_exact)
                         * (half - max_exact)).astype(I32)
    large = jnp.minimum(large, half - 1)
    bucket = jnp.where(rel > 0, half, 0) + jnp.where(n < max_exact, n, large)
    tk, tq = bucket.shape
    out = []
    for h in range(heads):
        tab = jnp.broadcast_to(tab_ref[h:h + 1, :], (tk, LANES))
        cols = [jnp.take_along_axis(tab, bucket[:, c * LANES:(c + 1) * LANES], axis=1,
                                    mode="promise_in_bounds")
                for c in range(tq // LANES)]
        out.append(jnp.concatenate(cols, axis=1))
    return out


def _dsa_kernel(pmin_ref, pmax_ref, prun_ref,
                qa_ref, qi_ref, wi_ref, ska_ref, ski_ref, vat_ref, posq_ref, posk_ref, tab_ref,
                far_ref, fit_ref,
                o_ref,
                keys_ref, gmax_ref, m_ref, acc_ref, s_ref, q8_ref, *, tq, topk, nq):
    b = pl.program_id(0)
    i = pl.program_id(1)
    tk = tq
    n_tiles = i + 1

    w_t = wi_ref[...].T[0:IDX_HEADS, :]
    key_l = lax.broadcasted_iota(I32, (tk, tq), 0)
    q_l = lax.broadcasted_iota(I32, (tk, tq), 1)
    diag_ok = key_l < (q_l // CHUNK + 1) * CHUNK
    not_key_l = ~key_l

    def score_tile(j, diag):
        r0 = pl.multiple_of(j * tk, tk)
        ki_lo = ski_ref[pl.ds(r0, tk), 0:128]
        ki_hi = ski_ref[pl.ds(r0, tk), 128:256]
        score = None
        for p in range(IDX_HEADS // 2):
            qp = qi_ref[:, p * LANES:(p + 1) * LANES]
            for h, kk in ((2 * p, ki_lo), (2 * p + 1, ki_hi)):
                t = w_t[h:h + 1, :] * jnp.maximum(_dot_nt(kk, qp), 0.0)
                score = t if score is None else score + t
        bits = lax.bitcast_convert_type(score, I32)
        key = jnp.where(bits < 0, (bits ^ 0x7FFFFFFF) - ZERO_BAND, bits)
        key = jnp.where(score == 0.0, not_key_l - j * tk, key)
        if diag:
            key = jnp.where(diag_ok, key, INT_MIN)
        keys_ref[pl.ds(r0, tk), :] = key
        gmax_ref[...] = jnp.maximum(gmax_ref[...],
                                    jnp.max(key.reshape(tk // GROUPS, GROUPS, tq), axis=0))

    gmax_ref[...] = jnp.full(gmax_ref.shape, INT_MIN, I32)
    lax.fori_loop(0, i, lambda j, c: (score_tile(j, False), c)[1], 0)
    score_tile(i, True)

    def count_tiles(pred):
        def body(j, acc):
            r0 = pl.multiple_of(j * tk, tk)
            return acc + _rowsum8(jnp.where(pred(keys_ref[pl.ds(r0, tk), :], j), 1, 0))
        acc = lax.fori_loop(0, n_tiles, body, jnp.zeros((SUBLANES, tq), I32))
        return jnp.sum(acc, axis=0, keepdims=True)

    q_row = lax.broadcasted_iota(I32, (1, tq), 1)
    n_adm = i * tq + (q_row // CHUNK + 1) * CHUNK

    gmax = gmax_ref[...]
    top = jnp.max(gmax, axis=0, keepdims=True)
    low = jnp.min(gmax, axis=0, keepdims=True)
    cnt_pos = count_tiles(lambda k_, j: k_ >= 1)
    cnt_zero = count_tiles(lambda k_, j: k_ >= -ZERO_BAND)
    few = n_adm <= topk
    in_pos = cnt_pos >= topk
    in_zero = jnp.logical_not(in_pos) & (cnt_zero >= topk)
    base_lo = jnp.where(in_pos, 1, jnp.where(in_zero, -ZERO_BAND, INT_MIN + 1))
    base_cnt = jnp.where(in_pos, cnt_pos, jnp.where(in_zero, cnt_zero, n_adm))
    hi0 = jnp.where(in_pos, top + 1, jnp.where(in_zero, 1, -ZERO_BAND))
    tighter = low > base_lo
    lo0 = jnp.where(few, INT_MIN + 1, jnp.where(tighter, low, base_lo))
    cnt0 = jnp.where(few, topk, jnp.where(tighter, COUNT_UNKNOWN, base_cnt))
    hi0 = jnp.where(few, INT_MIN + 2, hi0)

    def any_row(pred):
        return jnp.max(jnp.where(pred, 1, 0))

    def bisect(lo, hi, cnt_lo, slack):
        def pending(lo, hi, cnt_lo):
            return any_row((cnt_lo - topk > slack) & (hi - lo > 1))

        def step(st):
            it, lo, hi, cnt_lo, _ = st
            mid = lo + ((hi - lo) >> 1)
            cnt = count_tiles(lambda k_, j: k_ >= mid)
            take = cnt >= topk
            lo = jnp.where(take, mid, lo)
            hi = jnp.where(take, hi, mid)
            cnt_lo = jnp.where(take, cnt, cnt_lo)
            return it + 1, lo, hi, cnt_lo, pending(lo, hi, cnt_lo)

        return lax.while_loop(lambda st: (st[0] < 34) & (st[4] > 0), step,
                              (jnp.int32(0), lo, hi, cnt_lo, pending(lo, hi, cnt_lo)))

    _, lo1, hi1, cnt1, _ = bisect(lo0, hi0, cnt0, FINISH_STEPS)

    def surplus(lo, cnt_lo):
        return (cnt_lo > topk) & (cnt_lo - topk <= FINISH_STEPS) & (hi1 - lo > 1)

    def drop_step(st):
        it, lo, cnt_lo, _ = st
        def body(j, acc):
            r0 = pl.multiple_of(j * tk, tk)
            k_ = keys_ref[pl.ds(r0, tk), :]
            cand = jnp.where(k_ >= lo, k_, INT_MAX)
            v = cand.reshape(tk // SUBLANES, SUBLANES, tq)
            while v.shape[0] > 1:
                half = v.shape[0] // 2
                v = jnp.minimum(v[:half], v[half:])
            return jnp.minimum(acc, v[0])
        acc = lax.fori_loop(0, n_tiles, body, jnp.full((SUBLANES, tq), INT_MAX, I32))
        smallest = jnp.min(acc, axis=0, keepdims=True)
        go = surplus(lo, cnt_lo)
        lo = jnp.where(go, smallest + 1, lo)
        cnt_lo = jnp.where(go, cnt_lo - 1, cnt_lo)
        return it + 1, lo, cnt_lo, any_row(surplus(lo, cnt_lo))

    _, lo2, _, _ = lax.while_loop(lambda st: (st[0] < FINISH_STEPS) & (st[3] > 0), drop_step,
                                  (jnp.int32(0), lo1, cnt1, any_row(surplus(lo1, cnt1))))

    cnt2 = count_tiles(lambda k_, j: k_ >= lo2)
    redo = (cnt2 != topk) & jnp.logical_not(few)
    its, tau, _, _, _ = bisect(jnp.where(redo, lo1, lo2), jnp.where(redo, hi1, lo2 + 1),
                               jnp.where(redo, cnt1, topk), 0)
    cnt_ge = lax.cond(its > 0, lambda: count_tiles(lambda k_, j: k_ >= tau), lambda: cnt2)
    over = (cnt_ge > topk) & jnp.logical_not(few)
    tau_sel = tau

    @pl.when(jnp.max(jnp.where(over, 1, 0)) > 0)
    def _ties():
        cnt_gt = count_tiles(lambda k_, j: k_ > tau)
        room = topk - cnt_gt

        def idx_step(p, jmax):
            cand = jmax | jnp.left_shift(jnp.int32(1), 14 - p)
            cnt = count_tiles(lambda k_, j: (k_ == tau) & (key_l + j * tk < cand))
            return jnp.where(cnt <= room, cand, jmax)

        jmax = lax.fori_loop(0, 15, idx_step, jnp.zeros((1, tq), I32))

        def demote(j, carry):
            r0 = pl.multiple_of(j * tk, tk)
            k_ = keys_ref[pl.ds(r0, tk), :]
            drop = (k_ == tau) & (key_l + j * tk >= jmax) & over
            keys_ref[pl.ds(r0, tk), :] = jnp.where(drop, k_ - 1, k_)
            return carry

        lax.fori_loop(0, n_tiles, demote, 0)

    m_ref[...] = jnp.full(m_ref.shape, NEG_BIG, F32)
    acc_ref[...] = jnp.zeros(acc_ref.shape, F32)
    posq = posq_ref[0]
    q8_ref[...] = _fit_queries(qa_ref[...], fit_ref[0, b])

    def attn_tile(j, mode):
        near = mode != "far"
        r0 = pl.multiple_of(j * tk, tk)
        ka_lo = ska_ref[pl.ds(r0, tk), 0:128]
        ka_hi = ska_ref[pl.ds(r0, tk), 128:256]
        va_t = vat_ref[0:PV_ROWS_F8, pl.ds(r0, tk)]
        maskb = jnp.where(keys_ref[pl.ds(r0, tk), :] >= tau_sel, 0.0, NEG_BIG)
        if mode == "any":
            posk_row = lax.bitcast_convert_type(posk_ref[0, :, pl.ds(r0, tk)], F32)
            posk_col = lax.bitcast_convert_type(
                jnp.broadcast_to(posk_row, (LANES, tk)).T, I32)
            bias = _t5_bias_tiles(jnp.tile(posk_col, (1, tq // LANES)), posq, tab_ref, A_HEADS)
        elif mode == "run":
            nb = tq // LANES
            shift_max = (nb - 1) * LANES
            d0 = pmin_ref[b * nq + j] - pmin_ref[b * nq + i]
            x = lax.broadcasted_iota(I32, (tk + shift_max, LANES), 0)
            l = lax.broadcasted_iota(I32, (tk + shift_max, LANES), 1)
            panel = _t5_bias_of_rel(x - l + (d0 - shift_max), tab_ref, A_HEADS)
            bias = [jnp.concatenate([p[shift_max - cb * LANES:shift_max - cb * LANES + tk]
                                     for cb in range(nb)], axis=1) for p in panel]
        m_prev = m_ref[...]
        m_next = [None] * A_HEADS
        slot = lambda h: h % DSA_SLOTS

        def logits(g):
            for h in range(g * DSA_HEAD_GROUP, (g + 1) * DSA_HEAD_GROUP):
                s = _dot_nt(ka_hi if h % 2 else ka_lo,
                            q8_ref[:, (h // 2) * LANES:(h // 2 + 1) * LANES])
                s = s + ((bias[h] + maskb) if near else maskb)
                s_ref[slot(h)] = s
                s_max = jnp.max(s, axis=0, keepdims=True)
                m_next[h] = jnp.maximum(m_prev[h:h + 1, :],
                                        s_max if near else s_max + far_ref[h])

        def weighted_values(g):
            for h in range(g * DSA_HEAD_GROUP, (g + 1) * DSA_HEAD_GROUP):
                alpha = jnp.exp2(m_prev[h:h + 1, :] - m_next[h])
                shift = m_next[h] - (P_SHIFT if near else P_SHIFT + far_ref[h])
                pr = jnp.exp2(s_ref[slot(h)] - shift).astype(F8)
                acc_ref[h, 0:PV_ROWS_F8, :] = (alpha * acc_ref[h, 0:PV_ROWS_F8, :]
                                               + _dot(va_t, pr))

        _two_phase(A_HEADS // DSA_HEAD_GROUP, logits, weighted_values, ahead=DSA_AHEAD)
        m_ref[...] = jnp.concatenate(m_next, axis=0)

    def attn_body(j, carry):
        is_far = pmax_ref[b * nq + j] - pmin_ref[b * nq + i] <= -MAX_DISTANCE
        is_run = (prun_ref[b * nq + j] == 1) & (prun_ref[b * nq + i] == 1)
        lax.cond(is_far, lambda: attn_tile(j, "far"),
                 lambda: lax.cond(is_run, lambda: attn_tile(j, "run"),
                                  lambda: attn_tile(j, "any")))
        return carry

    lax.fori_loop(0, n_tiles, attn_body, 0)
    _store_heads(acc_ref, o_ref, A_HEADS, fit_ref[1, b])


def _dsa(qa, qi, wi, ska, ski, vat, posq, posk, tab, far, fit, pmin, pmax, prun, batch, seq,
         tq, topk):
    n = qa.shape[0]
    nq = seq // tq
    row = lambda w: pl.BlockSpec((tq, w), lambda b, i, *_: (b * nq + i, 0))
    grid_spec = pltpu.PrefetchScalarGridSpec(
        num_scalar_prefetch=3,
        grid=(batch, nq),
        in_specs=[row(A_Q_W), row(I_Q_W), row(LANES),
                  pl.BlockSpec((seq, PAIR_W), lambda b, i, *_: (b, 0),
                               pipeline_mode=pl.Buffered(1)),
                  pl.BlockSpec((seq, PAIR_W), lambda b, i, *_: (b, 0),
                               pipeline_mode=pl.Buffered(1)),
                  pl.BlockSpec((LANES, seq), lambda b, i, *_: (0, b)),
                  pl.BlockSpec((1, 1, tq), lambda b, i, *_: (b * nq + i, 0, 0)),
                  pl.BlockSpec((1, 1, seq), lambda b, i, *_: (b, 0, 0)),
                  pl.BlockSpec((A_HEADS, LANES), lambda b, i, *_: (0, 0)),
                  pl.BlockSpec(memory_space=pltpu.SMEM),
                  pl.BlockSpec(memory_space=pltpu.SMEM)],
        out_specs=row(A_HEADS * V_DIM),
        scratch_shapes=[pltpu.VMEM((seq, tq), I32),
                        pltpu.VMEM((GROUPS, tq), I32),
                        pltpu.VMEM((A_HEADS, tq), F32),
                        pltpu.VMEM((A_HEADS, LANES, tq), F32),
                        pltpu.VMEM((DSA_SLOTS, tq, tq), F32),
                        pltpu.VMEM((tq, A_Q_W), F8)],
    )
    return pl.pallas_call(
        functools.partial(_dsa_kernel, tq=tq, topk=topk, nq=nq),
        grid_spec=grid_spec,
        out_shape=jax.ShapeDtypeStruct((n, A_HEADS * V_DIM), BF16),
        compiler_params=pltpu.CompilerParams(vmem_limit_bytes=VMEM_LIMIT),
        name="dsa",
    )(pmin, pmax, prun, qa, qi, wi, ska, ski, vat, posq, posk, tab, far, fit)


MLA_HEAD_GROUP = 8
MLA_AHEAD = 2


def _mla_kernel(q_ref, k_ref, vt_ref, fit_ref, o_ref, acc_ref, s_ref, q8_ref, *, tq):
    b = pl.program_id(0)
    i = pl.program_id(2)
    tk = tq
    hg = MLA_HEAD_GROUP
    acc_ref[...] = jnp.zeros(acc_ref.shape, F32)
    q8_ref[...] = _fit_queries(q_ref[...], fit_ref[0, b])

    ahead = MLA_AHEAD
    key_l = lax.broadcasted_iota(I32, (tk, tq), 0)
    q_l = lax.broadcasted_iota(I32, (tk, tq), 1)
    ok = key_l < (q_l // CHUNK + 1) * CHUNK
    head = lambda h: slice(h * LANES, (h + 1) * LANES)

    def logits(j, h, masked):
        r0 = j * tk if isinstance(j, int) else pl.multiple_of(j * tk, tk)
        s = _dot_nt(k_ref[pl.ds(r0, tk), head(h)], q8_ref[:, head(h)])
        if masked is True:
            s = jnp.where(ok, s, NEG_BIG)
        elif masked is not False:
            s = jnp.where(ok | jnp.logical_not(masked), s, NEG_BIG)
        s_ref[h] = s
        return jnp.max(s, axis=0, keepdims=True)

    def weighted_values(j, h, m_prev, s_max):
        r0 = pl.multiple_of(j * tk, tk)
        m_next = jnp.maximum(m_prev, s_max)
        alpha = jnp.exp2(m_prev - m_next)
        pr = jnp.exp2(s_ref[h] - (m_next - P_SHIFT)).astype(F8)
        v_t = vt_ref[h * LANES:h * LANES + PV_ROWS_F8, pl.ds(r0, tk)]
        acc_ref[h, 0:PV_ROWS_F8, :] = alpha * acc_ref[h, 0:PV_ROWS_F8, :] + _dot(v_t, pr)
        return m_next

    def tile(j, ms, lead, last, next_last=False):
        s_max = list(lead) + [None] * (hg - ahead)
        nxt = []
        out = []
        for h in range(hg):
            if h + ahead < hg:
                s_max[h + ahead] = logits(j, h + ahead, last)
            elif not last:
                nxt.append(logits(j + 1, h + ahead - hg, next_last))
            out.append(weighted_values(j, h, ms[h], s_max[h]))
        return tuple(out), tuple(nxt)

    m0 = tuple(jnp.full((1, tq), NEG_BIG, F32) for _ in range(hg))
    lead0 = tuple(logits(0, h, i == 0) for h in range(ahead))
    st = lax.fori_loop(0, jnp.maximum(i - 1, 0),
                       lambda j, st: tile(j, st[0], st[1], False), (m0, lead0))
    ms, lead = lax.cond(i >= 1, lambda: tile(i - 1, st[0], st[1], False, True), lambda: st)
    tile(i, ms, lead, True)
    _store_heads(acc_ref, o_ref, hg, fit_ref[1, b])


def _mla(qb, kb, vbt, fit, batch, seq, tq):
    n = qb.shape[0]
    nq = seq // tq
    hg = MLA_HEAD_GROUP
    ng = B_HEADS // hg
    w = hg * LANES
    return pl.pallas_call(
        functools.partial(_mla_kernel, tq=tq),
        grid=(batch, ng, nq),
        in_specs=[pl.BlockSpec((tq, w), lambda b, g, i: (b * nq + i, g)),
                  pl.BlockSpec((seq, w), lambda b, g, i: (b, g), pipeline_mode=pl.Buffered(1)),
                  pl.BlockSpec((w, seq), lambda b, g, i: (b * ng + g, 0),
                               pipeline_mode=pl.Buffered(1)),
                  pl.BlockSpec(memory_space=pltpu.SMEM)],
        out_specs=pl.BlockSpec((tq, hg * V_DIM), lambda b, g, i: (b * nq + i, g)),
        out_shape=jax.ShapeDtypeStruct((n, B_HEADS * V_DIM), BF16),
        scratch_shapes=[pltpu.VMEM((hg, LANES, tq), F32), pltpu.VMEM((hg, tq, tq), F32),
                        pltpu.VMEM((tq, w), F8)],
        compiler_params=pltpu.CompilerParams(vmem_limit_bytes=VMEM_LIMIT),
        name="mla",
    )(qb, kb, vbt, fit)


def _tail_kernel(x_ref, oa_ref, ob_ref, gt_ref, mod_ref, woa_ref, wob_ref, wout_ref,
                 g2_ref, win_ref, wdn_ref, gf_ref, o_ref, *, d, d_ff, chunk):
    x = x_ref[...]
    gt2 = mod_ref[0, 5:6, :]
    sh3, sc3, gt3 = mod_ref[0, 6:7, :], mod_ref[0, 7:8, :], mod_ref[0, 8:9, :]
    ya = _dot(oa_ref[...], woa_ref[...])
    yb = _dot(ob_ref[...], wob_ref[...])
    y = gt_ref[:, 0:d].astype(F32) * ya + gt_ref[:, d:2 * d].astype(F32) * yb
    x = x + gt2 * _dot(y.astype(BF16), wout_ref[...])
    h = _rms(x, g2_ref[...]) * (1.0 + sc3) + sh3
    f = _swiglu_tile(h.astype(BF16), win_ref, wdn_ref, d_ff, chunk)
    x = x + (0.5 * gt3) * f
    o_ref[...] = _rms(x, gf_ref[...])


def _tail(x1, oa, ob, gates, mod3, woa, wob, wout, g2, w_in, w_down, gf, seq, tm):
    n, d = x1.shape
    d_ff = w_down.shape[0]
    tiles_per_batch = seq // tm
    row = lambda w: pl.BlockSpec((tm, w), lambda i: (i, 0))
    kern = functools.partial(_tail_kernel, d=d, d_ff=d_ff, chunk=_ffn_chunk(d_ff))
    return pl.pallas_call(
        kern,
        grid=(n // tm,),
        in_specs=[row(d), row(oa.shape[1]), row(ob.shape[1]), row(2 * d),
                  pl.BlockSpec((1, N_ADA, d), lambda i: (i // tiles_per_batch, 0, 0)),
                  _const_spec(woa.shape), _const_spec(wob.shape), _const_spec(wout.shape),
                  _const_spec((1, d)), _const_spec(w_in.shape), _const_spec(w_down.shape),
                  _const_spec((1, d))],
        out_specs=row(d),
        out_shape=jax.ShapeDtypeStruct((n, d), F32),
        compiler_params=pltpu.CompilerParams(vmem_limit_bytes=VMEM_LIMIT),
        name="tail",
    )(x1, oa, ob, gates, mod3, woa, wob, wout, g2, w_in, w_down, gf)


def _pad_heads(w, heads, width):
    k = w.shape[0]
    w3 = w.reshape(k, heads, width)
    return jnp.pad(w3, ((0, 0), (0, 0), (0, LANES - width))).reshape(k, heads * LANES)


def _layout_w_in(w_in, d):
    z = lambda c: jnp.zeros((d, c), w_in.dtype)
    sizes = (A_Q_W, A_HEAD_DIM, A_HEAD_DIM, I_Q_W, IDX_DIM, IDX_HEADS, Q_LORA, KV_LORA,
             QK_ROPE, 2 * d)
    starts = np.cumsum((0,) + sizes)
    qa, ka, va, qi, ki, wi, cq, ckv, kr, gates = (
        w_in[:, a:a + s] for a, s in zip(starts, sizes))
    qa = qa * (A_HEAD_DIM ** -0.5 * LOG2E)
    hr = QK_ROPE // 2
    kt = jnp.concatenate([-kr[:, hr:], kr[:, :hr]], axis=1)
    half = LANES // 2
    tail = LANES - QK_NOPE - QK_ROPE
    cols = [qa, qi,
            ka, z(half), z(half), ka, va, z(half), ki, z(half), z(half), ki,
            wi, z(LANES - IDX_HEADS),
            z(QK_NOPE), kr, z(tail), z(QK_NOPE), kt, z(tail),
            cq, ckv, gates]
    return jnp.concatenate(cols, axis=1).astype(BF16)


def _layout_w_uq(w_uq):
    scale = (QK_NOPE + QK_ROPE) ** -0.5 * LOG2E
    dh = QK_NOPE + QK_ROPE
    hr = QK_ROPE // 2
    src_plain = np.zeros(B_HEADS * LANES, np.int32)
    mul_plain = np.zeros(B_HEADS * LANES, np.float32)
    src_rot = np.zeros(B_HEADS * LANES, np.int32)
    mul_rot = np.zeros(B_HEADS * LANES, np.float32)
    for h in range(B_HEADS):
        for c in range(dh):
            src_plain[h * LANES + c] = h * dh + c
            mul_plain[h * LANES + c] = scale
        for c in range(QK_ROPE):
            first = c < hr
            src_rot[h * LANES + QK_NOPE + c] = h * dh + QK_NOPE + (c + hr if first else c - hr)
            mul_rot[h * LANES + QK_NOPE + c] = -scale if first else scale
    plain = jnp.take(w_uq, src_plain, axis=1) * mul_plain
    rotw = jnp.take(w_uq, src_rot, axis=1) * mul_rot
    return plain.astype(BF16), rotw.astype(BF16)


def _rope_dense(pos):
    half = QK_ROPE // 2
    assert SUBLANES * half == LANES
    freqs = ROPE_THETA ** (-2.0 * jnp.arange(half, dtype=F32) / QK_ROPE)
    pos_dense = jnp.repeat(pos.reshape(-1, SUBLANES, 1)[:, ::-1, :], half,
                           axis=2).reshape(-1, LANES)
    return pos_dense, jnp.tile(freqs, SUBLANES).reshape(1, LANES)


def _fit_keys_values(q_max, k_max, v_max, k, v_t, batch, per_batch_rows):
    n = k.shape[0]
    tiny = jnp.float32(1e-30)
    q_max, k_max, v_max = (jnp.maximum(m, tiny) for m in (q_max, k_max, v_max))
    c = jnp.sqrt(q_max / k_max)
    k8 = jnp.clip(k.astype(F32) * jnp.repeat(c, n // batch)[:, None], -F8_MAX, F8_MAX).astype(F8)
    v_scale = F8_FIT / v_max
    if per_batch_rows:
        v_scale = jnp.repeat(v_scale, v_t.shape[0] // batch)[:, None]
    else:
        v_scale = jnp.repeat(v_scale, v_t.shape[1] // batch)[None, :]
    ones = (jnp.arange(v_t.shape[0]) % LANES == V_DIM).astype(F32)[:, None]
    v8_t = (v_t.astype(F32) * v_scale + ones).astype(F8)
    return k8, v8_t, jnp.stack([1.0 / c, v_max / F8_FIT])


def _tiles(seq):
    tm = 512 if seq % 512 == 0 else 256
    tq_dsa = 512 if seq % 512 == 0 else 256
    tq_mla = 512 if seq % 512 == 0 else 256
    return tm, tq_dsa, tq_mla


def kernel(x, c, positions, w_ada, b_ada, g_ffn1, w_ffn1_in, w_ffn1_down, g_mix, w_in,
           g_cq, w_uq, g_ckv, w_uk, w_uv, rel_bias, w_o_a, w_o_b, w_out,
           g_ffn2, w_ffn2_in, w_ffn2_down, g_final):
    batch, seq, d = x.shape
    depth = w_ada.shape[0]
    n = batch * seq
    topk = min(TOPK_MAX, seq // 4)
    tm, tq_dsa, tq_mla = _tiles(seq)
    assert seq % tq_dsa == 0 and seq % tm == 0 and d % LANES == 0 and seq <= ZERO_BAND
    assert topk <= GROUPS and tq_dsa % GROUPS == 0
    assert A_HEAD_DIM == IDX_DIM == V_DIM == LANES // 2
    assert depth == 1, "the tail kernel fuses the final norm into the single layer"

    xf = x.reshape(n, d)
    c_pad = jnp.pad(c, ((0, 8 - batch % 8 if batch % 8 else 0), (0, 0)))
    pos = positions.astype(I32)
    posq = pos.reshape(n // tq_dsa, 1, tq_dsa)
    posk = pos.reshape(batch, 1, seq)
    ptile = pos.reshape(n // tq_dsa, tq_dsa)
    pmin, pmax = ptile.min(axis=1), ptile.max(axis=1)
    prun = jnp.all(ptile[:, 1:] - ptile[:, :-1] == 1, axis=1).astype(I32)
    tab = jnp.pad(rel_bias.T.astype(F32) * LOG2E, ((0, 0), (0, LANES - NUM_BUCKETS)))
    far = rel_bias[NUM_BUCKETS // 2 - 1].astype(F32) * LOG2E

    l = 0
    mod = _ada(c_pad, w_ada.reshape(w_ada.shape[1:]), b_ada.reshape(-1))[:batch]
    mod3 = mod.reshape(batch, N_ADA, d)
    x1 = _ffn1(xf, mod3, g_ffn1[l].reshape(1, d), w_ffn1_in[l].astype(BF16),
               w_ffn1_down[l].astype(BF16), seq, tm)
    wq, wqr = _layout_w_uq(w_uq[l])
    qa, qi, ska, ski, vat, wi, qb, kb, vbt, gates, amax = _proj(
        x1, mod3, g_mix[l].reshape(1, d), _layout_w_in(w_in[l], d),
        g_cq[l].reshape(1, Q_LORA), wq, wqr, g_ckv[l].reshape(1, KV_LORA),
        _pad_heads(w_uk[l], B_HEADS, QK_NOPE).astype(BF16),
        _pad_heads(w_uv[l], B_HEADS, V_DIM).astype(BF16),
        *_rope_dense(pos), seq, tm)
    bmax = amax.reshape(batch, seq // tm, SUBLANES, LANES)[:, :, :, 0].max(axis=1)
    ska8, vat8, fit_a = _fit_keys_values(bmax[:, 0], bmax[:, 1], bmax[:, 2], ska, vat, batch,
                                         per_batch_rows=False)
    kb8, vbt8, fit_b = _fit_keys_values(bmax[:, 3], bmax[:, 4], bmax[:, 5], kb, vbt, batch,
                                        per_batch_rows=True)
    oa = _dsa(qa, qi, wi, ska8, ski, vat8, posq, posk, tab, far, fit_a, pmin, pmax, prun, batch,
              seq, tq_dsa, topk)
    ob = _mla(qb, kb8, vbt8, fit_b, batch, seq, tq_mla)
    out = _tail(x1, oa, ob, gates, mod3,
                w_o_a[l].astype(BF16), w_o_b[l].astype(BF16),
                w_out[l].astype(BF16), g_ffn2[l].reshape(1, d),
                w_ffn2_in[l].astype(BF16), w_ffn2_down[l].astype(BF16),
                g_final.reshape(1, d), seq, tm)
    return out.reshape(batch, seq, d)
```

```python
import functools
import math

import jax
import jax.numpy as jnp
import numpy as np
from jax import lax
from jax.experimental import pallas as pl
from jax.experimental.pallas import tpu as pltpu

F32 = jnp.float32
BF16 = jnp.bfloat16
I32 = jnp.int32

CHUNK = 64
EPS = 1e-6
A_HEADS = 8
A_HEAD_DIM = 64
IDX_HEADS = 8
IDX_DIM = 64
TOPK_MAX = 256
B_HEADS = 8
QK_NOPE = 64
QK_ROPE = 32
V_DIM = 64
Q_LORA = 384
KV_LORA = 256
ROPE_THETA = 10000.0
NUM_BUCKETS = 32
MAX_DISTANCE = 128
N_ADA = 9

LANES = 128
SUBLANES = 8
VMEM_LIMIT = 56 * 1024 * 1024
INT_MIN = -2 ** 31
INT_MAX = 2 ** 31 - 1
NEG_BIG = -1e30
LOG2E = math.log2(math.e)
F8 = jnp.float8_e4m3fn
P_SHIFT = 8.0
F8_MAX = 448.0
F8_FIT = 224.0
F8_ROWS = 32
PV_ROWS_F8 = -(-(V_DIM + 1) // F8_ROWS) * F8_ROWS

A_Q_W = A_HEADS * A_HEAD_DIM
I_Q_W = IDX_HEADS * IDX_DIM
PAIR_W = 2 * LANES
SM_W = 5 * LANES
MISC_W = 3 * LANES
C_QA = 0
C_QI = C_QA + A_Q_W
C_SM = C_QI + I_Q_W
C_WI = C_SM + SM_W
C_CQ = C_WI + MISC_W
C_CKV = C_CQ + Q_LORA
C_GT = C_CKV + KV_LORA


def _dot(a, b):
    return jnp.dot(a, b, preferred_element_type=F32)


def _dot_nt(a, b):
    return lax.dot_general(a, b, (((1,), (1,)), ((), ())), preferred_element_type=F32)


def _two_phase(n_groups, phase1, phase2, ahead=1):
    for g in range(min(ahead, n_groups)):
        phase1(g)
    for g in range(n_groups):
        if g + ahead < n_groups:
            phase1(g + ahead)
        phase2(g)


def _fit_queries(q, scale):
    return jnp.clip(q.astype(F32) * scale, -F8_MAX, F8_MAX).astype(F8)


def _store_heads(acc_ref, o_ref, heads, unscale):
    def normalised(h):
        a = acc_ref[h]
        return (a * (unscale / a[V_DIM:V_DIM + 1, :])).T[:, 0:V_DIM]

    for h in range(0, heads, 2):
        pair = jnp.concatenate([normalised(h), normalised(h + 1)], axis=1)
        o_ref[:, h * V_DIM:(h + 2) * V_DIM] = pair.astype(BF16)


def _rms(x, g):
    return x * lax.rsqrt(jnp.mean(x * x, axis=-1, keepdims=True) + EPS) * g


def _const_spec(shape):
    nd = len(shape)
    return pl.BlockSpec(shape, lambda *_: (0,) * nd, pipeline_mode=pl.Buffered(1))


def _ada_kernel(c_ref, w_ref, b_ref, o_ref):
    c = c_ref[...]
    s = c * jax.nn.sigmoid(c)
    o_ref[...] = _dot(s.astype(BF16), w_ref[...].astype(BF16)) + b_ref[...]


def _ada(c_pad, w_ada, b_ada):
    d = c_pad.shape[1]
    n_out = w_ada.shape[1]
    return pl.pallas_call(
        _ada_kernel,
        grid=(n_out // d,),
        in_specs=[pl.BlockSpec(c_pad.shape, lambda j: (0, 0)),
                  pl.BlockSpec((d, d), lambda j: (0, j)),
                  pl.BlockSpec((1, d), lambda j: (0, j))],
        out_specs=pl.BlockSpec(c_pad.shape, lambda j: (0, j)),
        out_shape=jax.ShapeDtypeStruct((c_pad.shape[0], n_out), F32),
        compiler_params=pltpu.CompilerParams(vmem_limit_bytes=VMEM_LIMIT),
        name="ada",
    )(c_pad, w_ada, b_ada.reshape(1, n_out))


def _swiglu_tile(hb, win_ref, wdn_ref, d_ff, chunk):
    acc = None
    for j in range(d_ff // chunk):
        g = _dot(hb, win_ref[:, j * chunk:(j + 1) * chunk])
        u = _dot(hb, win_ref[:, d_ff + j * chunk:d_ff + (j + 1) * chunk])
        a = (g * jax.nn.sigmoid(g) * u).astype(BF16)
        part = _dot(a, wdn_ref[j * chunk:(j + 1) * chunk, :])
        acc = part if acc is None else acc + part
    return acc


def _ffn_chunk(d_ff):
    for c in (512, 256, 128):
        if d_ff % c == 0:
            return c
    raise ValueError("d_ff must be a multiple of 128")


def _ffn1_kernel(x_ref, mod_ref, g_ref, win_ref, wdn_ref, o_ref, *, d_ff, chunk):
    x = x_ref[...]
    sh, sc, gt = mod_ref[0, 0:1, :], mod_ref[0, 1:2, :], mod_ref[0, 2:3, :]
    h = _rms(x, g_ref[...]) * (1.0 + sc) + sh
    y = _swiglu_tile(h.astype(BF16), win_ref, wdn_ref, d_ff, chunk)
    o_ref[...] = x + (0.5 * gt) * y


def _ffn1(x2, mod3, g, w_in, w_down, seq, tm):
    n, d = x2.shape
    d_ff = w_down.shape[0]
    tiles_per_batch = seq // tm
    kern = functools.partial(_ffn1_kernel, d_ff=d_ff, chunk=_ffn_chunk(d_ff))
    return pl.pallas_call(
        kern,
        grid=(n // tm,),
        in_specs=[pl.BlockSpec((tm, d), lambda i: (i, 0)),
                  pl.BlockSpec((1, N_ADA, d), lambda i: (i // tiles_per_batch, 0, 0)),
                  _const_spec((1, d)),
                  _const_spec(w_in.shape),
                  _const_spec(w_down.shape)],
        out_specs=pl.BlockSpec((tm, d), lambda i: (i, 0)),
        out_shape=jax.ShapeDtypeStruct((n, d), F32),
        compiler_params=pltpu.CompilerParams(vmem_limit_bytes=VMEM_LIMIT),
        name="ffn1",
    )(x2, mod3, g, w_in, w_down)


def _proj_kernel(x_ref, mod_ref, g_ref, w_ref, gcq_ref, wq_ref, wqr_ref, gckv_ref,
                 wk_ref, wv_ref, pos_ref, freq_ref,
                 qa_ref, qi_ref, ska_ref, ski_ref, vat_ref, wi_ref, qb_ref, kb_ref, vbt_ref, gt_ref,
                 amax_ref, *, d):
    x = x_ref[...]
    sh, sc = mod_ref[0, 3:4, :], mod_ref[0, 4:5, :]
    hb = (_rms(x, g_ref[...]) * (1.0 + sc) + sh).astype(BF16)
    lane = lax.broadcasted_iota(I32, (1, LANES), 1)
    tm = x.shape[0]
    half = QK_ROPE // 2
    ang = pos_ref[...].astype(F32) * freq_ref[...]

    def spread(dense):
        rows = jnp.broadcast_to(dense[:, None, :], (tm // SUBLANES, SUBLANES, LANES))
        top = pltpu.roll(rows.reshape(tm, LANES), 0, 1, stride=half, stride_axis=0)
        first = pltpu.roll(top, QK_NOPE + half, 1)
        return first, pltpu.roll(first, half, 1)

    in_first = (lane >= QK_NOPE) & (lane < QK_NOPE + half)
    in_second = (lane >= QK_NOPE + half) & (lane < QK_NOPE + QK_ROPE)
    c1, c2 = spread(jnp.cos(ang))
    s1, s2 = spread(jnp.sin(ang))
    cs = jnp.where(lane < QK_NOPE, 1.0, jnp.where(in_first, c1, jnp.where(in_second, c2, 0.0)))
    sn = jnp.where(in_first, s1, jnp.where(in_second, s2, 0.0))

    def tile_max(a):
        col = jnp.max(jnp.abs(a), axis=0, keepdims=True)
        return jnp.broadcast_to(jnp.max(col, axis=1, keepdims=True), (1, LANES))

    qa = _dot(hb, w_ref[:, C_QA:C_QA + A_Q_W])
    qa_ref[...] = qa.astype(BF16)
    qi_ref[...] = _dot(hb, w_ref[:, C_QI:C_QI + I_Q_W]).astype(BF16)

    sm = _dot(hb, w_ref[:, C_SM:C_SM + SM_W])
    ska_ref[...] = sm[:, 0:PAIR_W].astype(BF16)
    va = sm[:, PAIR_W:PAIR_W + LANES]
    vat_ref[...] = va.T.astype(BF16)
    ski_ref[...] = sm[:, PAIR_W + LANES:SM_W].astype(BF16)
    maxima = [tile_max(qa), tile_max(sm[:, 0:LANES]), tile_max(va)]

    misc = _dot(hb, w_ref[:, C_WI:C_WI + MISC_W])
    wi_ref[...] = misc[:, 0:LANES]
    kpe = misc[:, LANES:2 * LANES] * cs + misc[:, 2 * LANES:3 * LANES] * sn

    lat = _dot(hb, w_ref[:, C_CQ:C_GT])
    cq = _rms(lat[:, 0:Q_LORA], gcq_ref[...]).astype(BF16)
    ckv = _rms(lat[:, Q_LORA:Q_LORA + KV_LORA], gckv_ref[...]).astype(BF16)
    qn = _dot(cq, wq_ref[...])
    qr = _dot(cq, wqr_ref[...])
    kn = _dot(ckv, wk_ref[...])
    vv = _dot(ckv, wv_ref[...])
    q_abs = k_abs = None
    for h in range(B_HEADS):
        sl = slice(h * LANES, (h + 1) * LANES)
        qh = qn[:, sl] * cs + qr[:, sl] * sn
        kh = kn[:, sl] + kpe
        qb_ref[:, sl] = qh.astype(BF16)
        kb_ref[:, sl] = kh.astype(BF16)
        q_abs = jnp.abs(qh) if q_abs is None else jnp.maximum(q_abs, jnp.abs(qh))
        k_abs = jnp.abs(kh) if k_abs is None else jnp.maximum(k_abs, jnp.abs(kh))
    vbt_ref[...] = vv.T.astype(BF16)
    maxima += [tile_max(q_abs), tile_max(k_abs), tile_max(vv)]
    amax_ref[...] = jnp.concatenate(
        maxima + [jnp.zeros((SUBLANES - len(maxima), LANES), F32)], axis=0)

    gl = _dot(hb, w_ref[:, C_GT:C_GT + 2 * d])
    gt_ref[...] = jax.nn.sigmoid(gl).astype(BF16)


def _proj(x1, mod3, g, w_all, gcq, wq, wqr, gckv, wk, wv, pos_col, freq_row, seq, tm):
    n, d = x1.shape
    tiles_per_batch = seq // tm
    row = lambda w: pl.BlockSpec((tm, w), lambda i: (i, 0))
    slots = B_HEADS * LANES
    n_tiles = n // tm
    outs = [(row(A_Q_W), (n, A_Q_W), BF16),
            (row(I_Q_W), (n, I_Q_W), BF16),
            (row(PAIR_W), (n, PAIR_W), BF16),
            (row(PAIR_W), (n, PAIR_W), BF16),
            (pl.BlockSpec((LANES, tm), lambda i: (0, i)), (LANES, n), BF16),
            (row(LANES), (n, LANES), F32),
            (row(slots), (n, slots), BF16),
            (row(slots), (n, slots), BF16),
            (pl.BlockSpec((slots, tm), lambda i: (i // tiles_per_batch, i % tiles_per_batch)),
             (n // seq * slots, seq), BF16),
            (row(2 * d), (n, 2 * d), BF16),
            (pl.BlockSpec((SUBLANES, LANES), lambda i: (i, 0)),
             (n_tiles * SUBLANES, LANES), F32)]
    return pl.pallas_call(
        functools.partial(_proj_kernel, d=d),
        grid=(n // tm,),
        in_specs=[row(d),
                  pl.BlockSpec((1, N_ADA, d), lambda i: (i // tiles_per_batch, 0, 0)),
                  _const_spec((1, d)), _const_spec(w_all.shape),
                  _const_spec(gcq.shape), _const_spec(wq.shape), _const_spec(wqr.shape),
                  _const_spec(gckv.shape), _const_spec(wk.shape), _const_spec(wv.shape),
                  pl.BlockSpec((tm // SUBLANES, LANES), lambda i: (i, 0)),
                  _const_spec((1, LANES))],
        out_specs=[o[0] for o in outs],
        out_shape=[jax.ShapeDtypeStruct(o[1], o[2]) for o in outs],
        compiler_params=pltpu.CompilerParams(vmem_limit_bytes=VMEM_LIMIT),
        name="proj",
    )(x1, mod3, g, w_all, gcq, wq, wqr, gckv, wk, wv, pos_col, freq_row)


DSA_HEAD_GROUP = 4
DSA_AHEAD = 0
DSA_SLOTS = (DSA_AHEAD + 1) * DSA_HEAD_GROUP
ZERO_BAND = 1 << 20
GROUPS = 256
COUNT_UNKNOWN = 1 << 30
FINISH_STEPS = 2


def _rowsum8(x):
    v = x.reshape(x.shape[0] // SUBLANES, SUBLANES, x.shape[1])
    while v.shape[0] > 1:
        half = v.shape[0] // 2
        v = v[:half] + v[half:]
    return v[0]


def _t5_bias_tiles(posk, posq, tab_ref, heads):
    return _t5_bias_of_rel(posk - posq, tab_ref, heads)


def _t5_bias_of_rel(rel, tab_ref, heads):
    half = NUM_BUCKETS // 2
    max_exact = half // 2
    n = jnp.abs(rel)
    nf = jnp.maximum(n, 1).astype(F32)
    large = max_exact + (jnp.log(nf / max_exact) / math.log(MAX_DISTANCE / max_exact)
                         * (half - max_exact)).astype(I32)
    large = jnp.minimum(large, half - 1)
    bucket = jnp.where(rel > 0, half, 0) + jnp.where(n < max_exact, n, large)
    tk, tq = bucket.shape
    out = []
    for h in range(heads):
        tab = jnp.broadcast_to(tab_ref[h:h + 1, :], (tk, LANES))
        cols = [jnp.take_along_axis(tab, bucket[:, c * LANES:(c + 1) * LANES], axis=1,
                                    mode="promise_in_bounds")
                for c in range(tq // LANES)]
        out.append(jnp.concatenate(cols, axis=1))
    return out


def _dsa_kernel(pmin_ref, pmax_ref, prun_ref,
                qa_ref, qi_ref, wi_ref, ska_ref, ski_ref, vat_ref, posq_ref, posk_ref, tab_ref,
                far_ref, fit_ref,
                o_ref,
                keys_ref, gmax_ref, m_ref, acc_ref, s_ref, q8_ref, *, tq, topk, nq):
    b = pl.program_id(0)
    i = pl.program_id(1)
    tk = tq
    n_tiles = i + 1

    w_t = wi_ref[...].T[0:IDX_HEADS, :]
    key_l = lax.broadcasted_iota(I32, (tk, tq), 0)
    q_l = lax.broadcasted_iota(I32, (tk, tq), 1)
    diag_ok = key_l < (q_l // CHUNK + 1) * CHUNK
    not_key_l = ~key_l

    def score_tile(j, diag):
        r0 = pl.multiple_of(j * tk, tk)
        ki_lo = ski_ref[pl.ds(r0, tk), 0:128]
        ki_hi = ski_ref[pl.ds(r0, tk), 128:256]
        score = None
        for p in range(IDX_HEADS // 2):
            qp = qi_ref[:, p * LANES:(p + 1) * LANES]
            for h, kk in ((2 * p, ki_lo), (2 * p + 1, ki_hi)):
                t = w_t[h:h + 1, :] * jnp.maximum(_dot_nt(kk, qp), 0.0)
                score = t if score is None else score + t
        score = score + 0.0
        bits = lax.bitcast_convert_type(score, I32)
        key = jnp.where(bits < 0, (bits ^ 0x7FFFFFFF) - ZERO_BAND, bits)
        key = jnp.where(score == 0.0, not_key_l - j * tk, key)
        if diag:
            key = jnp.where(diag_ok, key, INT_MIN)
        keys_ref[pl.ds(r0, tk), :] = key
        gmax_ref[...] = jnp.maximum(gmax_ref[...],
                                    jnp.max(key.reshape(tk // GROUPS, GROUPS, tq), axis=0))

    gmax_ref[...] = jnp.full(gmax_ref.shape, INT_MIN, I32)
    lax.fori_loop(0, i, lambda j, c: (score_tile(j, False), c)[1], 0)
    score_tile(i, True)

    def count_tiles(pred):
        def body(j, acc):
            r0 = pl.multiple_of(j * tk, tk)
            return acc + _rowsum8(jnp.where(pred(keys_ref[pl.ds(r0, tk), :], j), 1, 0))
        acc = lax.fori_loop(0, n_tiles, body, jnp.zeros((SUBLANES, tq), I32))
        return jnp.sum(acc, axis=0, keepdims=True)

    q_row = lax.broadcasted_iota(I32, (1, tq), 1)
    n_adm = i * tq + (q_row // CHUNK + 1) * CHUNK

    gmax = gmax_ref[...]
    top = jnp.max(gmax, axis=0, keepdims=True)
    low = jnp.min(gmax, axis=0, keepdims=True)
    cnt_pos = count_tiles(lambda k_, j: k_ >= 1)
    cnt_zero = count_tiles(lambda k_, j: k_ >= -ZERO_BAND)
    few = n_adm <= topk
    in_pos = cnt_pos >= topk
    in_zero = jnp.logical_not(in_pos) & (cnt_zero >= topk)
    base_lo = jnp.where(in_pos, 1, jnp.where(in_zero, -ZERO_BAND, INT_MIN + 1))
    base_cnt = jnp.where(in_pos, cnt_pos, jnp.where(in_zero, cnt_zero, n_adm))
    hi0 = jnp.where(in_pos, top + 1, jnp.where(in_zero, 1, -ZERO_BAND))
    tighter = low > base_lo
    lo0 = jnp.where(few, INT_MIN + 1, jnp.where(tighter, low, base_lo))
    cnt0 = jnp.where(few, topk, jnp.where(tighter, COUNT_UNKNOWN, base_cnt))
    hi0 = jnp.where(few, INT_MIN + 2, hi0)

    def any_row(pred):
        return jnp.max(jnp.where(pred, 1, 0))

    def bisect(lo, hi, cnt_lo, slack):
        def pending(lo, hi, cnt_lo):
            return any_row((cnt_lo - topk > slack) & (hi - lo > 1))

        def step(st):
            it, lo, hi, cnt_lo, _ = st
            mid = lo + ((hi - lo) >> 1)
            cnt = count_tiles(lambda k_, j: k_ >= mid)
            take = cnt >= topk
            lo = jnp.where(take, mid, lo)
            hi = jnp.where(take, hi, mid)
            cnt_lo = jnp.where(take, cnt, cnt_lo)
            return it + 1, lo, hi, cnt_lo, pending(lo, hi, cnt_lo)

        return lax.while_loop(lambda st: (st[0] < 34) & (st[4] > 0), step,
                              (jnp.int32(0), lo, hi, cnt_lo, pending(lo, hi, cnt_lo)))

    _, lo1, hi1, cnt1, _ = bisect(lo0, hi0, cnt0, FINISH_STEPS)

    def surplus(lo, cnt_lo):
        return (cnt_lo > topk) & (cnt_lo - topk <= FINISH_STEPS) & (hi1 - lo > 1)

    def drop_step(st):
        it, lo, cnt_lo, _ = st
        def body(j, acc):
            r0 = pl.multiple_of(j * tk, tk)
            k_ = keys_ref[pl.ds(r0, tk), :]
            cand = jnp.where(k_ >= lo, k_, INT_MAX)
            v = cand.reshape(tk // SUBLANES, SUBLANES, tq)
            while v.shape[0] > 1:
                half = v.shape[0] // 2
                v = jnp.minimum(v[:half], v[half:])
            return jnp.minimum(acc, v[0])
        acc = lax.fori_loop(0, n_tiles, body, jnp.full((SUBLANES, tq), INT_MAX, I32))
        smallest = jnp.min(acc, axis=0, keepdims=True)
        go = surplus(lo, cnt_lo)
        lo = jnp.where(go, smallest + 1, lo)
        cnt_lo = jnp.where(go, cnt_lo - 1, cnt_lo)
        return it + 1, lo, cnt_lo, any_row(surplus(lo, cnt_lo))

    _, lo2, _, _ = lax.while_loop(lambda st: (st[0] < FINISH_STEPS) & (st[3] > 0), drop_step,
                                  (jnp.int32(0), lo1, cnt1, any_row(surplus(lo1, cnt1))))

    cnt2 = count_tiles(lambda k_, j: k_ >= lo2)
    redo = (cnt2 != topk) & jnp.logical_not(few)
    its, tau, _, _, _ = bisect(jnp.where(redo, lo1, lo2), jnp.where(redo, hi1, lo2 + 1),
                               jnp.where(redo, cnt1, topk), 0)
    cnt_ge = lax.cond(its > 0, lambda: count_tiles(lambda k_, j: k_ >= tau), lambda: cnt2)
    over = (cnt_ge > topk) & jnp.logical_not(few)
    tau_sel = tau

    @pl.when(jnp.max(jnp.where(over, 1, 0)) > 0)
    def _ties():
        cnt_gt = count_tiles(lambda k_, j: k_ > tau)
        room = topk - cnt_gt

        def idx_step(p, jmax):
            cand = jmax | jnp.left_shift(jnp.int32(1), 14 - p)
            cnt = count_tiles(lambda k_, j: (k_ == tau) & (key_l + j * tk < cand))
            return jnp.where(cnt <= room, cand, jmax)

        jmax = lax.fori_loop(0, 15, idx_step, jnp.zeros((1, tq), I32))

        def demote(j, carry):
            r0 = pl.multiple_of(j * tk, tk)
            k_ = keys_ref[pl.ds(r0, tk), :]
            drop = (k_ == tau) & (key_l + j * tk >= jmax) & over
            keys_ref[pl.ds(r0, tk), :] = jnp.where(drop, k_ - 1, k_)
            return carry

        lax.fori_loop(0, n_tiles, demote, 0)

    m_ref[...] = jnp.full(m_ref.shape, NEG_BIG, F32)
    acc_ref[...] = jnp.zeros(acc_ref.shape, F32)
    posq = posq_ref[0]
    q8_ref[...] = _fit_queries(qa_ref[...], fit_ref[0, b])

    def attn_tile(j, mode):
        near = mode != "far"
        r0 = pl.multiple_of(j * tk, tk)
        ka_lo = ska_ref[pl.ds(r0, tk), 0:128]
        ka_hi = ska_ref[pl.ds(r0, tk), 128:256]
        va_t = vat_ref[0:PV_ROWS_F8, pl.ds(r0, tk)]
        maskb = jnp.where(keys_ref[pl.ds(r0, tk), :] >= tau_sel, 0.0, NEG_BIG)
        if mode == "any":
            posk_row = lax.bitcast_convert_type(posk_ref[0, :, pl.ds(r0, tk)], F32)
            posk_col = lax.bitcast_convert_type(
                jnp.broadcast_to(posk_row, (LANES, tk)).T, I32)
            bias = _t5_bias_tiles(jnp.tile(posk_col, (1, tq // LANES)), posq, tab_ref, A_HEADS)
        elif mode == "run":
            nb = tq // LANES
            shift_max = (nb - 1) * LANES
            d0 = pmin_ref[b * nq + j] - pmin_ref[b * nq + i]
            x = lax.broadcasted_iota(I32, (tk + shift_max, LANES), 0)
            l = lax.broadcasted_iota(I32, (tk + shift_max, LANES), 1)
            panel = _t5_bias_of_rel(x - l + (d0 - shift_max), tab_ref, A_HEADS)
            bias = [jnp.concatenate([p[shift_max - cb * LANES:shift_max - cb * LANES + tk]
                                     for cb in range(nb)], axis=1) for p in panel]
        m_prev = m_ref[...]
        m_next = [None] * A_HEADS
        slot = lambda h: h % DSA_SLOTS

        def logits(g):
            for h in range(g * DSA_HEAD_GROUP, (g + 1) * DSA_HEAD_GROUP):
                s = _dot_nt(ka_hi if h % 2 else ka_lo,
                            q8_ref[:, (h // 2) * LANES:(h // 2 + 1) * LANES])
                s = s + ((bias[h] + maskb) if near else maskb)
                s_ref[slot(h)] = s
                s_max = jnp.max(s, axis=0, keepdims=True)
                m_next[h] = jnp.maximum(m_prev[h:h + 1, :],
                                        s_max if near else s_max + far_ref[h])

        def weighted_values(g):
            for h in range(g * DSA_HEAD_GROUP, (g + 1) * DSA_HEAD_GROUP):
                alpha = jnp.exp2(m_prev[h:h + 1, :] - m_next[h])
                shift = m_next[h] - (P_SHIFT if near else P_SHIFT + far_ref[h])
                pr = jnp.exp2(s_ref[slot(h)] - shift).astype(F8)
                acc_ref[h, 0:PV_ROWS_F8, :] = (alpha * acc_ref[h, 0:PV_ROWS_F8, :]
                                               + _dot(va_t, pr))

        _two_phase(A_HEADS // DSA_HEAD_GROUP, logits, weighted_values, ahead=DSA_AHEAD)
        m_ref[...] = jnp.concatenate(m_next, axis=0)

    def attn_body(j, carry):
        is_far = pmax_ref[b * nq + j] - pmin_ref[b * nq + i] <= -MAX_DISTANCE
        is_run = (prun_ref[b * nq + j] == 1) & (prun_ref[b * nq + i] == 1)
        lax.cond(is_far, lambda: attn_tile(j, "far"),
                 lambda: lax.cond(is_run, lambda: attn_tile(j, "run"),
                                  lambda: attn_tile(j, "any")))
        return carry

    lax.fori_loop(0, n_tiles, attn_body, 0)
    _store_heads(acc_ref, o_ref, A_HEADS, fit_ref[1, b])


def _dsa(qa, qi, wi, ska, ski, vat, posq, posk, tab, far, fit, pmin, pmax, prun, batch, seq,
         tq, topk):
    n = qa.shape[0]
    nq = seq // tq
    row = lambda w: pl.BlockSpec((tq, w), lambda b, i, *_: (b * nq + i, 0))
    grid_spec = pltpu.PrefetchScalarGridSpec(
        num_scalar_prefetch=3,
        grid=(batch, nq),
        in_specs=[row(A_Q_W), row(I_Q_W), row(LANES),
                  pl.BlockSpec((seq, PAIR_W), lambda b, i, *_: (b, 0),
                               pipeline_mode=pl.Buffered(1)),
                  pl.BlockSpec((seq, PAIR_W), lambda b, i, *_: (b, 0),
                               pipeline_mode=pl.Buffered(1)),
                  pl.BlockSpec((LANES, seq), lambda b, i, *_: (0, b)),
                  pl.BlockSpec((1, 1, tq), lambda b, i, *_: (b * nq + i, 0, 0)),
                  pl.BlockSpec((1, 1, seq), lambda b, i, *_: (b, 0, 0)),
                  pl.BlockSpec((A_HEADS, LANES), lambda b, i, *_: (0, 0)),
                  pl.BlockSpec(memory_space=pltpu.SMEM),
                  pl.BlockSpec(memory_space=pltpu.SMEM)],
        out_specs=row(A_HEADS * V_DIM),
        scratch_shapes=[pltpu.VMEM((seq, tq), I32),
                        pltpu.VMEM((GROUPS, tq), I32),
                        pltpu.VMEM((A_HEADS, tq), F32),
                        pltpu.VMEM((A_HEADS, LANES, tq), F32),
                        pltpu.VMEM((DSA_SLOTS, tq, tq), F32),
                        pltpu.VMEM((tq, A_Q_W), F8)],
    )
    return pl.pallas_call(
        functools.partial(_dsa_kernel, tq=tq, topk=topk, nq=nq),
        grid_spec=grid_spec,
        out_shape=jax.ShapeDtypeStruct((n, A_HEADS * V_DIM), BF16),
        compiler_params=pltpu.CompilerParams(vmem_limit_bytes=VMEM_LIMIT),
        name="dsa",
    )(pmin, pmax, prun, qa, qi, wi, ska, ski, vat, posq, posk, tab, far, fit)


MLA_HEAD_GROUP = 8
MLA_AHEAD = 2


def _mla_kernel(q_ref, k_ref, vt_ref, fit_ref, o_ref, acc_ref, s_ref, q8_ref, *, tq):
    b = pl.program_id(0)
    i = pl.program_id(2)
    tk = tq
    hg = MLA_HEAD_GROUP
    acc_ref[...] = jnp.zeros(acc_ref.shape, F32)
    q8_ref[...] = _fit_queries(q_ref[...], fit_ref[0, b])

    ahead = MLA_AHEAD
    key_l = lax.broadcasted_iota(I32, (tk, tq), 0)
    q_l = lax.broadcasted_iota(I32, (tk, tq), 1)
    ok = key_l < (q_l // CHUNK + 1) * CHUNK
    head = lambda h: slice(h * LANES, (h + 1) * LANES)

    def logits(j, h, masked):
        r0 = j * tk if isinstance(j, int) else pl.multiple_of(j * tk, tk)
        s = _dot_nt(k_ref[pl.ds(r0, tk), head(h)], q8_ref[:, head(h)])
        if masked is True:
            s = jnp.where(ok, s, NEG_BIG)
        elif masked is not False:
            s = jnp.where(ok | jnp.logical_not(masked), s, NEG_BIG)
        s_ref[h] = s
        return jnp.max(s, axis=0, keepdims=True)

    def weighted_values(j, h, m_prev, s_max):
        r0 = pl.multiple_of(j * tk, tk)
        m_next = jnp.maximum(m_prev, s_max)
        alpha = jnp.exp2(m_prev - m_next)
        pr = jnp.exp2(s_ref[h] - (m_next - P_SHIFT)).astype(F8)
        v_t = vt_ref[h * LANES:h * LANES + PV_ROWS_F8, pl.ds(r0, tk)]
        acc_ref[h, 0:PV_ROWS_F8, :] = alpha * acc_ref[h, 0:PV_ROWS_F8, :] + _dot(v_t, pr)
        return m_next

    def tile(j, ms, lead, last, next_last=False):
        s_max = list(lead) + [None] * (hg - ahead)
        nxt = []
        out = []
        for h in range(hg):
            if h + ahead < hg:
                s_max[h + ahead] = logits(j, h + ahead, last)
            elif not last:
                nxt.append(logits(j + 1, h + ahead - hg, next_last))
            out.append(weighted_values(j, h, ms[h], s_max[h]))
        return tuple(out), tuple(nxt)

    m0 = tuple(jnp.full((1, tq), NEG_BIG, F32) for _ in range(hg))
    lead0 = tuple(logits(0, h, i == 0) for h in range(ahead))
    st = lax.fori_loop(0, jnp.maximum(i - 1, 0),
                       lambda j, st: tile(j, st[0], st[1], False), (m0, lead0))
    ms, lead = lax.cond(i >= 1, lambda: tile(i - 1, st[0], st[1], False, True), lambda: st)
    tile(i, ms, lead, True)
    _store_heads(acc_ref, o_ref, hg, fit_ref[1, b])


def _mla(qb, kb, vbt, fit, batch, seq, tq):
    n = qb.shape[0]
    nq = seq // tq
    hg = MLA_HEAD_GROUP
    ng = B_HEADS // hg
    w = hg * LANES
    return pl.pallas_call(
        functools.partial(_mla_kernel, tq=tq),
        grid=(batch, ng, nq),
        in_specs=[pl.BlockSpec((tq, w), lambda b, g, i: (b * nq + i, g)),
                  pl.BlockSpec((seq, w), lambda b, g, i: (b, g), pipeline_mode=pl.Buffered(1)),
                  pl.BlockSpec((w, seq), lambda b, g, i: (b * ng + g, 0),
                               pipeline_mode=pl.Buffered(1)),
                  pl.BlockSpec(memory_space=pltpu.SMEM)],
        out_specs=pl.BlockSpec((tq, hg * V_DIM), lambda b, g, i: (b * nq + i, g)),
        out_shape=jax.ShapeDtypeStruct((n, B_HEADS * V_DIM), BF16),
        scratch_shapes=[pltpu.VMEM((hg, LANES, tq), F32), pltpu.VMEM((hg, tq, tq), F32),
                        pltpu.VMEM((tq, w), F8)],
        compiler_params=pltpu.CompilerParams(vmem_limit_bytes=VMEM_LIMIT),
        name="mla",
    )(qb, kb, vbt, fit)


def _tail_kernel(x_ref, oa_ref, ob_ref, gt_ref, mod_ref, woa_ref, wob_ref, wout_ref,
                 g2_ref, win_ref, wdn_ref, gf_ref, o_ref, *, d, d_ff, chunk):
    x = x_ref[...]
    gt2 = mod_ref[0, 5:6, :]
    sh3, sc3, gt3 = mod_ref[0, 6:7, :], mod_ref[0, 7:8, :], mod_ref[0, 8:9, :]
    ya = _dot(oa_ref[...], woa_ref[...])
    yb = _dot(ob_ref[...], wob_ref[...])
    y = gt_ref[:, 0:d].astype(F32) * ya + gt_ref[:, d:2 * d].astype(F32) * yb
    x = x + gt2 * _dot(y.astype(BF16), wout_ref[...])
    h = _rms(x, g2_ref[...]) * (1.0 + sc3) + sh3
    f = _swiglu_tile(h.astype(BF16), win_ref, wdn_ref, d_ff, chunk)
    x = x + (0.5 * gt3) * f
    o_ref[...] = _rms(x, gf_ref[...])


def _tail(x1, oa, ob, gates, mod3, woa, wob, wout, g2, w_in, w_down, gf, seq, tm):
    n, d = x1.shape
    d_ff = w_down.shape[0]
    tiles_per_batch = seq // tm
    row = lambda w: pl.BlockSpec((tm, w), lambda i: (i, 0))
    kern = functools.partial(_tail_kernel, d=d, d_ff=d_ff, chunk=_ffn_chunk(d_ff))
    return pl.pallas_call(
        kern,
        grid=(n // tm,),
        in_specs=[row(d), row(oa.shape[1]), row(ob.shape[1]), row(2 * d),
                  pl.BlockSpec((1, N_ADA, d), lambda i: (i // tiles_per_batch, 0, 0)),
                  _const_spec(woa.shape), _const_spec(wob.shape), _const_spec(wout.shape),
                  _const_spec((1, d)), _const_spec(w_in.shape), _const_spec(w_down.shape),
                  _const_spec((1, d))],
        out_specs=row(d),
        out_shape=jax.ShapeDtypeStruct((n, d), F32),
        compiler_params=pltpu.CompilerParams(vmem_limit_bytes=VMEM_LIMIT),
        name="tail",
    )(x1, oa, ob, gates, mod3, woa, wob, wout, g2, w_in, w_down, gf)


def _pad_heads(w, heads, width):
    k = w.shape[0]
    w3 = w.reshape(k, heads, width)
    return jnp.pad(w3, ((0, 0), (0, 0), (0, LANES - width))).reshape(k, heads * LANES)


def _layout_w_in(w_in, d):
    z = lambda c: jnp.zeros((d, c), w_in.dtype)
    sizes = (A_Q_W, A_HEAD_DIM, A_HEAD_DIM, I_Q_W, IDX_DIM, IDX_HEADS, Q_LORA, KV_LORA,
             QK_ROPE, 2 * d)
    starts = np.cumsum((0,) + sizes)
    qa, ka, va, qi, ki, wi, cq, ckv, kr, gates = (
        w_in[:, a:a + s] for a, s in zip(starts, sizes))
    qa = qa * (A_HEAD_DIM ** -0.5 * LOG2E)
    hr = QK_ROPE // 2
    kt = jnp.concatenate([-kr[:, hr:], kr[:, :hr]], axis=1)
    half = LANES // 2
    tail = LANES - QK_NOPE - QK_ROPE
    cols = [qa, qi,
            ka, z(half), z(half), ka, va, z(half), ki, z(half), z(half), ki,
            wi, z(LANES - IDX_HEADS),
            z(QK_NOPE), kr, z(tail), z(QK_NOPE), kt, z(tail),
            cq, ckv, gates]
    return jnp.concatenate(cols, axis=1).astype(BF16)


def _layout_w_uq(w_uq):
    scale = (QK_NOPE + QK_ROPE) ** -0.5 * LOG2E
    dh = QK_NOPE + QK_ROPE
    hr = QK_ROPE // 2
    src_plain = np.zeros(B_HEADS * LANES, np.int32)
    mul_plain = np.zeros(B_HEADS * LANES, np.float32)
    src_rot = np.zeros(B_HEADS * LANES, np.int32)
    mul_rot = np.zeros(B_HEADS * LANES, np.float32)
    for h in range(B_HEADS):
        for c in range(dh):
            src_plain[h * LANES + c] = h * dh + c
            mul_plain[h * LANES + c] = scale
        for c in range(QK_ROPE):
            first = c < hr
            src_rot[h * LANES + QK_NOPE + c] = h * dh + QK_NOPE + (c + hr if first else c - hr)
            mul_rot[h * LANES + QK_NOPE + c] = -scale if first else scale
    plain = jnp.take(w_uq, src_plain, axis=1) * mul_plain
    rotw = jnp.take(w_uq, src_rot, axis=1) * mul_rot
    return plain.astype(BF16), rotw.astype(BF16)


def _rope_dense(pos):
    half = QK_ROPE // 2
    assert SUBLANES * half == LANES
    freqs = ROPE_THETA ** (-2.0 * jnp.arange(half, dtype=F32) / QK_ROPE)
    pos_dense = jnp.repeat(pos.reshape(-1, SUBLANES, 1)[:, ::-1, :], half,
                           axis=2).reshape(-1, LANES)
    return pos_dense, jnp.tile(freqs, SUBLANES).reshape(1, LANES)


def _fit_keys_values(q_max, k_max, v_max, k, v_t, batch, per_batch_rows):
    n = k.shape[0]
    tiny = jnp.float32(1e-30)
    q_max, k_max, v_max = (jnp.maximum(m, tiny) for m in (q_max, k_max, v_max))
    c = jnp.sqrt(q_max / k_max)
    k8 = jnp.clip(k.astype(F32) * jnp.repeat(c, n // batch)[:, None], -F8_MAX, F8_MAX).astype(F8)
    v_scale = F8_FIT / v_max
    if per_batch_rows:
        v_scale = jnp.repeat(v_scale, v_t.shape[0] // batch)[:, None]
    else:
        v_scale = jnp.repeat(v_scale, v_t.shape[1] // batch)[None, :]
    ones = (jnp.arange(v_t.shape[0]) % LANES == V_DIM).astype(F32)[:, None]
    v8_t = (v_t.astype(F32) * v_scale + ones).astype(F8)
    return k8, v8_t, jnp.stack([1.0 / c, v_max / F8_FIT])


def _tiles(seq):
    tm = 512 if seq % 512 == 0 else 256
    tq_dsa = 512 if seq % 512 == 0 else 256
    tq_mla = 512 if seq % 512 == 0 else 256
    return tm, tq_dsa, tq_mla


def kernel(x, c, positions, w_ada, b_ada, g_ffn1, w_ffn1_in, w_ffn1_down, g_mix, w_in,
           g_cq, w_uq, g_ckv, w_uk, w_uv, rel_bias, w_o_a, w_o_b, w_out,
           g_ffn2, w_ffn2_in, w_ffn2_down, g_final):
    batch, seq, d = x.shape
    depth = w_ada.shape[0]
    n = batch * seq
    topk = min(TOPK_MAX, seq // 4)
    tm, tq_dsa, tq_mla = _tiles(seq)
    assert seq % tq_dsa == 0 and seq % tm == 0 and d % LANES == 0 and seq <= ZERO_BAND
    assert topk <= GROUPS and tq_dsa % GROUPS == 0
    assert A_HEAD_DIM == IDX_DIM == V_DIM == LANES // 2
    assert depth == 1, "the tail kernel fuses the final norm into the single layer"

    xf = x.reshape(n, d)
    c_pad = jnp.pad(c, ((0, 8 - batch % 8 if batch % 8 else 0), (0, 0)))
    pos = positions.astype(I32)
    posq = pos.reshape(n // tq_dsa, 1, tq_dsa)
    posk = pos.reshape(batch, 1, seq)
    ptile = pos.reshape(n // tq_dsa, tq_dsa)
    pmin, pmax = ptile.min(axis=1), ptile.max(axis=1)
    prun = jnp.all(ptile[:, 1:] - ptile[:, :-1] == 1, axis=1).astype(I32)
    tab = jnp.pad(rel_bias.T.astype(F32) * LOG2E, ((0, 0), (0, LANES - NUM_BUCKETS)))
    far = rel_bias[NUM_BUCKETS // 2 - 1].astype(F32) * LOG2E

    l = 0
    mod = _ada(c_pad, w_ada[l], b_ada[l])[:batch]
    mod3 = mod.reshape(batch, N_ADA, d)
    x1 = _ffn1(xf, mod3, g_ffn1[l].reshape(1, d), w_ffn1_in[l].astype(BF16),
               w_ffn1_down[l].astype(BF16), seq, tm)
    wq, wqr = _layout_w_uq(w_uq[l])
    qa, qi, ska, ski, vat, wi, qb, kb, vbt, gates, amax = _proj(
        x1, mod3, g_mix[l].reshape(1, d), _layout_w_in(w_in[l], d),
        g_cq[l].reshape(1, Q_LORA), wq, wqr, g_ckv[l].reshape(1, KV_LORA),
        _pad_heads(w_uk[l], B_HEADS, QK_NOPE).astype(BF16),
        _pad_heads(w_uv[l], B_HEADS, V_DIM).astype(BF16),
        *_rope_dense(pos), seq, tm)
    bmax = amax.reshape(batch, seq // tm, SUBLANES, LANES)[:, :, :, 0].max(axis=1)
    ska8, vat8, fit_a = _fit_keys_values(bmax[:, 0], bmax[:, 1], bmax[:, 2], ska, vat, batch,
                                         per_batch_rows=False)
    kb8, vbt8, fit_b = _fit_keys_values(bmax[:, 3], bmax[:, 4], bmax[:, 5], kb, vbt, batch,
                                        per_batch_rows=True)
    oa = _dsa(qa, qi, wi, ska8, ski, vat8, posq, posk, tab, far, fit_a, pmin, pmax, prun, batch,
              seq, tq_dsa, topk)
    ob = _mla(qb, kb8, vbt8, fit_b, batch, seq, tq_mla)
    out = _tail(x1, oa, ob, gates, mod3,
                w_o_a[l].astype(BF16), w_o_b[l].astype(BF16),
                w_out[l].astype(BF16), g_ffn2[l].reshape(1, d),
                w_ffn2_in[l].astype(BF16), w_ffn2_down[l].astype(BF16),
                g_final.reshape(1, d), seq, tm)
    return out.reshape(batch, seq, d)
```

```python
import functools
import math

import jax
import jax.numpy as jnp
import numpy as np
from jax import lax
from jax.experimental import pallas as pl
from jax.experimental.pallas import tpu as pltpu

F32 = jnp.float32
BF16 = jnp.bfloat16
I32 = jnp.int32

CHUNK = 64
EPS = 1e-6
A_HEADS = 8
A_HEAD_DIM = 64
IDX_HEADS = 8
IDX_DIM = 64
TOPK_MAX = 256
B_HEADS = 8
QK_NOPE = 64
QK_ROPE = 32
V_DIM = 64
Q_LORA = 384
KV_LORA = 256
ROPE_THETA = 10000.0
NUM_BUCKETS = 32
MAX_DISTANCE = 128
N_ADA = 9

LANES = 128
SUBLANES = 8
VMEM_LIMIT = 56 * 1024 * 1024
INT_MIN = -2 ** 31
INT_MAX = 2 ** 31 - 1
NEG_BIG = -1e30
LOG2E = math.log2(math.e)
F8 = jnp.float8_e4m3fn
P_SHIFT = 8.0
F8_MAX = 448.0
F8_FIT = 224.0
F8_ROWS = 32
PV_ROWS_F8 = -(-(V_DIM + 1) // F8_ROWS) * F8_ROWS

A_Q_W = A_HEADS * A_HEAD_DIM
I_Q_W = IDX_HEADS * IDX_DIM
PAIR_W = 2 * LANES
SM_W = 5 * LANES
MISC_W = 3 * LANES
C_QA = 0
C_QI = C_QA + A_Q_W
C_SM = C_QI + I_Q_W
C_WI = C_SM + SM_W
C_CQ = C_WI + MISC_W
C_CKV = C_CQ + Q_LORA
C_GT = C_CKV + KV_LORA


def _dot(a, b):
    return jnp.dot(a, b, preferred_element_type=F32)


def _dot_nt(a, b):
    return lax.dot_general(a, b, (((1,), (1,)), ((), ())), preferred_element_type=F32)


def _two_phase(n_groups, phase1, phase2, ahead=1):
    for g in range(min(ahead, n_groups)):
        phase1(g)
    for g in range(n_groups):
        if g + ahead < n_groups:
            phase1(g + ahead)
        phase2(g)


def _fit_queries(q, scale):
    return jnp.clip(q.astype(F32) * scale, -F8_MAX, F8_MAX).astype(F8)


def _store_heads(acc_ref, o_ref, heads, unscale):
    def normalised(h):
        a = acc_ref[h]
        return (a * (unscale / a[V_DIM:V_DIM + 1, :])).T[:, 0:V_DIM]

    for h in range(0, heads, 2):
        pair = jnp.concatenate([normalised(h), normalised(h + 1)], axis=1)
        o_ref[:, h * V_DIM:(h + 2) * V_DIM] = pair.astype(BF16)


def _rms(x, g):
    return x * lax.rsqrt(jnp.mean(x * x, axis=-1, keepdims=True) + EPS) * g


def _const_spec(shape):
    nd = len(shape)
    return pl.BlockSpec(shape, lambda *_: (0,) * nd, pipeline_mode=pl.Buffered(1))


def _ada_kernel(c_ref, w_ref, b_ref, o_ref):
    c = c_ref[...]
    s = c * jax.nn.sigmoid(c)
    o_ref[...] = _dot(s.astype(BF16), w_ref[...].astype(BF16)) + b_ref[...]


def _ada(c_pad, w_ada, b_ada):
    d = c_pad.shape[1]
    n_out = w_ada.shape[1]
    return pl.pallas_call(
        _ada_kernel,
        grid=(n_out // d,),
        in_specs=[pl.BlockSpec(c_pad.shape, lambda j: (0, 0)),
                  pl.BlockSpec((d, d), lambda j: (0, j)),
                  pl.BlockSpec((1, d), lambda j: (0, j))],
        out_specs=pl.BlockSpec(c_pad.shape, lambda j: (0, j)),
        out_shape=jax.ShapeDtypeStruct((c_pad.shape[0], n_out), F32),
        compiler_params=pltpu.CompilerParams(vmem_limit_bytes=VMEM_LIMIT),
        name="ada",
    )(c_pad, w_ada, b_ada.reshape(1, n_out))


def _swiglu_tile(hb, win_ref, wdn_ref, d_ff, chunk):
    acc = None
    for j in range(d_ff // chunk):
        g = _dot(hb, win_ref[:, j * chunk:(j + 1) * chunk])
        u = _dot(hb, win_ref[:, d_ff + j * chunk:d_ff + (j + 1) * chunk])
        a = (g * jax.nn.sigmoid(g) * u).astype(BF16)
        part = _dot(a, wdn_ref[j * chunk:(j + 1) * chunk, :])
        acc = part if acc is None else acc + part
    return acc


def _ffn_chunk(d_ff):
    for c in (512, 256, 128):
        if d_ff % c == 0:
            return c
    raise ValueError("d_ff must be a multiple of 128")


def _ffn1_kernel(x_ref, mod_ref, g_ref, win_ref, wdn_ref, o_ref, *, d_ff, chunk):
    x = x_ref[...]
    sh, sc, gt = mod_ref[0, 0:1, :], mod_ref[0, 1:2, :], mod_ref[0, 2:3, :]
    h = _rms(x, g_ref[...]) * (1.0 + sc) + sh
    y = _swiglu_tile(h.astype(BF16), win_ref, wdn_ref, d_ff, chunk)
    o_ref[...] = x + (0.5 * gt) * y


def _ffn1(x2, mod3, g, w_in, w_down, seq, tm):
    n, d = x2.shape
    d_ff = w_down.shape[0]
    tiles_per_batch = seq // tm
    kern = functools.partial(_ffn1_kernel, d_ff=d_ff, chunk=_ffn_chunk(d_ff))
    return pl.pallas_call(
        kern,
        grid=(n // tm,),
        in_specs=[pl.BlockSpec((tm, d), lambda i: (i, 0)),
                  pl.BlockSpec((1, N_ADA, d), lambda i: (i // tiles_per_batch, 0, 0)),
                  _const_spec((1, d)),
                  _const_spec(w_in.shape),
                  _const_spec(w_down.shape)],
        out_specs=pl.BlockSpec((tm, d), lambda i: (i, 0)),
        out_shape=jax.ShapeDtypeStruct((n, d), F32),
        compiler_params=pltpu.CompilerParams(vmem_limit_bytes=VMEM_LIMIT),
        name="ffn1",
    )(x2, mod3, g, w_in, w_down)


def _proj_kernel(x_ref, mod_ref, g_ref, w_ref, gcq_ref, wq_ref, wqr_ref, gckv_ref,
                 wk_ref, wv_ref, pos_ref, freq_ref,
                 qa_ref, qi_ref, ska_ref, ski_ref, vat_ref, wi_ref, qb_ref, kb_ref, vbt_ref, gt_ref,
                 amax_ref, *, d):
    x = x_ref[...]
    sh, sc = mod_ref[0, 3:4, :], mod_ref[0, 4:5, :]
    hb = (_rms(x, g_ref[...]) * (1.0 + sc) + sh).astype(BF16)
    lane = lax.broadcasted_iota(I32, (1, LANES), 1)
    tm = x.shape[0]
    half = QK_ROPE // 2
    ang = pos_ref[...].astype(F32) * freq_ref[...]

    def spread(dense):
        rows = jnp.broadcast_to(dense[:, None, :], (tm // SUBLANES, SUBLANES, LANES))
        top = pltpu.roll(rows.reshape(tm, LANES), 0, 1, stride=half, stride_axis=0)
        first = pltpu.roll(top, QK_NOPE + half, 1)
        return first, pltpu.roll(first, half, 1)

    in_first = (lane >= QK_NOPE) & (lane < QK_NOPE + half)
    in_second = (lane >= QK_NOPE + half) & (lane < QK_NOPE + QK_ROPE)
    c1, c2 = spread(jnp.cos(ang))
    s1, s2 = spread(jnp.sin(ang))
    cs = jnp.where(lane < QK_NOPE, 1.0, jnp.where(in_first, c1, jnp.where(in_second, c2, 0.0)))
    sn = jnp.where(in_first, s1, jnp.where(in_second, s2, 0.0))

    def tile_max(a):
        col = jnp.max(jnp.abs(a), axis=0, keepdims=True)
        return jnp.broadcast_to(jnp.max(col, axis=1, keepdims=True), (1, LANES))

    qa = _dot(hb, w_ref[:, C_QA:C_QA + A_Q_W])
    qa_ref[...] = qa.astype(BF16)
    qi_ref[...] = _dot(hb, w_ref[:, C_QI:C_QI + I_Q_W]).astype(BF16)

    sm = _dot(hb, w_ref[:, C_SM:C_SM + SM_W])
    ska_ref[...] = sm[:, 0:PAIR_W].astype(BF16)
    va = sm[:, PAIR_W:PAIR_W + LANES]
    vat_ref[...] = va.T.astype(BF16)
    ski_ref[...] = sm[:, PAIR_W + LANES:SM_W].astype(BF16)
    maxima = [tile_max(qa), tile_max(sm[:, 0:LANES]), tile_max(va)]

    misc = _dot(hb, w_ref[:, C_WI:C_WI + MISC_W])
    wi_ref[...] = misc[:, 0:LANES]
    kpe = misc[:, LANES:2 * LANES] * cs + misc[:, 2 * LANES:3 * LANES] * sn

    lat = _dot(hb, w_ref[:, C_CQ:C_GT])
    cq = _rms(lat[:, 0:Q_LORA], gcq_ref[...]).astype(BF16)
    ckv = _rms(lat[:, Q_LORA:Q_LORA + KV_LORA], gckv_ref[...]).astype(BF16)
    qn = _dot(cq, wq_ref[...])
    qr = _dot(cq, wqr_ref[...])
    kn = _dot(ckv, wk_ref[...])
    vv = _dot(ckv, wv_ref[...])
    q_abs = k_abs = None
    for h in range(B_HEADS):
        sl = slice(h * LANES, (h + 1) * LANES)
        qh = qn[:, sl] * cs + qr[:, sl] * sn
        kh = kn[:, sl] + kpe
        qb_ref[:, sl] = qh.astype(BF16)
        kb_ref[:, sl] = kh.astype(BF16)
        q_abs = jnp.abs(qh) if q_abs is None else jnp.maximum(q_abs, jnp.abs(qh))
        k_abs = jnp.abs(kh) if k_abs is None else jnp.maximum(k_abs, jnp.abs(kh))
    vbt_ref[...] = vv.T.astype(BF16)
    maxima += [tile_max(q_abs), tile_max(k_abs), tile_max(vv)]
    amax_ref[...] = jnp.concatenate(
        maxima + [jnp.zeros((SUBLANES - len(maxima), LANES), F32)], axis=0)

    gl = _dot(hb, w_ref[:, C_GT:C_GT + 2 * d])
    gt_ref[...] = jax.nn.sigmoid(gl).astype(BF16)


def _proj(x1, mod3, g, w_all, gcq, wq, wqr, gckv, wk, wv, pos_col, freq_row, seq, tm):
    n, d = x1.shape
    tiles_per_batch = seq // tm
    row = lambda w: pl.BlockSpec((tm, w), lambda i: (i, 0))
    slots = B_HEADS * LANES
    n_tiles = n // tm
    outs = [(row(A_Q_W), (n, A_Q_W), BF16),
            (row(I_Q_W), (n, I_Q_W), BF16),
            (row(PAIR_W), (n, PAIR_W), BF16),
            (row(PAIR_W), (n, PAIR_W), BF16),
            (pl.BlockSpec((LANES, tm), lambda i: (0, i)), (LANES, n), BF16),
            (row(LANES), (n, LANES), F32),
            (row(slots), (n, slots), BF16),
            (row(slots), (n, slots), BF16),
            (pl.BlockSpec((slots, tm), lambda i: (i // tiles_per_batch, i % tiles_per_batch)),
             (n // seq * slots, seq), BF16),
            (row(2 * d), (n, 2 * d), BF16),
            (pl.BlockSpec((SUBLANES, LANES), lambda i: (i, 0)),
             (n_tiles * SUBLANES, LANES), F32)]
    return pl.pallas_call(
        functools.partial(_proj_kernel, d=d),
        grid=(n // tm,),
        in_specs=[row(d),
                  pl.BlockSpec((1, N_ADA, d), lambda i: (i // tiles_per_batch, 0, 0)),
                  _const_spec((1, d)), _const_spec(w_all.shape),
                  _const_spec(gcq.shape), _const_spec(wq.shape), _const_spec(wqr.shape),
                  _const_spec(gckv.shape), _const_spec(wk.shape), _const_spec(wv.shape),
                  pl.BlockSpec((tm // SUBLANES, LANES), lambda i: (i, 0)),
                  _const_spec((1, LANES))],
        out_specs=[o[0] for o in outs],
        out_shape=[jax.ShapeDtypeStruct(o[1], o[2]) for o in outs],
        compiler_params=pltpu.CompilerParams(vmem_limit_bytes=VMEM_LIMIT),
        name="proj",
    )(x1, mod3, g, w_all, gcq, wq, wqr, gckv, wk, wv, pos_col, freq_row)


DSA_HEAD_GROUP = 2
DSA_AHEAD = 1
DSA_SLOTS = (DSA_AHEAD + 1) * DSA_HEAD_GROUP
ZERO_BAND = 1 << 20
GROUPS = 256
COUNT_UNKNOWN = 1 << 30
FINISH_STEPS = 2


def _rowsum8(x):
    v = x.reshape(x.shape[0] // SUBLANES, SUBLANES, x.shape[1])
    while v.shape[0] > 1:
        half = v.shape[0] // 2
        v = v[:half] + v[half:]
    return v[0]


def _t5_bias_tiles(posk, posq, tab_ref, heads):
    return _t5_bias_of_rel(posk - posq, tab_ref, heads)


def _t5_bias_of_rel(rel, tab_ref, heads):
    half = NUM_BUCKETS // 2
    max_exact = half // 2
    n = jnp.abs(rel)
    nf = jnp.maximum(n, 1).astype(F32)
    large = max_exact + (jnp.log(nf / max_exact) / math.log(MAX_DISTANCE / max_exact)
                         * (half - max_exact)).astype(I32)
    large = jnp.minimum(large, half - 1)
    bucket = jnp.where(rel > 0, half, 0) + jnp.where(n < max_exact, n, large)
    tk, tq = bucket.shape
    out = []
    for h in range(heads):
        tab = jnp.broadcast_to(tab_ref[h:h + 1, :], (tk, LANES))
        cols = [jnp.take_along_axis(tab, bucket[:, c * LANES:(c + 1) * LANES], axis=1,
                                    mode="promise_in_bounds")
                for c in range(tq // LANES)]
        out.append(jnp.concatenate(cols, axis=1))
    return out


def _dsa_kernel(pmin_ref, pmax_ref, prun_ref,
                qa_ref, qi_ref, wi_ref, ska_ref, ski_ref, vat_ref, posq_ref, posk_ref, tab_ref,
                far_ref, fit_ref,
                o_ref,
                keys_ref, gmax_ref, m_ref, acc_ref, s_ref, q8_ref, *, tq, topk, nq):
    b = pl.program_id(0)
    i = pl.program_id(1)
    tk = tq
    n_tiles = i + 1

    w_t = wi_ref[...].T[0:IDX_HEADS, :]
    key_l = lax.broadcasted_iota(I32, (tk, tq), 0)
    q_l = lax.broadcasted_iota(I32, (tk, tq), 1)
    diag_ok = key_l < (q_l // CHUNK + 1) * CHUNK
    not_key_l = ~key_l

    def score_tile(j, diag):
        r0 = pl.multiple_of(j * tk, tk)
        ki_lo = ski_ref[pl.ds(r0, tk), 0:128]
        ki_hi = ski_ref[pl.ds(r0, tk), 128:256]
        score = None
        for p in range(IDX_HEADS // 2):
            qp = qi_ref[:, p * LANES:(p + 1) * LANES]
            for h, kk in ((2 * p, ki_lo), (2 * p + 1, ki_hi)):
                t = w_t[h:h + 1, :] * jnp.maximum(_dot_nt(kk, qp), 0.0)
                score = t if score is None else score + t
        score = score + 0.0
        bits = lax.bitcast_convert_type(score, I32)
        key = jnp.where(bits < 0, (bits ^ 0x7FFFFFFF) - ZERO_BAND, bits)
        key = jnp.where(score == 0.0, not_key_l - j * tk, key)
        if diag:
            key = jnp.where(diag_ok, key, INT_MIN)
        keys_ref[pl.ds(r0, tk), :] = key
        gmax_ref[...] = jnp.maximum(gmax_ref[...],
                                    jnp.max(key.reshape(tk // GROUPS, GROUPS, tq), axis=0))

    gmax_ref[...] = jnp.full(gmax_ref.shape, INT_MIN, I32)
    lax.fori_loop(0, i, lambda j, c: (score_tile(j, False), c)[1], 0)
    score_tile(i, True)

    def count_tiles(pred):
        def body(j, acc):
            r0 = pl.multiple_of(j * tk, tk)
            return acc + _rowsum8(jnp.where(pred(keys_ref[pl.ds(r0, tk), :], j), 1, 0))
        acc = lax.fori_loop(0, n_tiles, body, jnp.zeros((SUBLANES, tq), I32))
        return jnp.sum(acc, axis=0, keepdims=True)

    q_row = lax.broadcasted_iota(I32, (1, tq), 1)
    n_adm = i * tq + (q_row // CHUNK + 1) * CHUNK

    gmax = gmax_ref[...]
    top = jnp.max(gmax, axis=0, keepdims=True)
    low = jnp.min(gmax, axis=0, keepdims=True)
    cnt_pos = count_tiles(lambda k_, j: k_ >= 1)
    cnt_zero = count_tiles(lambda k_, j: k_ >= -ZERO_BAND)
    few = n_adm <= topk
    in_pos = cnt_pos >= topk
    in_zero = jnp.logical_not(in_pos) & (cnt_zero >= topk)
    base_lo = jnp.where(in_pos, 1, jnp.where(in_zero, -ZERO_BAND, INT_MIN + 1))
    base_cnt = jnp.where(in_pos, cnt_pos, jnp.where(in_zero, cnt_zero, n_adm))
    hi0 = jnp.where(in_pos, top + 1, jnp.where(in_zero, 1, -ZERO_BAND))
    tighter = low > base_lo
    lo0 = jnp.where(few, INT_MIN + 1, jnp.where(tighter, low, base_lo))
    cnt0 = jnp.where(few, topk, jnp.where(tighter, COUNT_UNKNOWN, base_cnt))
    hi0 = jnp.where(few, INT_MIN + 2, hi0)

    def any_row(pred):
        return jnp.max(jnp.where(pred, 1, 0))

    def bisect(lo, hi, cnt_lo, slack):
        def pending(lo, hi, cnt_lo):
            return any_row((cnt_lo - topk > slack) & (hi - lo > 1))

        def step(st):
            it, lo, hi, cnt_lo, _ = st
            mid = lo + ((hi - lo) >> 1)
            cnt = count_tiles(lambda k_, j: k_ >= mid)
            take = cnt >= topk
            lo = jnp.where(take, mid, lo)
            hi = jnp.where(take, hi, mid)
            cnt_lo = jnp.where(take, cnt, cnt_lo)
            return it + 1, lo, hi, cnt_lo, pending(lo, hi, cnt_lo)

        return lax.while_loop(lambda st: (st[0] < 34) & (st[4] > 0), step,
                              (jnp.int32(0), lo, hi, cnt_lo, pending(lo, hi, cnt_lo)))

    _, lo1, hi1, cnt1, _ = bisect(lo0, hi0, cnt0, FINISH_STEPS)

    def surplus(lo, cnt_lo):
        return (cnt_lo > topk) & (cnt_lo - topk <= FINISH_STEPS) & (hi1 - lo > 1)

    def drop_step(st):
        it, lo, cnt_lo, _ = st
        def body(j, acc):
            r0 = pl.multiple_of(j * tk, tk)
            k_ = keys_ref[pl.ds(r0, tk), :]
            cand = jnp.where(k_ >= lo, k_, INT_MAX)
            v = cand.reshape(tk // SUBLANES, SUBLANES, tq)
            while v.shape[0] > 1:
                half = v.shape[0] // 2
                v = jnp.minimum(v[:half], v[half:])
            return jnp.minimum(acc, v[0])
        acc = lax.fori_loop(0, n_tiles, body, jnp.full((SUBLANES, tq), INT_MAX, I32))
        smallest = jnp.min(acc, axis=0, keepdims=True)
        go = surplus(lo, cnt_lo)
        lo = jnp.where(go, smallest + 1, lo)
        cnt_lo = jnp.where(go, cnt_lo - 1, cnt_lo)
        return it + 1, lo, cnt_lo, any_row(surplus(lo, cnt_lo))

    _, lo2, _, _ = lax.while_loop(lambda st: (st[0] < FINISH_STEPS) & (st[3] > 0), drop_step,
                                  (jnp.int32(0), lo1, cnt1, any_row(surplus(lo1, cnt1))))

    cnt2 = count_tiles(lambda k_, j: k_ >= lo2)
    redo = (cnt2 != topk) & jnp.logical_not(few)
    its, tau, _, _, _ = bisect(jnp.where(redo, lo1, lo2), jnp.where(redo, hi1, lo2 + 1),
                               jnp.where(redo, cnt1, topk), 0)
    cnt_ge = lax.cond(its > 0, lambda: count_tiles(lambda k_, j: k_ >= tau), lambda: cnt2)
    over = (cnt_ge > topk) & jnp.logical_not(few)
    tau_sel = tau

    @pl.when(jnp.max(jnp.where(over, 1, 0)) > 0)
    def _ties():
        cnt_gt = count_tiles(lambda k_, j: k_ > tau)
        room = topk - cnt_gt

        def idx_step(p, jmax):
            cand = jmax | jnp.left_shift(jnp.int32(1), 14 - p)
            cnt = count_tiles(lambda k_, j: (k_ == tau) & (key_l + j * tk < cand))
            return jnp.where(cnt <= room, cand, jmax)

        jmax = lax.fori_loop(0, 15, idx_step, jnp.zeros((1, tq), I32))

        def demote(j, carry):
            r0 = pl.multiple_of(j * tk, tk)
            k_ = keys_ref[pl.ds(r0, tk), :]
            drop = (k_ == tau) & (key_l + j * tk >= jmax) & over
            keys_ref[pl.ds(r0, tk), :] = jnp.where(drop, k_ - 1, k_)
            return carry

        lax.fori_loop(0, n_tiles, demote, 0)

    m_ref[...] = jnp.full(m_ref.shape, NEG_BIG, F32)
    acc_ref[...] = jnp.zeros(acc_ref.shape, F32)
    posq = posq_ref[0]
    q8_ref[...] = _fit_queries(qa_ref[...], fit_ref[0, b])

    def attn_tile(j, mode):
        near = mode != "far"
        r0 = pl.multiple_of(j * tk, tk)
        ka_lo = ska_ref[pl.ds(r0, tk), 0:128]
        ka_hi = ska_ref[pl.ds(r0, tk), 128:256]
        va_t = vat_ref[0:PV_ROWS_F8, pl.ds(r0, tk)]
        maskb = jnp.where(keys_ref[pl.ds(r0, tk), :] >= tau_sel, 0.0, NEG_BIG)
        if mode == "any":
            posk_row = lax.bitcast_convert_type(posk_ref[0, :, pl.ds(r0, tk)], F32)
            posk_col = lax.bitcast_convert_type(
                jnp.broadcast_to(posk_row, (LANES, tk)).T, I32)
            bias = _t5_bias_tiles(jnp.tile(posk_col, (1, tq // LANES)), posq, tab_ref, A_HEADS)
        elif mode == "run":
            nb = tq // LANES
            shift_max = (nb - 1) * LANES
            d0 = pmin_ref[b * nq + j] - pmin_ref[b * nq + i]
            x = lax.broadcasted_iota(I32, (tk + shift_max, LANES), 0)
            l = lax.broadcasted_iota(I32, (tk + shift_max, LANES), 1)
            panel = _t5_bias_of_rel(x - l + (d0 - shift_max), tab_ref, A_HEADS)
            bias = [jnp.concatenate([p[shift_max - cb * LANES:shift_max - cb * LANES + tk]
                                     for cb in range(nb)], axis=1) for p in panel]
        m_prev = m_ref[...]
        m_next = [None] * A_HEADS
        slot = lambda h: h % DSA_SLOTS

        def logits(g):
            for h in range(g * DSA_HEAD_GROUP, (g + 1) * DSA_HEAD_GROUP):
                s = _dot_nt(ka_hi if h % 2 else ka_lo,
                            q8_ref[:, (h // 2) * LANES:(h // 2 + 1) * LANES])
                s = s + ((bias[h] + maskb) if near else maskb)
                s_ref[slot(h)] = s
                s_max = jnp.max(s, axis=0, keepdims=True)
                m_next[h] = jnp.maximum(m_prev[h:h + 1, :],
                                        s_max if near else s_max + far_ref[h])

        def weighted_values(g):
            for h in range(g * DSA_HEAD_GROUP, (g + 1) * DSA_HEAD_GROUP):
                alpha = jnp.exp2(m_prev[h:h + 1, :] - m_next[h])
                shift = m_next[h] - (P_SHIFT if near else P_SHIFT + far_ref[h])
                pr = jnp.exp2(s_ref[slot(h)] - shift).astype(F8)
                acc_ref[h, 0:PV_ROWS_F8, :] = (alpha * acc_ref[h, 0:PV_ROWS_F8, :]
                                               + _dot(va_t, pr))

        _two_phase(A_HEADS // DSA_HEAD_GROUP, logits, weighted_values, ahead=DSA_AHEAD)
        m_ref[...] = jnp.concatenate(m_next, axis=0)

    def attn_body(j, carry):
        is_far = pmax_ref[b * nq + j] - pmin_ref[b * nq + i] <= -MAX_DISTANCE
        is_run = (prun_ref[b * nq + j] == 1) & (prun_ref[b * nq + i] == 1)
        lax.cond(is_far, lambda: attn_tile(j, "far"),
                 lambda: lax.cond(is_run, lambda: attn_tile(j, "run"),
                                  lambda: attn_tile(j, "any")))
        return carry

    lax.fori_loop(0, n_tiles, attn_body, 0)
    _store_heads(acc_ref, o_ref, A_HEADS, fit_ref[1, b])


def _dsa(qa, qi, wi, ska, ski, vat, posq, posk, tab, far, fit, pmin, pmax, prun, batch, seq,
         tq, topk):
    n = qa.shape[0]
    nq = seq // tq
    row = lambda w: pl.BlockSpec((tq, w), lambda b, i, *_: (b * nq + i, 0))
    grid_spec = pltpu.PrefetchScalarGridSpec(
        num_scalar_prefetch=3,
        grid=(batch, nq),
        in_specs=[row(A_Q_W), row(I_Q_W), row(LANES),
                  pl.BlockSpec((seq, PAIR_W), lambda b, i, *_: (b, 0),
                               pipeline_mode=pl.Buffered(1)),
                  pl.BlockSpec((seq, PAIR_W), lambda b, i, *_: (b, 0),
                               pipeline_mode=pl.Buffered(1)),
                  pl.BlockSpec((LANES, seq), lambda b, i, *_: (0, b)),
                  pl.BlockSpec((1, 1, tq), lambda b, i, *_: (b * nq + i, 0, 0)),
                  pl.BlockSpec((1, 1, seq), lambda b, i, *_: (b, 0, 0)),
                  pl.BlockSpec((A_HEADS, LANES), lambda b, i, *_: (0, 0)),
                  pl.BlockSpec(memory_space=pltpu.SMEM),
                  pl.BlockSpec(memory_space=pltpu.SMEM)],
        out_specs=row(A_HEADS * V_DIM),
        scratch_shapes=[pltpu.VMEM((seq, tq), I32),
                        pltpu.VMEM((GROUPS, tq), I32),
                        pltpu.VMEM((A_HEADS, tq), F32),
                        pltpu.VMEM((A_HEADS, LANES, tq), F32),
                        pltpu.VMEM((DSA_SLOTS, tq, tq), F32),
                        pltpu.VMEM((tq, A_Q_W), F8)],
    )
    return pl.pallas_call(
        functools.partial(_dsa_kernel, tq=tq, topk=topk, nq=nq),
        grid_spec=grid_spec,
        out_shape=jax.ShapeDtypeStruct((n, A_HEADS * V_DIM), BF16),
        compiler_params=pltpu.CompilerParams(vmem_limit_bytes=VMEM_LIMIT),
        name="dsa",
    )(pmin, pmax, prun, qa, qi, wi, ska, ski, vat, posq, posk, tab, far, fit)


MLA_HEAD_GROUP = 8
MLA_AHEAD = 2


def _mla_kernel(q_ref, k_ref, vt_ref, fit_ref, o_ref, acc_ref, s_ref, q8_ref, *, tq):
    b = pl.program_id(0)
    i = pl.program_id(2)
    tk = tq
    hg = MLA_HEAD_GROUP
    acc_ref[...] = jnp.zeros(acc_ref.shape, F32)
    q8_ref[...] = _fit_queries(q_ref[...], fit_ref[0, b])

    ahead = MLA_AHEAD
    key_l = lax.broadcasted_iota(I32, (tk, tq), 0)
    q_l = lax.broadcasted_iota(I32, (tk, tq), 1)
    ok = key_l < (q_l // CHUNK + 1) * CHUNK
    head = lambda h: slice(h * LANES, (h + 1) * LANES)

    def logits(j, h, masked):
        r0 = j * tk if isinstance(j, int) else pl.multiple_of(j * tk, tk)
        s = _dot_nt(k_ref[pl.ds(r0, tk), head(h)], q8_ref[:, head(h)])
        if masked is True:
            s = jnp.where(ok, s, NEG_BIG)
        elif masked is not False:
            s = jnp.where(ok | jnp.logical_not(masked), s, NEG_BIG)
        s_ref[h] = s
        return jnp.max(s, axis=0, keepdims=True)

    def weighted_values(j, h, m_prev, s_max):
        r0 = pl.multiple_of(j * tk, tk)
        m_next = jnp.maximum(m_prev, s_max)
        alpha = jnp.exp2(m_prev - m_next)
        pr = jnp.exp2(s_ref[h] - (m_next - P_SHIFT)).astype(F8)
        v_t = vt_ref[h * LANES:h * LANES + PV_ROWS_F8, pl.ds(r0, tk)]
        acc_ref[h, 0:PV_ROWS_F8, :] = alpha * acc_ref[h, 0:PV_ROWS_F8, :] + _dot(v_t, pr)
        return m_next

    def tile(j, ms, lead, last, next_last=False):
        s_max = list(lead) + [None] * (hg - ahead)
        nxt = []
        out = []
        for h in range(hg):
            if h + ahead < hg:
                s_max[h + ahead] = logits(j, h + ahead, last)
            elif not last:
                nxt.append(logits(j + 1, h + ahead - hg, next_last))
            out.append(weighted_values(j, h, ms[h], s_max[h]))
        return tuple(out), tuple(nxt)

    m0 = tuple(jnp.full((1, tq), NEG_BIG, F32) for _ in range(hg))
    lead0 = tuple(logits(0, h, i == 0) for h in range(ahead))
    st = lax.fori_loop(0, jnp.maximum(i - 1, 0),
                       lambda j, st: tile(j, st[0], st[1], False), (m0, lead0))
    ms, lead = lax.cond(i >= 1, lambda: tile(i - 1, st[0], st[1], False, True), lambda: st)
    tile(i, ms, lead, True)
    _store_heads(acc_ref, o_ref, hg, fit_ref[1, b])


def _mla(qb, kb, vbt, fit, batch, seq, tq):
    n = qb.shape[0]
    nq = seq // tq
    hg = MLA_HEAD_GROUP
    ng = B_HEADS // hg
    w = hg * LANES
    return pl.pallas_call(
        functools.partial(_mla_kernel, tq=tq),
        grid=(batch, ng, nq),
        in_specs=[pl.BlockSpec((tq, w), lambda b, g, i: (b * nq + i, g)),
                  pl.BlockSpec((seq, w), lambda b, g, i: (b, g)),
                  pl.BlockSpec((w, seq), lambda b, g, i: (b * ng + g, 0)),
                  pl.BlockSpec(memory_space=pltpu.SMEM)],
        out_specs=pl.BlockSpec((tq, hg * V_DIM), lambda b, g, i: (b * nq + i, g)),
        out_shape=jax.ShapeDtypeStruct((n, B_HEADS * V_DIM), BF16),
        scratch_shapes=[pltpu.VMEM((hg, LANES, tq), F32), pltpu.VMEM((hg, tq, tq), F32),
                        pltpu.VMEM((tq, w), F8)],
        compiler_params=pltpu.CompilerParams(vmem_limit_bytes=VMEM_LIMIT),
        name="mla",
    )(qb, kb, vbt, fit)


def _tail_kernel(x_ref, oa_ref, ob_ref, gt_ref, mod_ref, woa_ref, wob_ref, wout_ref,
                 g2_ref, win_ref, wdn_ref, gf_ref, o_ref, *, d, d_ff, chunk):
    x = x_ref[...]
    gt2 = mod_ref[0, 5:6, :]
    sh3, sc3, gt3 = mod_ref[0, 6:7, :], mod_ref[0, 7:8, :], mod_ref[0, 8:9, :]
    ya = _dot(oa_ref[...], woa_ref[...])
    yb = _dot(ob_ref[...], wob_ref[...])
    y = gt_ref[:, 0:d].astype(F32) * ya + gt_ref[:, d:2 * d].astype(F32) * yb
    x = x + gt2 * _dot(y.astype(BF16), wout_ref[...])
    h = _rms(x, g2_ref[...]) * (1.0 + sc3) + sh3
    f = _swiglu_tile(h.astype(BF16), win_ref, wdn_ref, d_ff, chunk)
    x = x + (0.5 * gt3) * f
    o_ref[...] = _rms(x, gf_ref[...])


def _tail(x1, oa, ob, gates, mod3, woa, wob, wout, g2, w_in, w_down, gf, seq, tm):
    n, d = x1.shape
    d_ff = w_down.shape[0]
    tiles_per_batch = seq // tm
    row = lambda w: pl.BlockSpec((tm, w), lambda i: (i, 0))
    kern = functools.partial(_tail_kernel, d=d, d_ff=d_ff, chunk=_ffn_chunk(d_ff))
    return pl.pallas_call(
        kern,
        grid=(n // tm,),
        in_specs=[row(d), row(oa.shape[1]), row(ob.shape[1]), row(2 * d),
                  pl.BlockSpec((1, N_ADA, d), lambda i: (i // tiles_per_batch, 0, 0)),
                  _const_spec(woa.shape), _const_spec(wob.shape), _const_spec(wout.shape),
                  _const_spec((1, d)), _const_spec(w_in.shape), _const_spec(w_down.shape),
                  _const_spec((1, d))],
        out_specs=row(d),
        out_shape=jax.ShapeDtypeStruct((n, d), F32),
        compiler_params=pltpu.CompilerParams(vmem_limit_bytes=VMEM_LIMIT),
        name="tail",
    )(x1, oa, ob, gates, mod3, woa, wob, wout, g2, w_in, w_down, gf)


def _pad_heads(w, heads, width):
    k = w.shape[0]
    w3 = w.reshape(k, heads, width)
    return jnp.pad(w3, ((0, 0), (0, 0), (0, LANES - width))).reshape(k, heads * LANES)


def _layout_w_in(w_in, d):
    z = lambda c: jnp.zeros((d, c), w_in.dtype)
    sizes = (A_Q_W, A_HEAD_DIM, A_HEAD_DIM, I_Q_W, IDX_DIM, IDX_HEADS, Q_LORA, KV_LORA,
             QK_ROPE, 2 * d)
    starts = np.cumsum((0,) + sizes)
    qa, ka, va, qi, ki, wi, cq, ckv, kr, gates = (
        w_in[:, a:a + s] for a, s in zip(starts, sizes))
    qa = qa * (A_HEAD_DIM ** -0.5 * LOG2E)
    hr = QK_ROPE // 2
    kt = jnp.concatenate([-kr[:, hr:], kr[:, :hr]], axis=1)
    half = LANES // 2
    tail = LANES - QK_NOPE - QK_ROPE
    cols = [qa, qi,
            ka, z(half), z(half), ka, va, z(half), ki, z(half), z(half), ki,
            wi, z(LANES - IDX_HEADS),
            z(QK_NOPE), kr, z(tail), z(QK_NOPE), kt, z(tail),
            cq, ckv, gates]
    return jnp.concatenate(cols, axis=1).astype(BF16)


def _layout_w_uq(w_uq):
    scale = (QK_NOPE + QK_ROPE) ** -0.5 * LOG2E
    dh = QK_NOPE + QK_ROPE
    hr = QK_ROPE // 2
    src_plain = np.zeros(B_HEADS * LANES, np.int32)
    mul_plain = np.zeros(B_HEADS * LANES, np.float32)
    src_rot = np.zeros(B_HEADS * LANES, np.int32)
    mul_rot = np.zeros(B_HEADS * LANES, np.float32)
    for h in range(B_HEADS):
        for c in range(dh):
            src_plain[h * LANES + c] = h * dh + c
            mul_plain[h * LANES + c] = scale
        for c in range(QK_ROPE):
            first = c < hr
            src_rot[h * LANES + QK_NOPE + c] = h * dh + QK_NOPE + (c + hr if first else c - hr)
            mul_rot[h * LANES + QK_NOPE + c] = -scale if first else scale
    plain = jnp.take(w_uq, src_plain, axis=1) * mul_plain
    rotw = jnp.take(w_uq, src_rot, axis=1) * mul_rot
    return plain.astype(BF16), rotw.astype(BF16)


def _rope_dense(pos):
    half = QK_ROPE // 2
    assert SUBLANES * half == LANES
    freqs = ROPE_THETA ** (-2.0 * jnp.arange(half, dtype=F32) / QK_ROPE)
    pos_dense = jnp.repeat(pos.reshape(-1, SUBLANES, 1)[:, ::-1, :], half,
                           axis=2).reshape(-1, LANES)
    return pos_dense, jnp.tile(freqs, SUBLANES).reshape(1, LANES)


def _fit_keys_values(q_max, k_max, v_max, k, v_t, batch, per_batch_rows):
    n = k.shape[0]
    tiny = jnp.float32(1e-30)
    q_max, k_max, v_max = (jnp.maximum(m, tiny) for m in (q_max, k_max, v_max))
    c = jnp.sqrt(q_max / k_max)
    k8 = jnp.clip(k.astype(F32) * jnp.repeat(c, n // batch)[:, None], -F8_MAX, F8_MAX).astype(F8)
    v_scale = F8_FIT / v_max
    if per_batch_rows:
        v_scale = jnp.repeat(v_scale, v_t.shape[0] // batch)[:, None]
    else:
        v_scale = jnp.repeat(v_scale, v_t.shape[1] // batch)[None, :]
    ones = (jnp.arange(v_t.shape[0]) % LANES == V_DIM).astype(F32)[:, None]
    v8_t = (v_t.astype(F32) * v_scale + ones).astype(F8)
    return k8, v8_t, jnp.stack([1.0 / c, v_max / F8_FIT])


def _tiles(seq):
    tm = 512 if seq % 512 == 0 else 256
    tq_dsa = 512 if seq % 512 == 0 else 256
    tq_mla = 512 if seq % 512 == 0 else 256
    return tm, tq_dsa, tq_mla


def kernel(x, c, positions, w_ada, b_ada, g_ffn1, w_ffn1_in, w_ffn1_down, g_mix, w_in,
           g_cq, w_uq, g_ckv, w_uk, w_uv, rel_bias, w_o_a, w_o_b, w_out,
           g_ffn2, w_ffn2_in, w_ffn2_down, g_final):
    batch, seq, d = x.shape
    depth = w_ada.shape[0]
    n = batch * seq
    topk = min(TOPK_MAX, seq // 4)
    tm, tq_dsa, tq_mla = _tiles(seq)
    assert seq % tq_dsa == 0 and seq % tm == 0 and d % LANES == 0 and seq <= ZERO_BAND
    assert topk <= GROUPS and tq_dsa % GROUPS == 0
    assert A_HEAD_DIM == IDX_DIM == V_DIM == LANES // 2
    assert depth == 1, "the tail kernel fuses the final norm into the single layer"

    xf = x.reshape(n, d)
    c_pad = jnp.pad(c, ((0, 8 - batch % 8 if batch % 8 else 0), (0, 0)))
    pos = positions.astype(I32)
    posq = pos.reshape(n // tq_dsa, 1, tq_dsa)
    posk = pos.reshape(batch, 1, seq)
    ptile = pos.reshape(n // tq_dsa, tq_dsa)
    pmin, pmax = ptile.min(axis=1), ptile.max(axis=1)
    prun = jnp.all(ptile[:, 1:] - ptile[:, :-1] == 1, axis=1).astype(I32)
    tab = jnp.pad(rel_bias.T.astype(F32) * LOG2E, ((0, 0), (0, LANES - NUM_BUCKETS)))
    far = rel_bias[NUM_BUCKETS // 2 - 1].astype(F32) * LOG2E

    l = 0
    mod = _ada(c_pad, w_ada[l], b_ada[l])[:batch]
    mod3 = mod.reshape(batch, N_ADA, d)
    x1 = _ffn1(xf, mod3, g_ffn1[l].reshape(1, d), w_ffn1_in[l].astype(BF16),
               w_ffn1_down[l].astype(BF16), seq, tm)
    wq, wqr = _layout_w_uq(w_uq[l])
    qa, qi, ska, ski, vat, wi, qb, kb, vbt, gates, amax = _proj(
        x1, mod3, g_mix[l].reshape(1, d), _layout_w_in(w_in[l], d),
        g_cq[l].reshape(1, Q_LORA), wq, wqr, g_ckv[l].reshape(1, KV_LORA),
        _pad_heads(w_uk[l], B_HEADS, QK_NOPE).astype(BF16),
        _pad_heads(w_uv[l], B_HEADS, V_DIM).astype(BF16),
        *_rope_dense(pos), seq, tm)
    bmax = amax.reshape(batch, seq // tm, SUBLANES, LANES)[:, :, :, 0].max(axis=1)
    ska8, vat8, fit_a = _fit_keys_values(bmax[:, 0], bmax[:, 1], bmax[:, 2], ska, vat, batch,
                                         per_batch_rows=False)
    kb8, vbt8, fit_b = _fit_keys_values(bmax[:, 3], bmax[:, 4], bmax[:, 5], kb, vbt, batch,
                                        per_batch_rows=True)
    oa = _dsa(qa, qi, wi, ska8, ski, vat8, posq, posk, tab, far, fit_a, pmin, pmax, prun, batch,
              seq, tq_dsa, topk)
    ob = _mla(qb, kb8, vbt8, fit_b, batch, seq, tq_mla)
    out = _tail(x1, oa, ob, gates, mod3,
                w_o_a[l].astype(BF16), w_o_b[l].astype(BF16),
                w_out[l].astype(BF16), g_ffn2[l].reshape(1, d),
                w_ffn2_in[l].astype(BF16), w_ffn2_down[l].astype(BF16),
                g_final.reshape(1, d), seq, tm)
    return out.reshape(batch, seq, d)
```
